```python
import math
import jax, jax.numpy as jnp
from jax import lax
import numpy as np

D_MODEL = 1024
BATCH = 8
SEQ = 4096
DEPTH = 2
DEC_BATCH = 8
DEC_SEQ = 64
PAST_LEN = 2048

CHUNK = 64
N_HEADS = 8
N_KV_HEADS = 2
HEAD_DIM = 64
GROUP = N_HEADS // N_KV_HEADS
ATT_WIDTH = N_HEADS * HEAD_DIM
KV_WIDTH = N_KV_HEADS * HEAD_DIM
WINDOW = 128
WINDOW_CHUNKS = WINDOW // CHUNK
BAND = (WINDOW_CHUNKS + 1) * CHUNK
CONV_WIDTH = 512
CONV_K = 3
NUM_BUCKETS = 32
MAX_DISTANCE = 128
N_EXPERTS = 32
TOP_K = 4
D_FF = 1024
SWIGLU_LIMIT = 7.0
SWIGLU_ALPHA = 1.702
MOE_BLOCK = 256
LN_EPS = 1e-5
NEG_INF = -1e30
DEEPNORM_ALPHA = (2 * DEPTH) ** 0.25
DEEPNORM_BETA = (8 * DEPTH) ** -0.25
IN_SIZES = (ATT_WIDTH, KV_WIDTH, KV_WIDTH, CONV_WIDTH, CONV_WIDTH, CONV_WIDTH, D_MODEL, D_MODEL)
IN_WIDTH = sum(IN_SIZES)
IN_SPLITS = tuple(int(s) for s in np.cumsum(IN_SIZES)[:-1])

kernel_name = "chunk_causal_swa_shortconv_moe_step"


def _layernorm(x, g, b):
    xf = x.astype(jnp.float32)
    mu = jnp.mean(xf, axis=-1, keepdims=True)
    var = jnp.mean(jnp.square(xf - mu), axis=-1, keepdims=True)
    return ((xf - mu) * lax.rsqrt(var + LN_EPS)).astype(x.dtype) * g + b


def _rel_bucket(rel):
    half = NUM_BUCKETS // 2
    max_exact = half // 2
    n = jnp.abs(rel)
    n_f = jnp.maximum(n, 1).astype(jnp.float32)
    large = max_exact + (jnp.log(n_f / max_exact) / math.log(MAX_DISTANCE / max_exact)
                         * (half - max_exact)).astype(jnp.int32)
    large = jnp.minimum(large, half - 1)
    return jnp.where(rel > 0, half, 0) + jnp.where(n < max_exact, n, large)


def _attn_bias(rel, rel_table):
    q_len, k_len = rel.shape
    b = rel_table[_rel_bucket(rel)]
    return jnp.transpose(b, (2, 0, 1)).reshape(N_KV_HEADS, GROUP, q_len, k_len)


def _sink_softmax(s, sink):
    sink_b = jnp.broadcast_to(sink.astype(jnp.float32)[:, :, None, None], s.shape[:-1] + (1,))
    p = jax.nn.softmax(jnp.concatenate([s, sink_b], axis=-1), axis=-1)
    return p[..., :-1]


def _attn_prompt(q, k, v, sink, rel_table):
    b, t, _ = q.shape
    nc = t // CHUNK
    q = q.reshape(b, nc, CHUNK, N_KV_HEADS, GROUP, HEAD_DIM)
    k = k.reshape(b, nc, CHUNK, N_KV_HEADS, HEAD_DIM)
    v = v.reshape(b, nc, CHUNK, N_KV_HEADS, HEAD_DIM)
    pad = ((0, 0), (WINDOW_CHUNKS, 0), (0, 0), (0, 0), (0, 0))
    kp, vp = jnp.pad(k, pad), jnp.pad(v, pad)
    kb = jnp.concatenate([kp[:, w:w + nc] for w in range(WINDOW_CHUNKS + 1)], axis=2)
    vb = jnp.concatenate([vp[:, w:w + nc] for w in range(WINDOW_CHUNKS + 1)], axis=2)
    rel = jnp.arange(BAND)[None, :] - WINDOW - jnp.arange(CHUNK)[:, None]
    bias = _attn_bias(rel, rel_table)
    valid = (jnp.arange(nc)[:, None] * CHUNK + jnp.arange(BAND)[None, :] - WINDOW) >= 0
    s = jnp.einsum('bnqkgd,bnskd->bnkgqs', q, kb, preferred_element_type=jnp.float32) * (HEAD_DIM ** -0.5)
    s = jnp.where(valid[None, :, None, None, None, :], s + bias[None, None], NEG_INF)
    p = _sink_softmax(s, sink.reshape(N_KV_HEADS, GROUP))
    o = jnp.einsum('bnkgqs,bnskd->bnqkgd', p.astype(vb.dtype), vb)
    return o.reshape(b, t, ATT_WIDTH)


def _attn_sample(q, k_new, v_new, k_cache, v_cache, sink, rel_table):
    b, s_len, _ = q.shape
    w = k_cache.shape[1]
    k_all = jnp.concatenate([k_cache, k_new.reshape(b, s_len, N_KV_HEADS, HEAD_DIM)], axis=1)
    v_all = jnp.concatenate([v_cache, v_new.reshape(b, s_len, N_KV_HEADS, HEAD_DIM)], axis=1)
    rel = jnp.arange(w + s_len)[None, :] - w - jnp.arange(s_len)[:, None]
    bias = _attn_bias(rel, rel_table)
    qh = q.reshape(b, s_len, N_KV_HEADS, GROUP, HEAD_DIM)
    s = jnp.einsum('bqkgd,bskd->bkgqs', qh, k_all, preferred_element_type=jnp.float32) * (HEAD_DIM ** -0.5) + bias[None]
    p = _sink_softmax(s, sink.reshape(N_KV_HEADS, GROUP))
    o = jnp.einsum('bkgqs,bskd->bqkgd', p.astype(v_all.dtype), v_all)
    return o.reshape(b, s_len, ATT_WIDTH), k_all[:, -w:], v_all[:, -w:]


def _short_conv(u, hist, conv_w):
    full = jnp.concatenate([hist, u], axis=1)
    y = lax.conv_general_dilated(full, conv_w.astype(u.dtype)[:, None, :], window_strides=(1,),
                                 padding='VALID', dimension_numbers=('NWC', 'WIO', 'NWC'),
                                 feature_group_count=CONV_WIDTH)
    return y, full[:, -(CONV_K - 1):]


def _moe(h, w_router, b_router, w_gu, b_gu, w_down, b_down):
    n, d = h.shape
    logits = (h @ w_router + b_router).astype(jnp.float32)
    top_val, top_idx = lax.top_k(logits, TOP_K)
    gates = jax.nn.softmax(top_val, axis=-1)
    nk = n * TOP_K
    flat_e = top_idx.reshape(nk)
    order = jnp.argsort(flat_e)
    e_sorted = flat_e[order]
    tok = order // TOP_K
    counts = jnp.bincount(flat_e, length=N_EXPERTS)
    padded = (counts + MOE_BLOCK - 1) // MOE_BLOCK * MOE_BLOCK
    pad_end = jnp.cumsum(padded)
    pad_start = pad_end - padded
    start = jnp.cumsum(counts) - counts
    dest = pad_start[e_sorted] + jnp.arange(nk) - start[e_sorted]
    n_blocks = -(-nk // MOE_BLOCK) + N_EXPERTS
    xs = jnp.zeros((n_blocks * MOE_BLOCK, d), h.dtype).at[dest].set(h[tok])
    block_e = jnp.minimum(jnp.searchsorted(pad_end, jnp.arange(n_blocks) * MOE_BLOCK, side='right'),
                          N_EXPERTS - 1)

    def expert_block(args):
        xb, e = args
        gu = xb @ w_gu[e] + b_gu[e]
        g, lin = jnp.split(gu, 2, axis=-1)
        g = jnp.minimum(g, SWIGLU_LIMIT)
        lin = jnp.clip(lin, -SWIGLU_LIMIT, SWIGLU_LIMIT)
        a = g * jax.nn.sigmoid(SWIGLU_ALPHA * g) * (lin + 1)
        return a @ w_down[e] + b_down[e]

    ys = lax.map(expert_block, (xs.reshape(n_blocks, MOE_BLOCK, d), block_e)).reshape(-1, d)
    contrib = ys[dest] * gates.reshape(nk)[order][:, None].astype(ys.dtype)
    return jax.ops.segment_sum(contrib, tok, num_segments=n)


def _layer(x, c, hist_k, hist_v, hist_u, rel_table, w_ada, b_ada, w_in, b_in, sink, conv_w,
           w_oa, w_ob, w_o, ln1_g, ln1_b, w_router, b_router, w_gu, b_gu, w_down, b_down, ln2_g, ln2_b):
    b, t, d = x.shape
    mod = (jax.nn.silu(c) @ w_ada + b_ada)[:, None, :]
    sh1, sc1, g1, sh2, sc2, g2 = jnp.split(mod, 6, axis=-1)
    h = x * (1 + sc1) + sh1
    z = h @ w_in + b_in
    q, k, v, cb, cc, cx, ga, gb = jnp.split(z, IN_SPLITS, axis=-1)
    if hist_k is None:
        ya = _attn_prompt(q, k, v, sink, rel_table)
        new_k = k.reshape(b, t, N_KV_HEADS, HEAD_DIM)[:, -WINDOW:]
        new_v = v.reshape(b, t, N_KV_HEADS, HEAD_DIM)[:, -WINDOW:]
        hist_u = jnp.zeros((b, CONV_K - 1, CONV_WIDTH), cx.dtype)
    else:
        ya, new_k, new_v = _attn_sample(q, k, v, hist_k, hist_v, sink, rel_table)
    yc, new_u = _short_conv(cc * cx, hist_u, conv_w)
    yb = cb * yc
    mix = (jax.nn.sigmoid(ga) * (ya @ w_oa) + jax.nn.sigmoid(gb) * (yb @ w_ob)) @ w_o
    x = _layernorm(DEEPNORM_ALPHA * x + (1 + g1) * mix, ln1_g, ln1_b)
    h = x * (1 + sc2) + sh2
    ff = _moe(h.reshape(b * t, d), w_router, b_router, w_gu, b_gu, w_down, b_down).reshape(b, t, d)
    x = _layernorm(DEEPNORM_ALPHA * x + (1 + g2) * ff, ln2_g, ln2_b)
    return x, new_k, new_v, new_u


def setup_inputs(seed: int = 0) -> dict:
    key = jax.random.key(seed)
    ks = jax.random.split(key, 32)

    def nrm(k, shape, s):
        return jax.random.normal(k, shape, jnp.float32) * s

    w_cache = min(WINDOW, PAST_LEN)
    col_scale = jnp.concatenate([jnp.full((n,), DEEPNORM_BETA if i == 2 else 1.0, jnp.float32)
                                 for i, n in enumerate(IN_SIZES)])
    return {
        "x_prompt": nrm(ks[0], (BATCH, SEQ, D_MODEL), 1.0),
        "x_sample": nrm(ks[1], (DEC_BATCH, DEC_SEQ, D_MODEL), 1.0),
        "c_prompt": nrm(ks[2], (BATCH, D_MODEL), 1.0),
        "c_sample": nrm(ks[3], (DEC_BATCH, D_MODEL), 1.0),
        "cache_k": nrm(ks[4], (DEPTH, DEC_BATCH, w_cache, N_KV_HEADS, HEAD_DIM), 1.0),
        "cache_v": nrm(ks[5], (DEPTH, DEC_BATCH, w_cache, N_KV_HEADS, HEAD_DIM), 1.0),
        "state_conv": nrm(ks[6], (DEPTH, DEC_BATCH, CONV_K - 1, CONV_WIDTH), 1.0),
        "rel_table": nrm(ks[7], (NUM_BUCKETS, N_HEADS), 0.5),
        "ln0_g": 1.0 + nrm(ks[8], (D_MODEL,), 0.01),
        "ln0_b": nrm(ks[9], (D_MODEL,), 0.01),
        "w_ada": nrm(ks[10], (DEPTH, D_MODEL, 6 * D_MODEL), 0.1 * D_MODEL ** -0.5),
        "b_ada": nrm(ks[11], (DEPTH, 6 * D_MODEL), 0.01),
        "w_in": nrm(ks[12], (DEPTH, D_MODEL, IN_WIDTH), D_MODEL ** -0.5) * col_scale,
        "b_in": nrm(ks[13], (DEPTH, IN_WIDTH), 0.01),
        "sinks": nrm(ks[14], (DEPTH, N_HEADS), 1.0),
        "conv_w": nrm(ks[15], (DEPTH, CONV_K, CONV_WIDTH), CONV_K ** -0.5),
        "w_oa": nrm(ks[16], (DEPTH, ATT_WIDTH, D_MODEL), DEEPNORM_BETA * ATT_WIDTH ** -0.5),
        "w_ob": nrm(ks[17], (DEPTH, CONV_WIDTH, D_MODEL), DEEPNORM_BETA * CONV_WIDTH ** -0.5),
        "w_o": nrm(ks[18], (DEPTH, D_MODEL, D_MODEL), DEEPNORM_BETA * D_MODEL ** -0.5),
        "ln1_g": 1.0 + nrm(ks[19], (DEPTH, D_MODEL), 0.01),
        "ln1_b": nrm(ks[20], (DEPTH, D_MODEL), 0.01),
        "w_router": nrm(ks[21], (DEPTH, D_MODEL, N_EXPERTS), D_MODEL ** -0.5),
        "b_router": nrm(ks[22], (DEPTH, N_EXPERTS), 0.01),
        "w_gu": nrm(ks[23], (DEPTH, N_EXPERTS, D_MODEL, 2 * D_FF), DEEPNORM_BETA * D_MODEL ** -0.5),
        "b_gu": nrm(ks[24], (DEPTH, N_EXPERTS, 2 * D_FF), 0.01),
        "w_down": nrm(ks[25], (DEPTH, N_EXPERTS, D_FF, D_MODEL), DEEPNORM_BETA * D_FF ** -0.5),
        "b_down": nrm(ks[26], (DEPTH, N_EXPERTS, D_MODEL), 0.01),
        "ln2_g": 1.0 + nrm(ks[27], (DEPTH, D_MODEL), 0.01),
        "ln2_b": nrm(ks[28], (DEPTH, D_MODEL), 0.01),
    }


def reference(x_prompt, x_sample, c_prompt, c_sample, cache_k, cache_v, state_conv, rel_table,
              ln0_g, ln0_b, w_ada, b_ada, w_in, b_in, sinks, conv_w, w_oa, w_ob, w_o, ln1_g, ln1_b,
              w_router, b_router, w_gu, b_gu, w_down, b_down, ln2_g, ln2_b):
    y_prompt = _layernorm(x_prompt, ln0_g, ln0_b)
    y_sample = _layernorm(x_sample, ln0_g, ln0_b)
    kp, vp, up, ksm, vsm, usm = [], [], [], [], [], []
    for l in range(DEPTH):
        lw = (w_ada[l], b_ada[l], w_in[l], b_in[l], sinks[l], conv_w[l], w_oa[l], w_ob[l], w_o[l],
              ln1_g[l], ln1_b[l], w_router[l], b_router[l], w_gu[l], b_gu[l], w_down[l], b_down[l],
              ln2_g[l], ln2_b[l])
        y_prompt, k1, v1, u1 = _layer(y_prompt, c_prompt, None, None, None, rel_table, *lw)
        y_sample, k2, v2, u2 = _layer(y_sample, c_sample, cache_k[l], cache_v[l], state_conv[l], rel_table, *lw)
        kp.append(k1); vp.append(v1); up.append(u1)
        ksm.append(k2); vsm.append(v2); usm.append(u2)
    return (y_prompt, y_sample, jnp.stack(kp), jnp.stack(vp), jnp.stack(up),
            jnp.stack(ksm), jnp.stack(vsm), jnp.stack(usm))
```

```python
import functools
import math

import jax
import jax.numpy as jnp
import numpy as np
from jax import lax
from jax.experimental import pallas as pl
from jax.experimental.pallas import tpu as pltpu

D_MODEL = 1024
CHUNK = 64
N_HEADS = 8
N_KV_HEADS = 2
HEAD_DIM = 64
GROUP = N_HEADS // N_KV_HEADS
ATT_WIDTH = N_HEADS * HEAD_DIM
KV_WIDTH = N_KV_HEADS * HEAD_DIM
WINDOW = 128
CONV_WIDTH = 512
CONV_K = 3
NUM_BUCKETS = 32
MAX_DISTANCE = 128
N_EXPERTS = 32
TOP_K = 4
D_FF = 1024
SWIGLU_LIMIT = 7.0
SWIGLU_ALPHA = 1.702
MOE_BLOCK = 256
LN_EPS = 1e-5
NEG_INF = -1e30
IN_SIZES = (ATT_WIDTH, KV_WIDTH, KV_WIDTH, CONV_WIDTH, CONV_WIDTH, CONV_WIDTH, D_MODEL, D_MODEL)
IN_WIDTH = sum(IN_SIZES)
IN_OFFS = tuple(int(s) for s in np.cumsum((0,) + IN_SIZES))

SUBLANES = 8
LANES = 128
ROW_TILES = D_MODEL // LANES
assert ROW_TILES == SUBLANES

PROMPT_TILE = 256
ROUTE_TILE = 512
DEST_GROUP = 64

F32 = jnp.float32
BF16 = jnp.bfloat16
HIGHEST = lax.Precision.HIGHEST
NT_DIMS = (((1,), (1,)), ((), ()))


def _vmem_limit(mib):
    return mib * 1024 * 1024


def _layernorm(x, g, b):
    mu = jnp.mean(x, axis=-1, keepdims=True)
    xc = x - mu
    var = jnp.mean(xc * xc, axis=-1, keepdims=True)
    return xc * lax.rsqrt(var + LN_EPS) * g + b


def _to_row_tiles(ref, x, rows):
    for s in range(ROW_TILES):
        ref[pl.ds(s, rows, stride=ROW_TILES), :] = x[:, s * LANES:(s + 1) * LANES]


def _from_row_tiles(ref, base, rows):
    return jnp.concatenate(
        [ref[pl.ds(base * ROW_TILES + s, rows, stride=ROW_TILES), :] for s in range(ROW_TILES)], axis=-1)


def _ada_kernel(c_ref, w_ref, b_ref, o_ref):
    c = c_ref[...]
    s = c * jax.nn.sigmoid(c)
    o_ref[0] = jnp.dot(s, w_ref[0], precision=HIGHEST, preferred_element_type=F32) + b_ref[0]


def _ada(c_all, w_ada, b_ada):
    depth = w_ada.shape[0]
    nb = c_all.shape[0]
    n_col = 6 * D_MODEL // D_MODEL
    return pl.pallas_call(
        _ada_kernel,
        out_shape=jax.ShapeDtypeStruct((depth, nb, 6 * D_MODEL), F32),
        grid=(depth, n_col),
        in_specs=[
            pl.BlockSpec((nb, D_MODEL), lambda l, j: (0, 0)),
            pl.BlockSpec((1, D_MODEL, D_MODEL), lambda l, j: (l, 0, j)),
            pl.BlockSpec((1, 1, D_MODEL), lambda l, j: (l, 0, j)),
        ],
        out_specs=pl.BlockSpec((1, nb, D_MODEL), lambda l, j: (l, 0, j)),
        compiler_params=pltpu.CompilerParams(dimension_semantics=("arbitrary", "arbitrary"),
                                             vmem_limit_bytes=_vmem_limit(32)),
        name="ada",
    )(c_all, w_ada, b_ada.reshape(depth, 1, 6 * D_MODEL))


def _rel_bucket(rel):
    half = NUM_BUCKETS // 2
    max_exact = half // 2
    n = jnp.abs(rel)
    n_f = jnp.maximum(n, 1).astype(jnp.float32)
    large = max_exact + (jnp.log(n_f / max_exact) / math.log(MAX_DISTANCE / max_exact)
                         * (half - max_exact)).astype(jnp.int32)
    large = jnp.minimum(large, half - 1)
    return jnp.where(rel > 0, half, 0) + jnp.where(n < max_exact, n, large)


def _band_buckets(tile):
    qi = jnp.arange(tile)[:, None]
    kj = jnp.arange(tile + WINDOW)[None, :]
    bucket = _rel_bucket(kj - WINDOW - qi)
    cq = qi // CHUNK
    ck = kj // CHUNK
    in_band = jnp.logical_and(ck >= cq, ck <= cq + WINDOW // CHUNK)
    return jnp.where(in_band, bucket, -1).astype(jnp.int32)


def _bias_kernel(table_ref, bucket_ref, o_ref):
    h = pl.program_id(0)
    b = bucket_ref[...]
    acc = jnp.full(b.shape, NEG_INF, F32)
    for i in range(NUM_BUCKETS):
        acc = jnp.where(b == i, table_ref[i, h], acc)
    o_ref[0] = acc


def _bias_table(rel_table, tile):
    buckets = _band_buckets(tile)
    return pl.pallas_call(
        _bias_kernel,
        out_shape=jax.ShapeDtypeStruct((N_HEADS, tile, tile + WINDOW), F32),
        grid=(N_HEADS,),
        in_specs=[
            pl.BlockSpec(memory_space=pltpu.SMEM),
            pl.BlockSpec((tile, tile + WINDOW), lambda h: (0, 0)),
        ],
        out_specs=pl.BlockSpec((1, tile, tile + WINDOW), lambda h: (h, 0, 0)),
        compiler_params=pltpu.CompilerParams(dimension_semantics=("arbitrary",)),
        name="rel_bias",
    )(rel_table, buckets)


def _mix_kernel(x_ref, mod_ref, ln0g_ref, ln0b_ref, win_ref, bin_ref, bias_ref, sink_ref, convw_ref,
                woa_ref, wob_ref, wo_ref, ln1g_ref, ln1b_ref, wr_ref, br_ref, k0_ref, v0_ref, u0_ref,
                x1_ref, h2_ref, idx_ref, gate_ref, newk_ref, newv_ref, newu_ref,
                kc_ref, vc_ref, ubuf_ref, *, tile, apply_ln0, mask_first, alpha):
    i = pl.program_id(1)
    t = tile

    @pl.when(i == 0)
    def _():
        kc_ref[...] = k0_ref[0]
        vc_ref[...] = v0_ref[0]
        ubuf_ref[0:SUBLANES, :] = u0_ref[0]

    x = x_ref[0]
    if apply_ln0:
        x = _layernorm(x, ln0g_ref[...], ln0b_ref[...])
    mod = mod_ref[0]
    sh1, sc1, g1, sh2, sc2, g2 = [mod[j:j + 1, :] for j in range(6)]
    h = (x * (1.0 + sc1) + sh1).astype(BF16)

    def proj(j0, j1):
        lo, hi = IN_OFFS[j0], IN_OFFS[j1]
        return jnp.dot(h, win_ref[:, lo:hi], preferred_element_type=F32) + bin_ref[:, lo:hi]

    q = proj(0, 1)
    kv = proj(1, 3)
    kfull = jnp.concatenate([kc_ref[...], kv[:, :KV_WIDTH]], axis=0)
    vfull = jnp.concatenate([vc_ref[...], kv[:, KV_WIDTH:]], axis=0)
    kc_ref[...] = kfull[t:, :]
    vc_ref[...] = vfull[t:, :]
    newk_ref[0] = kfull[t:, :]
    newv_ref[0] = vfull[t:, :]
    kb = kfull.astype(BF16)
    vb = vfull.astype(BF16)
    if mask_first:
        col = lax.broadcasted_iota(jnp.int32, (t, t + WINDOW), 1)
        no_past = col < jnp.where(i == 0, WINDOW, 0)
    heads = []
    for hh in range(N_HEADS):
        g = hh // GROUP
        qh = q[:, hh * HEAD_DIM:(hh + 1) * HEAD_DIM].astype(BF16)
        kh = kb[:, g * HEAD_DIM:(g + 1) * HEAD_DIM]
        vh = vb[:, g * HEAD_DIM:(g + 1) * HEAD_DIM]
        s = lax.dot_general(qh, kh, NT_DIMS, preferred_element_type=F32) * (HEAD_DIM ** -0.5) + bias_ref[hh]
        if mask_first:
            s = jnp.where(no_past, NEG_INF, s)
        sink = sink_ref[hh]
        m = jnp.maximum(jnp.max(s, axis=-1, keepdims=True), sink)
        e = jnp.exp(s - m)
        denom = jnp.sum(e, axis=-1, keepdims=True) + jnp.exp(sink - m)
        o = jnp.dot(e.astype(BF16), vh, preferred_element_type=F32)
        heads.append(o * (1.0 / denom))
    ya = jnp.concatenate(heads, axis=-1)

    cb = proj(3, 4)
    u = proj(4, 5) * proj(5, 6)
    ubuf_ref[SUBLANES:t + SUBLANES, :] = u
    cw = convw_ref[...]
    yc = (cw[0:1, :] * ubuf_ref[SUBLANES - 2:t + SUBLANES - 2, :]
          + cw[1:2, :] * ubuf_ref[SUBLANES - 1:t + SUBLANES - 1, :] + cw[2:3, :] * u)
    yb = cb * yc
    tail = ubuf_ref[t:t + SUBLANES, :]
    newu_ref[0] = tail
    ubuf_ref[0:SUBLANES, :] = tail

    a_out = jnp.dot(ya.astype(BF16), woa_ref[...], preferred_element_type=F32)
    b_out = jnp.dot(yb.astype(BF16), wob_ref[...], preferred_element_type=F32)
    mixin = jax.nn.sigmoid(proj(6, 7)) * a_out + jax.nn.sigmoid(proj(7, 8)) * b_out
    mix = jnp.dot(mixin.astype(BF16), wo_ref[...], preferred_element_type=F32)
    x1 = _layernorm(alpha * x + (1.0 + g1) * mix, ln1g_ref[...], ln1b_ref[...])
    x1_ref[0] = x1

    h2 = x1 * (1.0 + sc2) + sh2
    _to_row_tiles(h2_ref, h2, t)
    logits = lax.dot_general(wr_ref[...], h2, NT_DIMS, precision=HIGHEST,
                             preferred_element_type=F32) + br_ref[...]
    eid = lax.broadcasted_iota(jnp.int32, (N_EXPERTS, t), 0)
    vals, ids = [], []
    for _ in range(TOP_K):
        mx = jnp.max(logits, axis=0, keepdims=True)
        sel = jnp.min(jnp.where(logits == mx, eid, N_EXPERTS), axis=0, keepdims=True)
        vals.append(mx)
        ids.append(sel)
        logits = jnp.where(eid == sel, -jnp.inf, logits)
    ex = [jnp.exp(v - vals[0]) for v in vals]
    tot = ex[0] + ex[1] + ex[2] + ex[3]
    idx_ref[0] = jnp.concatenate(ids, axis=0)
    gate_ref[0] = jnp.concatenate([e_ / tot for e_ in ex], axis=0)


def _mix(x, mod, ln0_g, ln0_b, w_in, b_in, bias, sink, conv_w, w_oa, w_ob, w_o, ln1_g, ln1_b,
         w_rt, b_r, k0, v0, u0, *, tile, apply_ln0, mask_first, alpha):
    b, s, d = x.shape
    n_t = s // tile
    const = lambda shape: pl.BlockSpec(shape, lambda bb, ii: (0,) * len(shape), pipeline_mode=pl.Buffered(1))
    per_b = lambda shape: pl.BlockSpec((1,) + shape, lambda bb, ii: (bb,) + (0,) * len(shape))
    kern = functools.partial(_mix_kernel, tile=tile, apply_ln0=apply_ln0, mask_first=mask_first, alpha=alpha)
    return pl.pallas_call(
        kern,
        out_shape=(
            jax.ShapeDtypeStruct((b, s, d), F32),
            jax.ShapeDtypeStruct((b * s * ROW_TILES, LANES), F32),
            jax.ShapeDtypeStruct((b * n_t, TOP_K, tile), jnp.int32),
            jax.ShapeDtypeStruct((b * n_t, TOP_K, tile), F32),
            jax.ShapeDtypeStruct((b, WINDOW, KV_WIDTH), F32),
            jax.ShapeDtypeStruct((b, WINDOW, KV_WIDTH), F32),
            jax.ShapeDtypeStruct((b, SUBLANES, CONV_WIDTH), F32),
        ),
        grid=(b, n_t),
        in_specs=[
            pl.BlockSpec((1, tile, d), lambda bb, ii: (bb, ii, 0)),
            per_b((6, d)),
            const((1, d)), const((1, d)),
            const((d, IN_WIDTH)), const((1, IN_WIDTH)),
            const((N_HEADS, tile, tile + WINDOW)),
            pl.BlockSpec(memory_space=pltpu.SMEM),
            const((CONV_K, CONV_WIDTH)),
            const((ATT_WIDTH, d)), const((CONV_WIDTH, d)), const((d, d)),
            const((1, d)), const((1, d)),
            const((N_EXPERTS, d)), const((N_EXPERTS, 1)),
            per_b((WINDOW, KV_WIDTH)), per_b((WINDOW, KV_WIDTH)), per_b((SUBLANES, CONV_WIDTH)),
        ],
        out_specs=(
            pl.BlockSpec((1, tile, d), lambda bb, ii: (bb, ii, 0)),
            pl.BlockSpec((tile * ROW_TILES, LANES), lambda bb, ii: (bb * n_t + ii, 0)),
            pl.BlockSpec((1, TOP_K, tile), lambda bb, ii: (bb * n_t + ii, 0, 0)),
            pl.BlockSpec((1, TOP_K, tile), lambda bb, ii: (bb * n_t + ii, 0, 0)),
            per_b((WINDOW, KV_WIDTH)), per_b((WINDOW, KV_WIDTH)), per_b((SUBLANES, CONV_WIDTH)),
        ),
        scratch_shapes=[
            pltpu.VMEM((WINDOW, KV_WIDTH), F32),
            pltpu.VMEM((WINDOW, KV_WIDTH), F32),
            pltpu.VMEM((tile + SUBLANES, CONV_WIDTH), F32),
        ],
        compiler_params=pltpu.CompilerParams(dimension_semantics=("arbitrary", "arbitrary"),
                                             vmem_limit_bytes=_vmem_limit(56)),
        name="mix",
    )(x, mod, ln0_g, ln0_b, w_in, b_in, bias, sink, conv_w, w_oa, w_ob, w_o, ln1_g, ln1_b, w_rt, b_r,
      k0, v0, u0)


def _route_kernel(idx_ref, dest_ref, cnt_ref, run_ref, start_ref):
    phase = pl.program_id(0)
    j = pl.program_id(1)
    t = ROUTE_TILE
    ids = idx_ref[...]
    eid = lax.broadcasted_iota(jnp.int32, (N_EXPERTS, t), 0)
    hits = [(eid == ids[k:k + 1, :]).astype(F32) for k in range(TOP_K)]
    hit = hits[0] + hits[1] + hits[2] + hits[3]

    @pl.when(jnp.logical_and(phase == 0, j == 0))
    def _():
        run_ref[...] = jnp.zeros_like(run_ref)

    @pl.when(phase == 0)
    def _():
        run_ref[...] += jnp.sum(hit, axis=1, keepdims=True)

    @pl.when(jnp.logical_and(phase == 1, j == 0))
    def _():
        cnt = run_ref[...]
        cnt_ref[...] = jnp.broadcast_to(cnt, cnt_ref.shape)
        padded = jnp.floor((cnt + (MOE_BLOCK - 1)) * (1.0 / MOE_BLOCK)) * MOE_BLOCK
        r = lax.broadcasted_iota(jnp.int32, (N_EXPERTS, N_EXPERTS), 0)
        c = lax.broadcasted_iota(jnp.int32, (N_EXPERTS, N_EXPERTS), 1)
        before = (c < r).astype(F32)
        start_ref[...] = jnp.dot(before, jnp.broadcast_to(padded, start_ref.shape), precision=HIGHEST,
                                 preferred_element_type=F32)
        run_ref[...] = jnp.zeros_like(run_ref)

    @pl.when(phase == 1)
    def _():
        r = lax.broadcasted_iota(jnp.int32, (t, t), 0)
        c = lax.broadcasted_iota(jnp.int32, (t, t), 1)
        upto = (r <= c).astype(BF16)
        incl = jnp.dot(hit.astype(BF16), upto, preferred_element_type=F32)
        pos = incl - hit + (start_ref[:, 0:1] + run_ref[...])
        for k in range(TOP_K):
            d = jnp.sum(hits[k] * pos, axis=0, keepdims=True).astype(jnp.int32)
            for gidx in range(t // DEST_GROUP):
                dest_ref[gidx, k:k + 1, :] = d[:, gidx * DEST_GROUP:(gidx + 1) * DEST_GROUP]
        run_ref[...] += jnp.sum(hit, axis=1, keepdims=True)


def _route(idx_all):
    n = idx_all.shape[1]
    n_t = n // ROUTE_TILE
    groups = ROUTE_TILE // DEST_GROUP
    return pl.pallas_call(
        _route_kernel,
        out_shape=(
            jax.ShapeDtypeStruct((n // DEST_GROUP, TOP_K, DEST_GROUP), jnp.int32),
            jax.ShapeDtypeStruct((N_EXPERTS, LANES), F32),
        ),
        grid=(2, n_t),
        in_specs=[pl.BlockSpec((TOP_K, ROUTE_TILE), lambda p, j: (0, j))],
        out_specs=(
            pl.BlockSpec((groups, TOP_K, DEST_GROUP), lambda p, j: (p * j, 0, 0)),
            pl.BlockSpec((N_EXPERTS, LANES), lambda p, j: (0, 0)),
        ),
        scratch_shapes=[pltpu.VMEM((N_EXPERTS, 1), F32), pltpu.VMEM((N_EXPERTS, LANES), F32)],
        compiler_params=pltpu.CompilerParams(dimension_semantics=("arbitrary", "arbitrary")),
        name="route",
    )(idx_all)


def _row(ref, r):
    return ref.at[pl.ds(pl.multiple_of(r * ROW_TILES, ROW_TILES), ROW_TILES), :]


def _scatter_kernel(cnt_ref, start_ref, nused_ref, dest_ref, hp_ref, hs_ref, xs_ref, zero_ref, sem, zsem,
                    *, n_prompt_tiles, n_blocks):
    j = pl.program_id(0)
    block_rows = MOE_BLOCK * ROW_TILES

    def zero_block_copy(blk):
        return pltpu.make_async_copy(
            zero_ref, xs_ref.at[pl.ds(pl.multiple_of(blk * block_rows, block_rows), block_rows), :], zsem)

    def zero_row_copy(r):
        return pltpu.make_async_copy(zero_ref.at[pl.ds(0, ROW_TILES), :], _row(xs_ref, r), zsem)

    @pl.when(j == 0)
    def _():
        zero_ref[...] = jnp.zeros_like(zero_ref)
        n_used = nused_ref[0]

        def per_expert(e, carry):
            lo = start_ref[e] + cnt_ref[e]
            hi = start_ref[e] + ((cnt_ref[e] + (MOE_BLOCK - 1)) // MOE_BLOCK) * MOE_BLOCK
            lax.fori_loop(lo, hi, lambda r, c: (zero_row_copy(r).start(), c)[1], 0)
            lax.fori_loop(lo, hi, lambda r, c: (zero_row_copy(r).wait(), c)[1], 0)
            return carry

        lax.fori_loop(0, N_EXPERTS, per_expert, 0)
        lax.fori_loop(n_used, n_blocks, lambda b_, c: (zero_block_copy(b_).start(), c)[1], 0)
        lax.fori_loop(n_used, n_blocks, lambda b_, c: (zero_block_copy(b_).wait(), c)[1], 0)

    def issue(src_ref, tile_idx):
        for g in range(ROUTE_TILE // DEST_GROUP):
            def body(tt, carry, g=g):
                tok = tile_idx * ROUTE_TILE + g * DEST_GROUP + tt
                for k in range(TOP_K):
                    pltpu.make_async_copy(_row(src_ref, tok), _row(xs_ref, dest_ref[g, k, tt]), sem).start()
                return carry
            lax.fori_loop(0, DEST_GROUP, body, 0)

    @pl.when(j < n_prompt_tiles)
    def _():
        issue(hp_ref, j)

    @pl.when(j >= n_prompt_tiles)
    def _():
        issue(hs_ref, j - n_prompt_tiles)

    n_rows = ROUTE_TILE * TOP_K * ROW_TILES
    pltpu.make_async_copy(xs_ref.at[pl.ds(0, n_rows), :], xs_ref.at[pl.ds(n_rows, n_rows), :], sem).wait()


def _scatter(counts, starts, n_used, dest, h_prompt, h_sample, n_blocks):
    n_prompt_tiles = h_prompt.shape[0] // (ROUTE_TILE * ROW_TILES)
    n_sample_tiles = h_sample.shape[0] // (ROUTE_TILE * ROW_TILES)
    groups = ROUTE_TILE // DEST_GROUP
    kern = functools.partial(_scatter_kernel, n_prompt_tiles=n_prompt_tiles, n_blocks=n_blocks)
    return pl.pallas_call(
        kern,
        out_shape=jax.ShapeDtypeStruct((n_blocks * MOE_BLOCK * ROW_TILES, LANES), F32),
        grid_spec=pltpu.PrefetchScalarGridSpec(
            num_scalar_prefetch=3,
            grid=(n_prompt_tiles + n_sample_tiles,),
            in_specs=[
                pl.BlockSpec((groups, TOP_K, DEST_GROUP), lambda j, *_: (j, 0, 0), memory_space=pltpu.SMEM),
                pl.BlockSpec(memory_space=pl.ANY),
                pl.BlockSpec(memory_space=pl.ANY),
            ],
            out_specs=pl.BlockSpec(memory_space=pl.ANY),
            scratch_shapes=[
                pltpu.VMEM((MOE_BLOCK * ROW_TILES, LANES), F32),
                pltpu.SemaphoreType.DMA,
                pltpu.SemaphoreType.DMA,
            ],
        ),
        compiler_params=pltpu.CompilerParams(dimension_semantics=("arbitrary",)),
        name="scatter",
    )(counts, starts, n_used, dest, h_prompt, h_sample)


def _ffn_kernel(be_ref, nused_ref, xs_ref, wgu_ref, bgu_ref, wd_ref, bd_ref, ys_ref, wgu_bf, wd_bf):
    i = pl.program_id(0)
    n_used = nused_ref[0]

    @pl.when(i < n_used)
    def _():
        prev = be_ref[jnp.maximum(i - 1, 0)]

        @pl.when(jnp.logical_or(i == 0, be_ref[i] != prev))
        def _():
            wgu_bf[...] = wgu_ref[0].astype(BF16)
            wd_bf[...] = wd_ref[0].astype(BF16)

        x = _from_row_tiles(xs_ref, 0, MOE_BLOCK).astype(BF16)
        gu = jnp.dot(x, wgu_bf[...], preferred_element_type=F32) + bgu_ref[0]
        g = jnp.minimum(gu[:, :D_FF], SWIGLU_LIMIT)
        lin = jnp.clip(gu[:, D_FF:], -SWIGLU_LIMIT, SWIGLU_LIMIT)
        a = g * jax.nn.sigmoid(SWIGLU_ALPHA * g) * (lin + 1.0)
        y = jnp.dot(a.astype(BF16), wd_bf[...], preferred_element_type=F32) + bd_ref[0]
        _to_row_tiles(ys_ref, y, MOE_BLOCK)

    @pl.when(i >= n_used)
    def _():
        ys_ref[...] = jnp.zeros_like(ys_ref)


def _ffn(block_e, n_used, xs, w_gu, b_gu, w_down, b_down, n_blocks):
    block_rows = MOE_BLOCK * ROW_TILES

    def live(i, nu):
        return jnp.minimum(i, nu[0] - 1)

    return pl.pallas_call(
        _ffn_kernel,
        out_shape=jax.ShapeDtypeStruct(xs.shape, F32),
        grid_spec=pltpu.PrefetchScalarGridSpec(
            num_scalar_prefetch=2,
            grid=(n_blocks,),
            in_specs=[
                pl.BlockSpec((block_rows, LANES), lambda i, be, nu: (live(i, nu), 0)),
                pl.BlockSpec((1, D_MODEL, 2 * D_FF), lambda i, be, nu: (be[live(i, nu)], 0, 0)),
                pl.BlockSpec((1, 1, 2 * D_FF), lambda i, be, nu: (be[live(i, nu)], 0, 0)),
                pl.BlockSpec((1, D_FF, D_MODEL), lambda i, be, nu: (be[live(i, nu)], 0, 0)),
                pl.BlockSpec((1, 1, D_MODEL), lambda i, be, nu: (be[live(i, nu)], 0, 0)),
            ],
            out_specs=pl.BlockSpec((block_rows, LANES), lambda i, be, nu: (i, 0)),
            scratch_shapes=[pltpu.VMEM((D_MODEL, 2 * D_FF), BF16), pltpu.VMEM((D_FF, D_MODEL), BF16)],
        ),
        compiler_params=pltpu.CompilerParams(dimension_semantics=("arbitrary",),
                                             vmem_limit_bytes=_vmem_limit(56)),
        name="ffn",
    )(block_e, n_used, xs, w_gu, b_gu.reshape(N_EXPERTS, 1, 2 * D_FF), w_down,
      b_down.reshape(N_EXPERTS, 1, D_MODEL))


def _combine_kernel(dest_ref, x1_ref, mod_ref, gate_ref, ln2g_ref, ln2b_ref, ys_ref, o_ref, buf_ref, sem,
                    *, tile, alpha):
    t = tile
    for g in range(t // DEST_GROUP):
        def body(tt, carry, g=g):
            tok = g * DEST_GROUP + tt
            for k in range(TOP_K):
                pltpu.make_async_copy(_row(ys_ref, dest_ref[g, k, tt]), _row(buf_ref, k * t + tok), sem).start()
            return carry
        lax.fori_loop(0, DEST_GROUP, body, 0)
    pltpu.make_async_copy(ys_ref.at[pl.ds(0, TOP_K * t * ROW_TILES), :], buf_ref, sem).wait()

    gates = gate_ref[0]
    gates_t = jnp.transpose(jnp.concatenate([gates, jnp.zeros_like(gates)], axis=0))
    ff = jnp.zeros((t, D_MODEL), F32)
    for k in range(TOP_K):
        ff = ff + gates_t[:, k:k + 1] * _from_row_tiles(buf_ref, k * t, t)
    g2 = mod_ref[0][5:6, :]
    o_ref[0] = _layernorm(alpha * x1_ref[0] + (1.0 + g2) * ff, ln2g_ref[...], ln2b_ref[...])


def _combine(dest, group_offset, x1, mod, gates, ln2_g, ln2_b, ys, *, tile, alpha):
    b, s, d = x1.shape
    n_t = s // tile
    groups = tile // DEST_GROUP
    goff = group_offset // groups
    kern = functools.partial(_combine_kernel, tile=tile, alpha=alpha)
    return pl.pallas_call(
        kern,
        out_shape=jax.ShapeDtypeStruct((b, s, d), F32),
        grid=(b, n_t),
        in_specs=[
            pl.BlockSpec((groups, TOP_K, DEST_GROUP), lambda bb, ii: (goff + bb * n_t + ii, 0, 0),
                         memory_space=pltpu.SMEM),
            pl.BlockSpec((1, tile, d), lambda bb, ii: (bb, ii, 0)),
            pl.BlockSpec((1, 6, d), lambda bb, ii: (bb, 0, 0)),
            pl.BlockSpec((1, TOP_K, tile), lambda bb, ii: (bb * n_t + ii, 0, 0)),
            pl.BlockSpec((1, d), lambda bb, ii: (0, 0)),
            pl.BlockSpec((1, d), lambda bb, ii: (0, 0)),
            pl.BlockSpec(memory_space=pl.ANY),
        ],
        out_specs=pl.BlockSpec((1, tile, d), lambda bb, ii: (bb, ii, 0)),
        scratch_shapes=[pltpu.VMEM((TOP_K * tile * ROW_TILES, LANES), F32), pltpu.SemaphoreType.DMA],
        compiler_params=pltpu.CompilerParams(dimension_semantics=("arbitrary", "arbitrary"),
                                             vmem_limit_bytes=_vmem_limit(32)),
        name="combine",
    )(dest, x1, mod, gates, ln2_g, ln2_b, ys)


def kernel(x_prompt, x_sample, c_prompt, c_sample, cache_k, cache_v, state_conv, rel_table, ln0_g, ln0_b, w_ada, b_ada, w_in, b_in, sinks, conv_w, w_oa, w_ob, w_o, ln1_g, ln1_b, w_router, b_router, w_gu, b_gu, w_down, b_down, ln2_g, ln2_b):
    depth = w_ada.shape[0]
    bp, sp, d = x_prompt.shape
    bs, ss, _ = x_sample.shape
    alpha = (2 * depth) ** 0.25
    n_prompt, n_sample = bp * sp, bs * ss
    n_tok = n_prompt + n_sample
    assert n_prompt % ROUTE_TILE == 0 and n_sample % ROUTE_TILE == 0 and sp % PROMPT_TILE == 0
    assert ss % DEST_GROUP == 0 and ss <= WINDOW
    n_blocks = -(-(n_tok * TOP_K) // MOE_BLOCK) + N_EXPERTS

    mod_all = _ada(jnp.concatenate([c_prompt, c_sample], axis=0), w_ada, b_ada)
    mod_all = mod_all.reshape(depth, bp + bs, 6, d)
    bias_p = _bias_table(rel_table, PROMPT_TILE)
    bias_s = _bias_table(rel_table, ss)
    row = lambda a: a.reshape(1, -1)
    zeros_kv = jnp.zeros((bp, WINDOW, KV_WIDTH), F32)
    zeros_u = jnp.zeros((bp, SUBLANES, CONV_WIDTH), F32)

    y_p, y_s = x_prompt, x_sample
    outs = {name: [] for name in ("kp", "vp", "up", "ks", "vs", "us")}
    for l in range(depth):
        shared = (w_in[l].astype(BF16), row(b_in[l]))
        tail = (sinks[l], conv_w[l], w_oa[l].astype(BF16), w_ob[l].astype(BF16), w_o[l].astype(BF16),
                row(ln1_g[l]), row(ln1_b[l]), w_router[l].T, b_router[l].reshape(N_EXPERTS, 1))
        mod_p, mod_s = mod_all[l, :bp], mod_all[l, bp:]
        x1_p, h2_p, idx_p, gate_p, k_p, v_p, u_p = _mix(
            y_p, mod_p, row(ln0_g), row(ln0_b), *shared, bias_p, *tail, zeros_kv, zeros_kv, zeros_u,
            tile=PROMPT_TILE, apply_ln0=(l == 0), mask_first=True, alpha=alpha)
        u0 = jnp.pad(state_conv[l], ((0, 0), (SUBLANES - (CONV_K - 1), 0), (0, 0)))
        x1_s, h2_s, idx_s, gate_s, k_s, v_s, u_s = _mix(
            y_s, mod_s, row(ln0_g), row(ln0_b), *shared, bias_s, *tail,
            cache_k[l].reshape(bs, WINDOW, KV_WIDTH), cache_v[l].reshape(bs, WINDOW, KV_WIDTH), u0,
            tile=ss, apply_ln0=(l == 0), mask_first=False, alpha=alpha)

        idx_all = jnp.concatenate([jnp.transpose(idx_p, (1, 0, 2)).reshape(TOP_K, n_prompt),
                                   jnp.transpose(idx_s, (1, 0, 2)).reshape(TOP_K, n_sample)], axis=1)
        dest, cnt = _route(idx_all)
        counts = cnt[:, 0].astype(jnp.int32)
        padded = (counts + MOE_BLOCK - 1) // MOE_BLOCK * MOE_BLOCK
        pad_end = jnp.cumsum(padded)
        starts = pad_end - padded
        n_used = (pad_end[-1:] // MOE_BLOCK).astype(jnp.int32)
        block_e = jnp.minimum(jnp.searchsorted(pad_end, jnp.arange(n_blocks) * MOE_BLOCK, side='right'),
                              N_EXPERTS - 1).astype(jnp.int32)

        xs = _scatter(counts, starts, n_used, dest, h2_p, h2_s, n_blocks)
        ys = _ffn(block_e, n_used, xs, w_gu[l], b_gu[l], w_down[l], b_down[l], n_blocks)
        y_p = _combine(dest, 0, x1_p, mod_p, gate_p, row(ln2_g[l]), row(ln2_b[l]), ys,
                       tile=PROMPT_TILE, alpha=alpha)
        y_s = _combine(dest, n_prompt // DEST_GROUP, x1_s, mod_s, gate_s, row(ln2_g[l]), row(ln2_b[l]), ys,
                       tile=ss, alpha=alpha)

        outs["kp"].append(k_p.reshape(bp, WINDOW, N_KV_HEADS, HEAD_DIM))
        outs["vp"].append(v_p.reshape(bp, WINDOW, N_KV_HEADS, HEAD_DIM))
        outs["up"].append(u_p[:, SUBLANES - (CONV_K - 1):, :])
        outs["ks"].append(k_s.reshape(bs, WINDOW, N_KV_HEADS, HEAD_DIM))
        outs["vs"].append(v_s.reshape(bs, WINDOW, N_KV_HEADS, HEAD_DIM))
        outs["us"].append(u_s[:, SUBLANES - (CONV_K - 1):, :])
    return (y_p, y_s, jnp.stack(outs["kp"]), jnp.stack(outs["vp"]), jnp.stack(outs["up"]),
            jnp.stack(outs["ks"]), jnp.stack(outs["vs"]), jnp.stack(outs["us"]))
```

```python
import functools
import math

import jax
import jax.numpy as jnp
import numpy as np
from jax import lax
from jax.experimental import pallas as pl
from jax.experimental.pallas import tpu as pltpu

D_MODEL = 1024
CHUNK = 64
N_HEADS = 8
N_KV_HEADS = 2
HEAD_DIM = 64
GROUP = N_HEADS // N_KV_HEADS
ATT_WIDTH = N_HEADS * HEAD_DIM
KV_WIDTH = N_KV_HEADS * HEAD_DIM
WINDOW = 128
CONV_WIDTH = 512
CONV_K = 3
NUM_BUCKETS = 32
MAX_DISTANCE = 128
N_EXPERTS = 32
TOP_K = 4
D_FF = 1024
SWIGLU_LIMIT = 7.0
SWIGLU_ALPHA = 1.702
MOE_BLOCK = 256
LN_EPS = 1e-5
NEG_INF = -1e30
IN_SIZES = (ATT_WIDTH, KV_WIDTH, KV_WIDTH, CONV_WIDTH, CONV_WIDTH, CONV_WIDTH, D_MODEL, D_MODEL)
IN_WIDTH = sum(IN_SIZES)
IN_OFFS = tuple(int(s) for s in np.cumsum((0,) + IN_SIZES))

SUBLANES = 8
LANES = 128
ROW_TILES = D_MODEL // LANES
assert ROW_TILES == SUBLANES

PROMPT_TILE = 256
ROUTE_TILE = 512
DEST_GROUP = 64

F32 = jnp.float32
BF16 = jnp.bfloat16
HIGHEST = lax.Precision.HIGHEST
NT_DIMS = (((1,), (1,)), ((), ()))


def _vmem_limit(mib):
    return mib * 1024 * 1024


def _layernorm(x, g, b):
    mu = jnp.mean(x, axis=-1, keepdims=True)
    xc = x - mu
    var = jnp.mean(xc * xc, axis=-1, keepdims=True)
    return xc * lax.rsqrt(var + LN_EPS) * g + b


def _to_row_tiles(ref, x, rows):
    for s in range(ROW_TILES):
        ref[pl.ds(s, rows, stride=ROW_TILES), :] = x[:, s * LANES:(s + 1) * LANES]


def _from_row_tiles(ref, base, rows):
    return jnp.concatenate(
        [ref[pl.ds(base * ROW_TILES + s, rows, stride=ROW_TILES), :] for s in range(ROW_TILES)], axis=-1)


def _ada_kernel(c_ref, w_ref, b_ref, o_ref):
    c = c_ref[...]
    s = c * jax.nn.sigmoid(c)
    o_ref[0] = jnp.dot(s, w_ref[0], precision=HIGHEST, preferred_element_type=F32) + b_ref[0]


def _ada(c_all, w_ada, b_ada):
    depth = w_ada.shape[0]
    nb = c_all.shape[0]
    n_col = 6 * D_MODEL // D_MODEL
    return pl.pallas_call(
        _ada_kernel,
        out_shape=jax.ShapeDtypeStruct((depth, nb, 6 * D_MODEL), F32),
        grid=(depth, n_col),
        in_specs=[
            pl.BlockSpec((nb, D_MODEL), lambda l, j: (0, 0)),
            pl.BlockSpec((1, D_MODEL, D_MODEL), lambda l, j: (l, 0, j)),
            pl.BlockSpec((1, 1, D_MODEL), lambda l, j: (l, 0, j)),
        ],
        out_specs=pl.BlockSpec((1, nb, D_MODEL), lambda l, j: (l, 0, j)),
        compiler_params=pltpu.CompilerParams(dimension_semantics=("arbitrary", "arbitrary"),
                                             vmem_limit_bytes=_vmem_limit(32)),
        name="ada",
    )(c_all, w_ada, b_ada.reshape(depth, 1, 6 * D_MODEL))


def _rel_bucket(rel):
    half = NUM_BUCKETS // 2
    max_exact = half // 2
    n = jnp.abs(rel)
    n_f = jnp.maximum(n, 1).astype(jnp.float32)
    large = max_exact + (jnp.log(n_f / max_exact) / math.log(MAX_DISTANCE / max_exact)
                         * (half - max_exact)).astype(jnp.int32)
    large = jnp.minimum(large, half - 1)
    return jnp.where(rel > 0, half, 0) + jnp.where(n < max_exact, n, large)


def _band_buckets(tile):
    qi = jnp.arange(tile)[:, None]
    kj = jnp.arange(tile + WINDOW)[None, :]
    bucket = _rel_bucket(kj - WINDOW - qi)
    cq = qi // CHUNK
    ck = kj // CHUNK
    in_band = jnp.logical_and(ck >= cq, ck <= cq + WINDOW // CHUNK)
    return jnp.where(in_band, bucket, -1).astype(jnp.int32)


def _bias_kernel(table_ref, bucket_ref, o_ref):
    h = pl.program_id(0)
    b = bucket_ref[...]
    acc = jnp.full(b.shape, NEG_INF, F32)
    for i in range(NUM_BUCKETS):
        acc = jnp.where(b == i, table_ref[i, h], acc)
    o_ref[0] = acc


def _bias_table(rel_table, tile):
    buckets = _band_buckets(tile)
    return pl.pallas_call(
        _bias_kernel,
        out_shape=jax.ShapeDtypeStruct((N_HEADS, tile, tile + WINDOW), F32),
        grid=(N_HEADS,),
        in_specs=[
            pl.BlockSpec(memory_space=pltpu.SMEM),
            pl.BlockSpec((tile, tile + WINDOW), lambda h: (0, 0)),
        ],
        out_specs=pl.BlockSpec((1, tile, tile + WINDOW), lambda h: (h, 0, 0)),
        compiler_params=pltpu.CompilerParams(dimension_semantics=("arbitrary",)),
        name="rel_bias",
    )(rel_table, buckets)


def _mix_kernel(x_ref, mod_ref, ln0g_ref, ln0b_ref, win_ref, bin_ref, bias_ref, sink_ref, convw_ref,
                woa_ref, wob_ref, wo_ref, ln1g_ref, ln1b_ref, wr_ref, br_ref, k0_ref, v0_ref, u0_ref,
                x1_ref, h2_ref, idx_ref, gate_ref, newk_ref, newv_ref, newu_ref,
                kc_ref, vc_ref, ubuf_ref, *, tile, apply_ln0, mask_first, alpha):
    i = pl.program_id(1)
    t = tile

    @pl.when(i == 0)
    def _():
        kc_ref[...] = k0_ref[0]
        vc_ref[...] = v0_ref[0]
        ubuf_ref[0:SUBLANES, :] = u0_ref[0]

    x = x_ref[0]
    if apply_ln0:
        x = _layernorm(x, ln0g_ref[...], ln0b_ref[...])
    mod = mod_ref[0]
    sh1, sc1, g1, sh2, sc2, g2 = [mod[j:j + 1, :] for j in range(6)]
    h = (x * (1.0 + sc1) + sh1).astype(BF16)

    def proj(j0, j1):
        lo, hi = IN_OFFS[j0], IN_OFFS[j1]
        return jnp.dot(h, win_ref[:, lo:hi], preferred_element_type=F32) + bin_ref[:, lo:hi]

    q = proj(0, 1)
    kv = proj(1, 3)
    kfull = jnp.concatenate([kc_ref[...], kv[:, :KV_WIDTH]], axis=0)
    vfull = jnp.concatenate([vc_ref[...], kv[:, KV_WIDTH:]], axis=0)
    kc_ref[...] = kfull[t:, :]
    vc_ref[...] = vfull[t:, :]
    newk_ref[0] = kfull[t:, :]
    newv_ref[0] = vfull[t:, :]
    kb = kfull.astype(BF16)
    vb = vfull.astype(BF16)
    if mask_first:
        col = lax.broadcasted_iota(jnp.int32, (t, t + WINDOW), 1)
        no_past = col < jnp.where(i == 0, WINDOW, 0)
    heads = []
    for hh in range(N_HEADS):
        g = hh // GROUP
        qh = q[:, hh * HEAD_DIM:(hh + 1) * HEAD_DIM].astype(BF16)
        kh = kb[:, g * HEAD_DIM:(g + 1) * HEAD_DIM]
        vh = vb[:, g * HEAD_DIM:(g + 1) * HEAD_DIM]
        s = lax.dot_general(qh, kh, NT_DIMS, preferred_element_type=F32) * (HEAD_DIM ** -0.5) + bias_ref[hh]
        if mask_first:
            s = jnp.where(no_past, NEG_INF, s)
        sink = sink_ref[hh]
        m = jnp.maximum(jnp.max(s, axis=-1, keepdims=True), sink)
        e = jnp.exp(s - m)
        denom = jnp.sum(e, axis=-1, keepdims=True) + jnp.exp(sink - m)
        o = jnp.dot(e.astype(BF16), vh, preferred_element_type=F32)
        heads.append(o * (1.0 / denom))
    ya = jnp.concatenate(heads, axis=-1)

    cb = proj(3, 4)
    u = proj(4, 5) * proj(5, 6)
    ubuf_ref[SUBLANES:t + SUBLANES, :] = u
    cw = convw_ref[...]
    yc = (cw[0:1, :] * ubuf_ref[SUBLANES - 2:t + SUBLANES - 2, :]
          + cw[1:2, :] * ubuf_ref[SUBLANES - 1:t + SUBLANES - 1, :] + cw[2:3, :] * u)
    yb = cb * yc
    tail = ubuf_ref[t:t + SUBLANES, :]
    newu_ref[0] = tail
    ubuf_ref[0:SUBLANES, :] = tail

    a_out = jnp.dot(ya.astype(BF16), woa_ref[...], preferred_element_type=F32)
    b_out = jnp.dot(yb.astype(BF16), wob_ref[...], preferred_element_type=F32)
    mixin = jax.nn.sigmoid(proj(6, 7)) * a_out + jax.nn.sigmoid(proj(7, 8)) * b_out
    mix = jnp.dot(mixin.astype(BF16), wo_ref[...], preferred_element_type=F32)
    x1 = _layernorm(alpha * x + (1.0 + g1) * mix, ln1g_ref[...], ln1b_ref[...])
    x1_ref[0] = x1

    h2 = x1 * (1.0 + sc2) + sh2
    _to_row_tiles(h2_ref, h2, t)
    logits = lax.dot_general(wr_ref[...], h2, NT_DIMS, precision=HIGHEST,
                             preferred_element_type=F32) + br_ref[...]
    eid = lax.broadcasted_iota(jnp.int32, (N_EXPERTS, t), 0)
    vals, ids = [], []
    for _ in range(TOP_K):
        mx = jnp.max(logits, axis=0, keepdims=True)
        sel = jnp.min(jnp.where(logits == mx, eid, N_EXPERTS), axis=0, keepdims=True)
        vals.append(mx)
        ids.append(sel)
        logits = jnp.where(eid == sel, -jnp.inf, logits)
    ex = [jnp.exp(v - vals[0]) for v in vals]
    tot = ex[0] + ex[1] + ex[2] + ex[3]
    idx_ref[0] = jnp.concatenate(ids, axis=0)
    gate_ref[0] = jnp.concatenate([e_ / tot for e_ in ex], axis=0)


def _mix(x, mod, ln0_g, ln0_b, w_in, b_in, bias, sink, conv_w, w_oa, w_ob, w_o, ln1_g, ln1_b,
         w_rt, b_r, k0, v0, u0, *, tile, apply_ln0, mask_first, alpha):
    b, s, d = x.shape
    n_t = s // tile
    const = lambda shape: pl.BlockSpec(shape, lambda bb, ii: (0,) * len(shape), pipeline_mode=pl.Buffered(1))
    per_b = lambda shape: pl.BlockSpec((1,) + shape, lambda bb, ii: (bb,) + (0,) * len(shape))
    kern = functools.partial(_mix_kernel, tile=tile, apply_ln0=apply_ln0, mask_first=mask_first, alpha=alpha)
    return pl.pallas_call(
        kern,
        out_shape=(
            jax.ShapeDtypeStruct((b, s, d), F32),
            jax.ShapeDtypeStruct((b * s * ROW_TILES, LANES), F32),
            jax.ShapeDtypeStruct((b * n_t, TOP_K, tile), jnp.int32),
            jax.ShapeDtypeStruct((b * n_t, TOP_K, tile), F32),
            jax.ShapeDtypeStruct((b, WINDOW, KV_WIDTH), F32),
            jax.ShapeDtypeStruct((b, WINDOW, KV_WIDTH), F32),
            jax.ShapeDtypeStruct((b, SUBLANES, CONV_WIDTH), F32),
        ),
        grid=(b, n_t),
        in_specs=[
            pl.BlockSpec((1, tile, d), lambda bb, ii: (bb, ii, 0)),
            per_b((6, d)),
            const((1, d)), const((1, d)),
            const((d, IN_WIDTH)), const((1, IN_WIDTH)),
            const((N_HEADS, tile, tile + WINDOW)),
            pl.BlockSpec(memory_space=pltpu.SMEM),
            const((CONV_K, CONV_WIDTH)),
            const((ATT_WIDTH, d)), const((CONV_WIDTH, d)), const((d, d)),
            const((1, d)), const((1, d)),
            const((N_EXPERTS, d)), const((N_EXPERTS, 1)),
            per_b((WINDOW, KV_WIDTH)), per_b((WINDOW, KV_WIDTH)), per_b((SUBLANES, CONV_WIDTH)),
        ],
        out_specs=(
            pl.BlockSpec((1, tile, d), lambda bb, ii: (bb, ii, 0)),
            pl.BlockSpec((tile * ROW_TILES, LANES), lambda bb, ii: (bb * n_t + ii, 0)),
            pl.BlockSpec((1, TOP_K, tile), lambda bb, ii: (bb * n_t + ii, 0, 0)),
            pl.BlockSpec((1, TOP_K, tile), lambda bb, ii: (bb * n_t + ii, 0, 0)),
            per_b((WINDOW, KV_WIDTH)), per_b((WINDOW, KV_WIDTH)), per_b((SUBLANES, CONV_WIDTH)),
        ),
        scratch_shapes=[
            pltpu.VMEM((WINDOW, KV_WIDTH), F32),
            pltpu.VMEM((WINDOW, KV_WIDTH), F32),
            pltpu.VMEM((tile + SUBLANES, CONV_WIDTH), F32),
        ],
        compiler_params=pltpu.CompilerParams(dimension_semantics=("arbitrary", "arbitrary"),
                                             vmem_limit_bytes=_vmem_limit(56)),
        name="mix",
    )(x, mod, ln0_g, ln0_b, w_in, b_in, bias, sink, conv_w, w_oa, w_ob, w_o, ln1_g, ln1_b, w_rt, b_r,
      k0, v0, u0)


def _route_kernel(idx_ref, dest_ref, cnt_ref, run_ref, start_ref):
    phase = pl.program_id(0)
    j = pl.program_id(1)
    t = ROUTE_TILE
    ids = idx_ref[...]
    eid = lax.broadcasted_iota(jnp.int32, (N_EXPERTS, t), 0)
    hits = [(eid == ids[k:k + 1, :]).astype(F32) for k in range(TOP_K)]
    hit = hits[0] + hits[1] + hits[2] + hits[3]

    @pl.when(jnp.logical_and(phase == 0, j == 0))
    def _():
        run_ref[...] = jnp.zeros_like(run_ref)

    @pl.when(phase == 0)
    def _():
        run_ref[...] += jnp.sum(hit, axis=1, keepdims=True)

    @pl.when(jnp.logical_and(phase == 1, j == 0))
    def _():
        cnt = run_ref[...]
        cnt_ref[...] = jnp.broadcast_to(cnt, cnt_ref.shape)
        padded = jnp.floor((cnt + (MOE_BLOCK - 1)) * (1.0 / MOE_BLOCK)) * MOE_BLOCK
        r = lax.broadcasted_iota(jnp.int32, (N_EXPERTS, N_EXPERTS), 0)
        c = lax.broadcasted_iota(jnp.int32, (N_EXPERTS, N_EXPERTS), 1)
        before = (c < r).astype(F32)
        start_ref[...] = jnp.dot(before, jnp.broadcast_to(padded, start_ref.shape), precision=HIGHEST,
                                 preferred_element_type=F32)
        run_ref[...] = jnp.zeros_like(run_ref)

    @pl.when(phase == 1)
    def _():
        r = lax.broadcasted_iota(jnp.int32, (t, t), 0)
        c = lax.broadcasted_iota(jnp.int32, (t, t), 1)
        upto = (r <= c).astype(BF16)
        incl = jnp.dot(hit.astype(BF16), upto, preferred_element_type=F32)
        pos = incl - hit + (start_ref[:, 0:1] + run_ref[...])
        for k in range(TOP_K):
            d = jnp.sum(hits[k] * pos, axis=0, keepdims=True).astype(jnp.int32)
            for gidx in range(t // DEST_GROUP):
                dest_ref[gidx, k:k + 1, :] = d[:, gidx * DEST_GROUP:(gidx + 1) * DEST_GROUP]
        run_ref[...] += jnp.sum(hit, axis=1, keepdims=True)


def _route(idx_all):
    n = idx_all.shape[1]
    n_t = n // ROUTE_TILE
    groups = ROUTE_TILE // DEST_GROUP
    return pl.pallas_call(
        _route_kernel,
        out_shape=(
            jax.ShapeDtypeStruct((n // DEST_GROUP, TOP_K, DEST_GROUP), jnp.int32),
            jax.ShapeDtypeStruct((N_EXPERTS, LANES), F32),
        ),
        grid=(2, n_t),
        in_specs=[pl.BlockSpec((TOP_K, ROUTE_TILE), lambda p, j: (0, j))],
        out_specs=(
            pl.BlockSpec((groups, TOP_K, DEST_GROUP), lambda p, j: (p * j, 0, 0)),
            pl.BlockSpec((N_EXPERTS, LANES), lambda p, j: (0, 0)),
        ),
        scratch_shapes=[pltpu.VMEM((N_EXPERTS, 1), F32), pltpu.VMEM((N_EXPERTS, LANES), F32)],
        compiler_params=pltpu.CompilerParams(dimension_semantics=("arbitrary", "arbitrary")),
        name="route",
    )(idx_all)


def _row(ref, r):
    return ref.at[pl.ds(pl.multiple_of(r * ROW_TILES, ROW_TILES), ROW_TILES), :]


def _scatter_kernel(cnt_ref, start_ref, nused_ref, dest_ref, hp_ref, hs_ref, xs_ref, zero_ref, sem, zsem,
                    *, n_prompt_tiles, n_blocks):
    j = pl.program_id(0)
    block_rows = MOE_BLOCK * ROW_TILES

    def zero_block_copy(blk):
        return pltpu.make_async_copy(
            zero_ref, xs_ref.at[pl.ds(pl.multiple_of(blk * block_rows, block_rows), block_rows), :], zsem)

    def zero_row_copy(r):
        return pltpu.make_async_copy(zero_ref.at[pl.ds(0, ROW_TILES), :], _row(xs_ref, r), zsem)

    @pl.when(j == 0)
    def _():
        zero_ref[...] = jnp.zeros_like(zero_ref)
        n_used = nused_ref[0]

        def per_expert(e, carry):
            lo = start_ref[e] + cnt_ref[e]
            hi = start_ref[e] + ((cnt_ref[e] + (MOE_BLOCK - 1)) // MOE_BLOCK) * MOE_BLOCK
            lax.fori_loop(lo, hi, lambda r, c: (zero_row_copy(r).start(), c)[1], 0)
            lax.fori_loop(lo, hi, lambda r, c: (zero_row_copy(r).wait(), c)[1], 0)
            return carry

        lax.fori_loop(0, N_EXPERTS, per_expert, 0)
        lax.fori_loop(n_used, n_blocks, lambda b_, c: (zero_block_copy(b_).start(), c)[1], 0)
        lax.fori_loop(n_used, n_blocks, lambda b_, c: (zero_block_copy(b_).wait(), c)[1], 0)

    def issue(src_ref):
        for g in range(ROUTE_TILE // DEST_GROUP):
            def body(tt, carry, g=g):
                tok = g * DEST_GROUP + tt
                for k in range(TOP_K):
                    pltpu.make_async_copy(_row(src_ref, tok), _row(xs_ref, dest_ref[g, k, tt]),
                                          sem).start(priority=k % 2)
                return carry
            lax.fori_loop(0, DEST_GROUP, body, 0)

    @pl.when(j < n_prompt_tiles)
    def _():
        issue(hp_ref)

    @pl.when(j >= n_prompt_tiles)
    def _():
        issue(hs_ref)

    for _ in range(TOP_K):
        pltpu.make_async_copy(hp_ref, xs_ref.at[pl.ds(0, ROUTE_TILE * ROW_TILES), :], sem).wait()


def _scatter(counts, starts, n_used, dest, h_prompt, h_sample, n_blocks):
    n_prompt_tiles = h_prompt.shape[0] // (ROUTE_TILE * ROW_TILES)
    n_sample_tiles = h_sample.shape[0] // (ROUTE_TILE * ROW_TILES)
    groups = ROUTE_TILE // DEST_GROUP
    kern = functools.partial(_scatter_kernel, n_prompt_tiles=n_prompt_tiles, n_blocks=n_blocks)
    return pl.pallas_call(
        kern,
        out_shape=jax.ShapeDtypeStruct((n_blocks * MOE_BLOCK * ROW_TILES, LANES), F32),
        grid_spec=pltpu.PrefetchScalarGridSpec(
            num_scalar_prefetch=3,
            grid=(n_prompt_tiles + n_sample_tiles,),
            in_specs=[
                pl.BlockSpec((groups, TOP_K, DEST_GROUP), lambda j, *_: (j, 0, 0), memory_space=pltpu.SMEM),
                pl.BlockSpec((ROUTE_TILE * ROW_TILES, LANES),
                             lambda j, *_: (jnp.minimum(j, n_prompt_tiles - 1), 0)),
                pl.BlockSpec((ROUTE_TILE * ROW_TILES, LANES),
                             lambda j, *_: (jnp.maximum(j - n_prompt_tiles, 0), 0)),
            ],
            out_specs=pl.BlockSpec(memory_space=pl.ANY),
            scratch_shapes=[
                pltpu.VMEM((MOE_BLOCK * ROW_TILES, LANES), F32),
                pltpu.SemaphoreType.DMA,
                pltpu.SemaphoreType.DMA,
            ],
        ),
        compiler_params=pltpu.CompilerParams(dimension_semantics=("arbitrary",)),
        name="scatter",
    )(counts, starts, n_used, dest, h_prompt, h_sample)


def _ffn_kernel(be_ref, nused_ref, xs_ref, wgu_ref, bgu_ref, wd_ref, bd_ref, ys_ref, wgu_bf, wd_bf):
    i = pl.program_id(0)
    n_used = nused_ref[0]

    @pl.when(i < n_used)
    def _():
        prev = be_ref[jnp.maximum(i - 1, 0)]

        @pl.when(jnp.logical_or(i == 0, be_ref[i] != prev))
        def _():
            wgu_bf[...] = wgu_ref[0, 0].astype(BF16)
            wd_bf[...] = wd_ref[0, 0].astype(BF16)

        x = _from_row_tiles(xs_ref, 0, MOE_BLOCK).astype(BF16)
        gu = jnp.dot(x, wgu_bf[...], preferred_element_type=F32) + bgu_ref[0, 0]
        g = jnp.minimum(gu[:, :D_FF], SWIGLU_LIMIT)
        lin = jnp.clip(gu[:, D_FF:], -SWIGLU_LIMIT, SWIGLU_LIMIT)
        a = g * jax.nn.sigmoid(SWIGLU_ALPHA * g) * (lin + 1.0)
        y = jnp.dot(a.astype(BF16), wd_bf[...], preferred_element_type=F32) + bd_ref[0, 0]
        _to_row_tiles(ys_ref, y, MOE_BLOCK)

    @pl.when(i >= n_used)
    def _():
        ys_ref[...] = jnp.zeros_like(ys_ref)


def _ffn(layer, block_e, n_used, xs, w_gu, b_gu, w_down, b_down, n_blocks):
    block_rows = MOE_BLOCK * ROW_TILES
    depth = w_gu.shape[0]

    def live(i, nu):
        return jnp.minimum(i, nu[0] - 1)

    def expert(i, be, nu):
        return (layer, be[live(i, nu)], 0, 0)

    return pl.pallas_call(
        _ffn_kernel,
        out_shape=jax.ShapeDtypeStruct(xs.shape, F32),
        grid_spec=pltpu.PrefetchScalarGridSpec(
            num_scalar_prefetch=2,
            grid=(n_blocks,),
            in_specs=[
                pl.BlockSpec((block_rows, LANES), lambda i, be, nu: (live(i, nu), 0)),
                pl.BlockSpec((1, 1, D_MODEL, 2 * D_FF), expert),
                pl.BlockSpec((1, 1, 1, 2 * D_FF), expert),
                pl.BlockSpec((1, 1, D_FF, D_MODEL), expert),
                pl.BlockSpec((1, 1, 1, D_MODEL), expert),
            ],
            out_specs=pl.BlockSpec((block_rows, LANES), lambda i, be, nu: (i, 0)),
            scratch_shapes=[pltpu.VMEM((D_MODEL, 2 * D_FF), BF16), pltpu.VMEM((D_FF, D_MODEL), BF16)],
        ),
        compiler_params=pltpu.CompilerParams(dimension_semantics=("arbitrary",),
                                             vmem_limit_bytes=_vmem_limit(56)),
        name="ffn",
    )(block_e, n_used, xs, w_gu, b_gu.reshape(depth, N_EXPERTS, 1, 2 * D_FF), w_down,
      b_down.reshape(depth, N_EXPERTS, 1, D_MODEL))


def _combine_kernel(dest_ref, x1_ref, mod_ref, gate_ref, ln2g_ref, ln2b_ref, ys_ref, o_ref, buf_ref, sem,
                    *, tile, alpha):
    t = tile
    for g in range(t // DEST_GROUP):
        def body(tt, carry, g=g):
            tok = g * DEST_GROUP + tt
            for k in range(TOP_K):
                pltpu.make_async_copy(_row(ys_ref, dest_ref[g, k, tt]), _row(buf_ref, k * t + tok), sem).start()
            return carry
        lax.fori_loop(0, DEST_GROUP, body, 0)
    pltpu.make_async_copy(ys_ref.at[pl.ds(0, TOP_K * t * ROW_TILES), :], buf_ref, sem).wait()

    gates = gate_ref[0]
    gates_t = jnp.transpose(jnp.concatenate([gates, jnp.zeros_like(gates)], axis=0))
    ff = jnp.zeros((t, D_MODEL), F32)
    for k in range(TOP_K):
        ff = ff + gates_t[:, k:k + 1] * _from_row_tiles(buf_ref, k * t, t)
    g2 = mod_ref[0][5:6, :]
    o_ref[0] = _layernorm(alpha * x1_ref[0] + (1.0 + g2) * ff, ln2g_ref[...], ln2b_ref[...])


def _combine(dest, group_offset, x1, mod, gates, ln2_g, ln2_b, ys, *, tile, alpha):
    b, s, d = x1.shape
    n_t = s // tile
    groups = tile // DEST_GROUP
    goff = group_offset // groups
    kern = functools.partial(_combine_kernel, tile=tile, alpha=alpha)
    return pl.pallas_call(
        kern,
        out_shape=jax.ShapeDtypeStruct((b, s, d), F32),
        grid=(b, n_t),
        in_specs=[
            pl.BlockSpec((groups, TOP_K, DEST_GROUP), lambda bb, ii: (goff + bb * n_t + ii, 0, 0),
                         memory_space=pltpu.SMEM),
            pl.BlockSpec((1, tile, d), lambda bb, ii: (bb, ii, 0)),
            pl.BlockSpec((1, 6, d), lambda bb, ii: (bb, 0, 0)),
            pl.BlockSpec((1, TOP_K, tile), lambda bb, ii: (bb * n_t + ii, 0, 0)),
            pl.BlockSpec((1, d), lambda bb, ii: (0, 0)),
            pl.BlockSpec((1, d), lambda bb, ii: (0, 0)),
            pl.BlockSpec(memory_space=pl.ANY),
        ],
        out_specs=pl.BlockSpec((1, tile, d), lambda bb, ii: (bb, ii, 0)),
        scratch_shapes=[pltpu.VMEM((TOP_K * tile * ROW_TILES, LANES), F32), pltpu.SemaphoreType.DMA],
        compiler_params=pltpu.CompilerParams(dimension_semantics=("arbitrary", "arbitrary"),
                                             vmem_limit_bytes=_vmem_limit(32)),
        name="combine",
    )(dest, x1, mod, gates, ln2_g, ln2_b, ys)


def kernel(x_prompt, x_sample, c_prompt, c_sample, cache_k, cache_v, state_conv, rel_table, ln0_g, ln0_b, w_ada, b_ada, w_in, b_in, sinks, conv_w, w_oa, w_ob, w_o, ln1_g, ln1_b, w_router, b_router, w_gu, b_gu, w_down, b_down, ln2_g, ln2_b):
    depth = w_ada.shape[0]
    bp, sp, d = x_prompt.shape
    bs, ss, _ = x_sample.shape
    alpha = (2 * depth) ** 0.25
    n_prompt, n_sample = bp * sp, bs * ss
    n_tok = n_prompt + n_sample
    assert n_prompt % ROUTE_TILE == 0 and n_sample % ROUTE_TILE == 0 and sp % PROMPT_TILE == 0
    assert ss % DEST_GROUP == 0 and ss <= WINDOW
    n_blocks = -(-(n_tok * TOP_K) // MOE_BLOCK) + N_EXPERTS

    mod_all = _ada(jnp.concatenate([c_prompt, c_sample], axis=0), w_ada, b_ada)
    mod_all = mod_all.reshape(depth, bp + bs, 6, d)
    bias_p = _bias_table(rel_table, PROMPT_TILE)
    bias_s = _bias_table(rel_table, ss)
    row = lambda a: a.reshape(1, -1)
    zeros_kv = jnp.zeros((bp, WINDOW, KV_WIDTH), F32)
    zeros_u = jnp.zeros((bp, SUBLANES, CONV_WIDTH), F32)

    y_p, y_s = x_prompt, x_sample
    outs = {name: [] for name in ("kp", "vp", "up", "ks", "vs", "us")}
    for l in range(depth):
        shared = (w_in[l].astype(BF16), row(b_in[l]))
        tail = (sinks[l], conv_w[l], w_oa[l].astype(BF16), w_ob[l].astype(BF16), w_o[l].astype(BF16),
                row(ln1_g[l]), row(ln1_b[l]), w_router[l].T, b_router[l].reshape(N_EXPERTS, 1))
        mod_p, mod_s = mod_all[l, :bp], mod_all[l, bp:]
        x1_p, h2_p, idx_p, gate_p, k_p, v_p, u_p = _mix(
            y_p, mod_p, row(ln0_g), row(ln0_b), *shared, bias_p, *tail, zeros_kv, zeros_kv, zeros_u,
            tile=PROMPT_TILE, apply_ln0=(l == 0), mask_first=True, alpha=alpha)
        u0 = jnp.pad(state_conv[l], ((0, 0), (SUBLANES - (CONV_K - 1), 0), (0, 0)))
        x1_s, h2_s, idx_s, gate_s, k_s, v_s, u_s = _mix(
            y_s, mod_s, row(ln0_g), row(ln0_b), *shared, bias_s, *tail,
            cache_k[l].reshape(bs, WINDOW, KV_WIDTH), cache_v[l].reshape(bs, WINDOW, KV_WIDTH), u0,
            tile=ss, apply_ln0=(l == 0), mask_first=False, alpha=alpha)

        idx_all = jnp.concatenate([jnp.transpose(idx_p, (1, 0, 2)).reshape(TOP_K, n_prompt),
                                   jnp.transpose(idx_s, (1, 0, 2)).reshape(TOP_K, n_sample)], axis=1)
        dest, cnt = _route(idx_all)
        counts = cnt[:, 0].astype(jnp.int32)
        padded = (counts + MOE_BLOCK - 1) // MOE_BLOCK * MOE_BLOCK
        pad_end = jnp.cumsum(padded)
        starts = pad_end - padded
        n_used = (pad_end[-1:] // MOE_BLOCK).astype(jnp.int32)
        block_start = jnp.arange(n_blocks, dtype=jnp.int32)[:, None] * MOE_BLOCK
        block_e = jnp.minimum(jnp.sum((pad_end[None, :] <= block_start).astype(jnp.int32), axis=1),
                              N_EXPERTS - 1)

        xs = _scatter(counts, starts, n_used, dest, h2_p, h2_s, n_blocks)
        ys = _ffn(l, block_e, n_used, xs, w_gu, b_gu, w_down, b_down, n_blocks)
        y_p = _combine(dest, 0, x1_p, mod_p, gate_p, row(ln2_g[l]), row(ln2_b[l]), ys,
                       tile=PROMPT_TILE, alpha=alpha)
        y_s = _combine(dest, n_prompt // DEST_GROUP, x1_s, mod_s, gate_s, row(ln2_g[l]), row(ln2_b[l]), ys,
                       tile=ss, alpha=alpha)

        outs["kp"].append(k_p.reshape(bp, WINDOW, N_KV_HEADS, HEAD_DIM))
        outs["vp"].append(v_p.reshape(bp, WINDOW, N_KV_HEADS, HEAD_DIM))
        outs["up"].append(u_p[:, SUBLANES - (CONV_K - 1):, :])
        outs["ks"].append(k_s.reshape(bs, WINDOW, N_KV_HEADS, HEAD_DIM))
        outs["vs"].append(v_s.reshape(bs, WINDOW, N_KV_HEADS, HEAD_DIM))
        outs["us"].append(u_s[:, SUBLANES - (CONV_K - 1):, :])
    return (y_p, y_s, jnp.stack(outs["kp"]), jnp.stack(outs["vp"]), jnp.stack(outs["up"]),
            jnp.stack(outs["ks"]), jnp.stack(outs["vs"]), jnp.stack(outs["us"]))
```

```python
import functools
import math

import jax
import jax.numpy as jnp
import numpy as np
from jax import lax
from jax.experimental import pallas as pl
from jax.experimental.pallas import tpu as pltpu

D_MODEL = 1024
CHUNK = 64
N_HEADS = 8
N_KV_HEADS = 2
HEAD_DIM = 64
GROUP = N_HEADS // N_KV_HEADS
ATT_WIDTH = N_HEADS * HEAD_DIM
KV_WIDTH = N_KV_HEADS * HEAD_DIM
WINDOW = 128
CONV_WIDTH = 512
CONV_K = 3
NUM_BUCKETS = 32
MAX_DISTANCE = 128
N_EXPERTS = 32
TOP_K = 4
D_FF = 1024
SWIGLU_LIMIT = 7.0
SWIGLU_ALPHA = 1.702
MOE_BLOCK = 256
LN_EPS = 1e-5
NEG_INF = -1e30
IN_SIZES = (ATT_WIDTH, KV_WIDTH, KV_WIDTH, CONV_WIDTH, CONV_WIDTH, CONV_WIDTH, D_MODEL, D_MODEL)
IN_WIDTH = sum(IN_SIZES)
IN_OFFS = tuple(int(s) for s in np.cumsum((0,) + IN_SIZES))

SUBLANES = 8
LANES = 128
ROW_TILES = D_MODEL // LANES
assert ROW_TILES == SUBLANES

PROMPT_TILE = 512
COMBINE_TILE = 256
ROUTE_TILE = 512
DEST_GROUP = 64

F32 = jnp.float32
BF16 = jnp.bfloat16
HIGHEST = lax.Precision.HIGHEST
NT_DIMS = (((1,), (1,)), ((), ()))


def _vmem_limit(mib):
    return mib * 1024 * 1024


def _layernorm(x, g, b):
    mu = jnp.mean(x, axis=-1, keepdims=True)
    xc = x - mu
    var = jnp.mean(xc * xc, axis=-1, keepdims=True)
    return xc * lax.rsqrt(var + LN_EPS) * g + b


def _to_row_tiles(ref, x, rows):
    for s in range(ROW_TILES):
        ref[pl.ds(s, rows, stride=ROW_TILES), :] = x[:, s * LANES:(s + 1) * LANES]


def _from_row_tiles(ref, base, rows):
    return jnp.concatenate(
        [ref[pl.ds(base * ROW_TILES + s, rows, stride=ROW_TILES), :] for s in range(ROW_TILES)], axis=-1)


def _ada_kernel(c_ref, w_ref, b_ref, o_ref):
    c = c_ref[...]
    s = c * jax.nn.sigmoid(c)
    o_ref[0] = jnp.dot(s, w_ref[0], precision=HIGHEST, preferred_element_type=F32) + b_ref[0]


def _ada(c_all, w_ada, b_ada):
    depth = w_ada.shape[0]
    nb = c_all.shape[0]
    n_col = 6 * D_MODEL // D_MODEL
    return pl.pallas_call(
        _ada_kernel,
        out_shape=jax.ShapeDtypeStruct((depth, nb, 6 * D_MODEL), F32),
        grid=(depth, n_col),
        in_specs=[
            pl.BlockSpec((nb, D_MODEL), lambda l, j: (0, 0)),
            pl.BlockSpec((1, D_MODEL, D_MODEL), lambda l, j: (l, 0, j)),
            pl.BlockSpec((1, 1, D_MODEL), lambda l, j: (l, 0, j)),
        ],
        out_specs=pl.BlockSpec((1, nb, D_MODEL), lambda l, j: (l, 0, j)),
        compiler_params=pltpu.CompilerParams(dimension_semantics=("arbitrary", "arbitrary"),
                                             vmem_limit_bytes=_vmem_limit(32)),
        name="ada",
    )(c_all, w_ada, b_ada.reshape(depth, 1, 6 * D_MODEL))


def _rel_bucket(rel):
    half = NUM_BUCKETS // 2
    max_exact = half // 2
    n = jnp.abs(rel)
    n_f = jnp.maximum(n, 1).astype(jnp.float32)
    large = max_exact + (jnp.log(n_f / max_exact) / math.log(MAX_DISTANCE / max_exact)
                         * (half - max_exact)).astype(jnp.int32)
    large = jnp.minimum(large, half - 1)
    return jnp.where(rel > 0, half, 0) + jnp.where(n < max_exact, n, large)


BAND = WINDOW + CHUNK
PAIR_ROWS = 2 * CHUNK
PAIR_COLS = 2 * BAND


def _band_codes():
    r = jnp.arange(PAIR_ROWS)[:, None]
    j = jnp.arange(PAIR_COLS)[None, :]
    bucket = _rel_bucket(j % BAND - WINDOW - r % CHUNK)
    head = 2 * (r // CHUNK) + j // BAND
    return (bucket + NUM_BUCKETS * head).astype(jnp.int32)


def _bias_kernel(table_ref, code_ref, o_ref):
    g = pl.program_id(0)
    code = code_ref[...]
    acc = jnp.zeros(code.shape, F32)
    for hq in range(GROUP):
        for i in range(NUM_BUCKETS):
            acc = jnp.where(code == hq * NUM_BUCKETS + i, table_ref[i, g * GROUP + hq], acc)
    o_ref[0] = acc


def _bias_table(rel_table):
    return pl.pallas_call(
        _bias_kernel,
        out_shape=jax.ShapeDtypeStruct((N_KV_HEADS, PAIR_ROWS, PAIR_COLS), F32),
        grid=(N_KV_HEADS,),
        in_specs=[
            pl.BlockSpec(memory_space=pltpu.SMEM),
            pl.BlockSpec((PAIR_ROWS, PAIR_COLS), lambda g: (0, 0)),
        ],
        out_specs=pl.BlockSpec((1, PAIR_ROWS, PAIR_COLS), lambda g: (g, 0, 0)),
        compiler_params=pltpu.CompilerParams(dimension_semantics=("arbitrary",)),
        name="rel_bias",
    )(rel_table, _band_codes())


def _mix_kernel(x_ref, mod_ref, ln0g_ref, ln0b_ref, win_ref, bin_ref, bias_ref, sink_ref, convw_ref,
                woa_ref, wob_ref, wo_ref, ln1g_ref, ln1b_ref, wr_ref, br_ref, k0_ref, v0_ref, u0_ref,
                x1_ref, h2_ref, idx_ref, gate_ref, newk_ref, newv_ref, newu_ref,
                kc_ref, vc_ref, ubuf_ref, *, tile, apply_ln0, mask_first, alpha):
    i = pl.program_id(1)
    t = tile

    @pl.when(i == 0)
    def _():
        kc_ref[...] = k0_ref[0]
        vc_ref[...] = v0_ref[0]
        ubuf_ref[0:SUBLANES, :] = u0_ref[0]

    x = x_ref[0]
    if apply_ln0:
        x = _layernorm(x, ln0g_ref[...], ln0b_ref[...])
    mod = mod_ref[0]
    sh1, sc1, g1, sh2, sc2, g2 = [mod[j:j + 1, :] for j in range(6)]
    h = (x * (1.0 + sc1) + sh1).astype(BF16)

    def proj(j0, j1):
        lo, hi = IN_OFFS[j0], IN_OFFS[j1]
        return jnp.dot(h, win_ref[:, lo:hi], preferred_element_type=F32) + bin_ref[:, lo:hi]

    q = proj(0, 1)
    kv = proj(1, 3)
    kfull = jnp.concatenate([kc_ref[...], kv[:, :KV_WIDTH]], axis=0)
    vfull = jnp.concatenate([vc_ref[...], kv[:, KV_WIDTH:]], axis=0)
    kc_ref[...] = kfull[t:, :]
    vc_ref[...] = vfull[t:, :]
    newk_ref[0] = kfull[t:, :]
    newv_ref[0] = vfull[t:, :]
    low = lax.broadcasted_iota(jnp.int32, kfull.shape, 1) < HEAD_DIM
    k_sw = pltpu.roll(kfull, HEAD_DIM, axis=1)
    v_sw = pltpu.roll(vfull, HEAD_DIM, axis=1)
    k_even = [jnp.where(low, kfull, 0.0).astype(BF16), jnp.where(low, k_sw, 0.0).astype(BF16)]
    k_odd = [jnp.where(low, 0.0, k_sw).astype(BF16), jnp.where(low, 0.0, kfull).astype(BF16)]
    v_even = [jnp.where(low, vfull, 0.0).astype(BF16), jnp.where(low, v_sw, 0.0).astype(BF16)]
    v_odd = [jnp.where(low, 0.0, v_sw).astype(BF16), jnp.where(low, 0.0, vfull).astype(BF16)]
    col = lax.broadcasted_iota(jnp.int32, (PAIR_ROWS, PAIR_COLS), 1)
    even = col < BAND
    band_col = jnp.where(even, col, col - BAND)
    first_pair = lax.broadcasted_iota(jnp.int32, (PAIR_ROWS, 1), 0) < CHUNK
    out_low = lax.broadcasted_iota(jnp.int32, (PAIR_ROWS, LANES), 1) < HEAD_DIM
    sink_even = [jnp.where(first_pair, sink_ref[g * GROUP], sink_ref[g * GROUP + 2]) for g in range(N_KV_HEADS)]
    sink_odd = [jnp.where(first_pair, sink_ref[g * GROUP + 1], sink_ref[g * GROUP + 3])
                for g in range(N_KV_HEADS)]
    units = [(c, g) for c in range(t // CHUNK) for g in range(N_KV_HEADS)]
    scores = []
    for c, g in units:
        r0 = c * CHUNK
        ql = jnp.concatenate([q[r0:r0 + CHUNK, (2 * g) * LANES:(2 * g + 1) * LANES],
                              q[r0:r0 + CHUNK, (2 * g + 1) * LANES:(2 * g + 2) * LANES]], axis=0)
        kp = jnp.concatenate([k_even[g][r0:r0 + BAND], k_odd[g][r0:r0 + BAND]], axis=0)
        s = lax.dot_general(ql.astype(BF16), kp, NT_DIMS, preferred_element_type=F32)
        s = s * (HEAD_DIM ** -0.5) + bias_ref[g]
        if mask_first and c < WINDOW // CHUNK:
            s = jnp.where(band_col < jnp.where(i == 0, WINDOW - r0, 0), NEG_INF, s)
        scores.append(s)
    maxes = []
    for (c, g), s in zip(units, scores):
        m_e = jnp.maximum(jnp.max(jnp.where(even, s, -jnp.inf), axis=-1, keepdims=True), sink_even[g])
        m_o = jnp.maximum(jnp.max(jnp.where(even, -jnp.inf, s), axis=-1, keepdims=True), sink_odd[g])
        maxes.append((m_e, m_o))
    exps, scales = [], []
    for (c, g), s, (m_e, m_o) in zip(units, scores, maxes):
        e = jnp.exp(s - jnp.where(even, m_e, m_o))
        d_e = jnp.sum(jnp.where(even, e, 0.0), axis=-1, keepdims=True) + jnp.exp(sink_even[g] - m_e)
        d_o = jnp.sum(jnp.where(even, 0.0, e), axis=-1, keepdims=True) + jnp.exp(sink_odd[g] - m_o)
        exps.append(e.astype(BF16))
        scales.append(jnp.where(out_low, 1.0 / d_e, 1.0 / d_o))
    outs = {}
    for (c, g), e, scale in zip(units, exps, scales):
        r0 = c * CHUNK
        vp = jnp.concatenate([v_even[g][r0:r0 + BAND], v_odd[g][r0:r0 + BAND]], axis=0)
        outs[c, g] = jnp.dot(e, vp, preferred_element_type=F32) * scale
    ya = jnp.concatenate(
        [jnp.concatenate([outs[c, g][half * CHUNK:(half + 1) * CHUNK] for g in range(N_KV_HEADS)
                          for half in range(2)], axis=-1) for c in range(t // CHUNK)], axis=0)

    cb = proj(3, 4)
    u = proj(4, 5) * proj(5, 6)
    ubuf_ref[SUBLANES:t + SUBLANES, :] = u
    cw = convw_ref[...]
    yc = (cw[0:1, :] * ubuf_ref[SUBLANES - 2:t + SUBLANES - 2, :]
          + cw[1:2, :] * ubuf_ref[SUBLANES - 1:t + SUBLANES - 1, :] + cw[2:3, :] * u)
    yb = cb * yc
    tail = ubuf_ref[t:t + SUBLANES, :]
    newu_ref[0] = tail
    ubuf_ref[0:SUBLANES, :] = tail

    a_out = jnp.dot(ya.astype(BF16), woa_ref[...], preferred_element_type=F32)
    b_out = jnp.dot(yb.astype(BF16), wob_ref[...], preferred_element_type=F32)
    mixin = jax.nn.sigmoid(proj(6, 7)) * a_out + jax.nn.sigmoid(proj(7, 8)) * b_out
    mix = jnp.dot(mixin.astype(BF16), wo_ref[...], preferred_element_type=F32)
    x1 = _layernorm(alpha * x + (1.0 + g1) * mix, ln1g_ref[...], ln1b_ref[...])
    x1_ref[0] = x1

    h2 = x1 * (1.0 + sc2) + sh2
    _to_row_tiles(h2_ref, h2, t)
    logits = lax.dot_general(wr_ref[...], h2.astype(BF16), NT_DIMS, preferred_element_type=F32) + br_ref[...]
    eid = lax.broadcasted_iota(jnp.int32, (N_EXPERTS, t), 0)
    vals, ids = [], []
    for _ in range(TOP_K):
        mx = jnp.max(logits, axis=0, keepdims=True)
        sel = jnp.min(jnp.where(logits == mx, eid, N_EXPERTS), axis=0, keepdims=True)
        vals.append(mx)
        ids.append(sel)
        logits = jnp.where(eid == sel, -jnp.inf, logits)
    ex = [jnp.exp(v - vals[0]) for v in vals]
    tot = ex[0] + ex[1] + ex[2] + ex[3]
    idx_ref[0] = jnp.concatenate(ids, axis=0)
    gate_ref[0] = jnp.concatenate([e_ / tot for e_ in ex], axis=0)


def _mix(x, mod, ln0_g, ln0_b, w_in, b_in, bias, sink, conv_w, w_oa, w_ob, w_o, ln1_g, ln1_b,
         w_rt, b_r, k0, v0, u0, *, tile, apply_ln0, mask_first, alpha):
    b, s, d = x.shape
    n_t = s // tile
    const = lambda shape: pl.BlockSpec(shape, lambda bb, ii: (0,) * len(shape), pipeline_mode=pl.Buffered(1))
    per_b = lambda shape: pl.BlockSpec((1,) + shape, lambda bb, ii: (bb,) + (0,) * len(shape))
    kern = functools.partial(_mix_kernel, tile=tile, apply_ln0=apply_ln0, mask_first=mask_first, alpha=alpha)
    return pl.pallas_call(
        kern,
        out_shape=(
            jax.ShapeDtypeStruct((b, s, d), F32),
            jax.ShapeDtypeStruct((b * s * ROW_TILES, LANES), F32),
            jax.ShapeDtypeStruct((b * n_t, TOP_K, tile), jnp.int32),
            jax.ShapeDtypeStruct((b * n_t, TOP_K, tile), F32),
            jax.ShapeDtypeStruct((b, WINDOW, KV_WIDTH), F32),
            jax.ShapeDtypeStruct((b, WINDOW, KV_WIDTH), F32),
            jax.ShapeDtypeStruct((b, SUBLANES, CONV_WIDTH), F32),
        ),
        grid=(b, n_t),
        in_specs=[
            pl.BlockSpec((1, tile, d), lambda bb, ii: (bb, ii, 0)),
            per_b((6, d)),
            const((1, d)), const((1, d)),
            const((d, IN_WIDTH)), const((1, IN_WIDTH)),
            const((N_KV_HEADS, PAIR_ROWS, PAIR_COLS)),
            pl.BlockSpec(memory_space=pltpu.SMEM),
            const((CONV_K, CONV_WIDTH)),
            const((ATT_WIDTH, d)), const((CONV_WIDTH, d)), const((d, d)),
            const((1, d)), const((1, d)),
            const((N_EXPERTS, d)), const((N_EXPERTS, 1)),
            per_b((WINDOW, KV_WIDTH)), per_b((WINDOW, KV_WIDTH)), per_b((SUBLANES, CONV_WIDTH)),
        ],
        out_specs=(
            pl.BlockSpec((1, tile, d), lambda bb, ii: (bb, ii, 0)),
            pl.BlockSpec((tile * ROW_TILES, LANES), lambda bb, ii: (bb * n_t + ii, 0)),
            pl.BlockSpec((1, TOP_K, tile), lambda bb, ii: (bb * n_t + ii, 0, 0)),
            pl.BlockSpec((1, TOP_K, tile), lambda bb, ii: (bb * n_t + ii, 0, 0)),
            per_b((WINDOW, KV_WIDTH)), per_b((WINDOW, KV_WIDTH)), per_b((SUBLANES, CONV_WIDTH)),
        ),
        scratch_shapes=[
            pltpu.VMEM((WINDOW, KV_WIDTH), F32),
            pltpu.VMEM((WINDOW, KV_WIDTH), F32),
            pltpu.VMEM((tile + SUBLANES, CONV_WIDTH), F32),
        ],
        compiler_params=pltpu.CompilerParams(dimension_semantics=("arbitrary", "arbitrary"),
                                             vmem_limit_bytes=_vmem_limit(56)),
        name="mix",
    )(x, mod, ln0_g, ln0_b, w_in, b_in, bias, sink, conv_w, w_oa, w_ob, w_o, ln1_g, ln1_b, w_rt, b_r,
      k0, v0, u0)


def _route_kernel(idx_ref, dest_ref, cnt_ref, run_ref, start_ref):
    phase = pl.program_id(0)
    j = pl.program_id(1)
    t = ROUTE_TILE
    ids = idx_ref[...]
    eid = lax.broadcasted_iota(jnp.int32, (N_EXPERTS, t), 0)
    hits = [(eid == ids[k:k + 1, :]).astype(F32) for k in range(TOP_K)]
    hit = hits[0] + hits[1] + hits[2] + hits[3]

    @pl.when(jnp.logical_and(phase == 0, j == 0))
    def _():
        run_ref[...] = jnp.zeros_like(run_ref)

    @pl.when(phase == 0)
    def _():
        run_ref[...] += jnp.sum(hit, axis=1, keepdims=True)

    @pl.when(jnp.logical_and(phase == 1, j == 0))
    def _():
        cnt = run_ref[...]
        cnt_ref[...] = jnp.broadcast_to(cnt, cnt_ref.shape)
        padded = jnp.floor((cnt + (MOE_BLOCK - 1)) * (1.0 / MOE_BLOCK)) * MOE_BLOCK
        r = lax.broadcasted_iota(jnp.int32, (N_EXPERTS, N_EXPERTS), 0)
        c = lax.broadcasted_iota(jnp.int32, (N_EXPERTS, N_EXPERTS), 1)
        before = (c < r).astype(F32)
        start_ref[...] = jnp.dot(before, jnp.broadcast_to(padded, start_ref.shape), precision=HIGHEST,
                                 preferred_element_type=F32)
        run_ref[...] = jnp.zeros_like(run_ref)

    @pl.when(phase == 1)
    def _():
        r = lax.broadcasted_iota(jnp.int32, (t, t), 0)
        c = lax.broadcasted_iota(jnp.int32, (t, t), 1)
        upto = (r <= c).astype(BF16)
        incl = jnp.dot(hit.astype(BF16), upto, preferred_element_type=F32)
        pos = incl - hit + (start_ref[:, 0:1] + run_ref[...])
        for k in range(TOP_K):
            d = jnp.sum(hits[k] * pos, axis=0, keepdims=True).astype(jnp.int32)
            for gidx in range(t // DEST_GROUP):
                dest_ref[gidx, k:k + 1, :] = d[:, gidx * DEST_GROUP:(gidx + 1) * DEST_GROUP]
        run_ref[...] += jnp.sum(hit, axis=1, keepdims=True)


def _route(idx_all):
    n = idx_all.shape[1]
    n_t = n // ROUTE_TILE
    groups = ROUTE_TILE // DEST_GROUP
    return pl.pallas_call(
        _route_kernel,
        out_shape=(
            jax.ShapeDtypeStruct((n // DEST_GROUP, TOP_K, DEST_GROUP), jnp.int32),
            jax.ShapeDtypeStruct((N_EXPERTS, LANES), F32),
        ),
        grid=(2, n_t),
        in_specs=[pl.BlockSpec((TOP_K, ROUTE_TILE), lambda p, j: (0, j))],
        out_specs=(
            pl.BlockSpec((groups, TOP_K, DEST_GROUP), lambda p, j: (p * j, 0, 0)),
            pl.BlockSpec((N_EXPERTS, LANES), lambda p, j: (0, 0)),
        ),
        scratch_shapes=[pltpu.VMEM((N_EXPERTS, 1), F32), pltpu.VMEM((N_EXPERTS, LANES), F32)],
        compiler_params=pltpu.CompilerParams(dimension_semantics=("arbitrary", "arbitrary")),
        name="route",
    )(idx_all)


def _row(ref, r):
    return ref.at[pl.ds(pl.multiple_of(r * ROW_TILES, ROW_TILES), ROW_TILES), :]


def _scatter_kernel(cnt_ref, start_ref, nused_ref, dest_ref, hp_ref, hs_ref, xs_ref, zero_ref, sem, zsem,
                    *, n_prompt_tiles, n_blocks):
    j = pl.program_id(0)
    block_rows = MOE_BLOCK * ROW_TILES

    def zero_block_copy(blk):
        return pltpu.make_async_copy(
            zero_ref, xs_ref.at[pl.ds(pl.multiple_of(blk * block_rows, block_rows), block_rows), :], zsem)

    def zero_row_copy(r):
        return pltpu.make_async_copy(zero_ref.at[pl.ds(0, ROW_TILES), :], _row(xs_ref, r), zsem)

    @pl.when(j == 0)
    def _():
        zero_ref[...] = jnp.zeros_like(zero_ref)
        n_used = nused_ref[0]

        def per_expert(e, carry):
            lo = start_ref[e] + cnt_ref[e]
            hi = start_ref[e] + ((cnt_ref[e] + (MOE_BLOCK - 1)) // MOE_BLOCK) * MOE_BLOCK
            lax.fori_loop(lo, hi, lambda r, c: (zero_row_copy(r).start(), c)[1], 0)
            lax.fori_loop(lo, hi, lambda r, c: (zero_row_copy(r).wait(), c)[1], 0)
            return carry

        lax.fori_loop(0, N_EXPERTS, per_expert, 0)
        lax.fori_loop(n_used, n_blocks, lambda b_, c: (zero_block_copy(b_).start(), c)[1], 0)
        lax.fori_loop(n_used, n_blocks, lambda b_, c: (zero_block_copy(b_).wait(), c)[1], 0)

    def issue(src_ref):
        for g in range(ROUTE_TILE // DEST_GROUP):
            def body(tt, carry, g=g):
                tok = g * DEST_GROUP + tt
                for k in range(TOP_K):
                    pltpu.make_async_copy(_row(src_ref, tok), _row(xs_ref, dest_ref[g, k, tt]),
                                          sem).start(priority=k % 2)
                return carry
            lax.fori_loop(0, DEST_GROUP, body, 0)

    @pl.when(j < n_prompt_tiles)
    def _():
        issue(hp_ref)

    @pl.when(j >= n_prompt_tiles)
    def _():
        issue(hs_ref)

    for _ in range(TOP_K):
        pltpu.make_async_copy(hp_ref, xs_ref.at[pl.ds(0, ROUTE_TILE * ROW_TILES), :], sem).wait()


def _scatter(counts, starts, n_used, dest, h_prompt, h_sample, n_blocks):
    n_prompt_tiles = h_prompt.shape[0] // (ROUTE_TILE * ROW_TILES)
    n_sample_tiles = h_sample.shape[0] // (ROUTE_TILE * ROW_TILES)
    groups = ROUTE_TILE // DEST_GROUP
    kern = functools.partial(_scatter_kernel, n_prompt_tiles=n_prompt_tiles, n_blocks=n_blocks)
    return pl.pallas_call(
        kern,
        out_shape=jax.ShapeDtypeStruct((n_blocks * MOE_BLOCK * ROW_TILES, LANES), F32),
        grid_spec=pltpu.PrefetchScalarGridSpec(
            num_scalar_prefetch=3,
            grid=(n_prompt_tiles + n_sample_tiles,),
            in_specs=[
                pl.BlockSpec((groups, TOP_K, DEST_GROUP), lambda j, *_: (j, 0, 0), memory_space=pltpu.SMEM),
                pl.BlockSpec((ROUTE_TILE * ROW_TILES, LANES),
                             lambda j, *_: (jnp.minimum(j, n_prompt_tiles - 1), 0)),
                pl.BlockSpec((ROUTE_TILE * ROW_TILES, LANES),
                             lambda j, *_: (jnp.maximum(j - n_prompt_tiles, 0), 0)),
            ],
            out_specs=pl.BlockSpec(memory_space=pl.ANY),
            scratch_shapes=[
                pltpu.VMEM((MOE_BLOCK * ROW_TILES, LANES), F32),
                pltpu.SemaphoreType.DMA,
                pltpu.SemaphoreType.DMA,
            ],
        ),
        compiler_params=pltpu.CompilerParams(dimension_semantics=("arbitrary",)),
        name="scatter",
    )(counts, starts, n_used, dest, h_prompt, h_sample)


def _ffn_kernel(be_ref, nused_ref, xs_ref, wgu_ref, bgu_ref, wd_ref, bd_ref, ys_ref, wgu_bf, wd_bf):
    i = pl.program_id(0)
    n_used = nused_ref[0]

    @pl.when(i < n_used)
    def _():
        prev = be_ref[jnp.maximum(i - 1, 0)]

        @pl.when(jnp.logical_or(i == 0, be_ref[i] != prev))
        def _():
            wgu_bf[...] = wgu_ref[0, 0].astype(BF16)
            wd_bf[...] = wd_ref[0, 0].astype(BF16)

        x = _from_row_tiles(xs_ref, 0, MOE_BLOCK).astype(BF16)
        gu = jnp.dot(x, wgu_bf[...], preferred_element_type=F32) + bgu_ref[0, 0]
        g = jnp.minimum(gu[:, :D_FF], SWIGLU_LIMIT)
        lin = jnp.clip(gu[:, D_FF:], -SWIGLU_LIMIT, SWIGLU_LIMIT)
        a = g * jax.nn.sigmoid(SWIGLU_ALPHA * g) * (lin + 1.0)
        y = jnp.dot(a.astype(BF16), wd_bf[...], preferred_element_type=F32) + bd_ref[0, 0]
        _to_row_tiles(ys_ref, y, MOE_BLOCK)

    @pl.when(i >= n_used)
    def _():
        ys_ref[...] = jnp.zeros_like(ys_ref)


def _ffn(layer, block_e, n_used, xs, w_gu, b_gu, w_down, b_down, n_blocks):
    block_rows = MOE_BLOCK * ROW_TILES
    depth = w_gu.shape[0]

    def live(i, nu):
        return jnp.minimum(i, nu[0] - 1)

    def expert(i, be, nu):
        return (layer, be[live(i, nu)], 0, 0)

    return pl.pallas_call(
        _ffn_kernel,
        out_shape=jax.ShapeDtypeStruct(xs.shape, F32),
        grid_spec=pltpu.PrefetchScalarGridSpec(
            num_scalar_prefetch=2,
            grid=(n_blocks,),
            in_specs=[
                pl.BlockSpec((block_rows, LANES), lambda i, be, nu: (live(i, nu), 0)),
                pl.BlockSpec((1, 1, D_MODEL, 2 * D_FF), expert),
                pl.BlockSpec((1, 1, 1, 2 * D_FF), expert),
                pl.BlockSpec((1, 1, D_FF, D_MODEL), expert),
                pl.BlockSpec((1, 1, 1, D_MODEL), expert),
            ],
            out_specs=pl.BlockSpec((block_rows, LANES), lambda i, be, nu: (i, 0)),
            scratch_shapes=[pltpu.VMEM((D_MODEL, 2 * D_FF), BF16), pltpu.VMEM((D_FF, D_MODEL), BF16)],
        ),
        compiler_params=pltpu.CompilerParams(dimension_semantics=("arbitrary",),
                                             vmem_limit_bytes=_vmem_limit(56)),
        name="ffn",
    )(block_e, n_used, xs, w_gu, b_gu.reshape(depth, N_EXPERTS, 1, 2 * D_FF), w_down,
      b_down.reshape(depth, N_EXPERTS, 1, D_MODEL))


def _combine_kernel(dest_ref, x1_ref, mod_ref, gate_ref, ln2g_ref, ln2b_ref, ys_ref, o_ref, buf_ref, sem,
                    *, tile, alpha):
    t = tile
    for g in range(t // DEST_GROUP):
        def body(tt, carry, g=g):
            tok = g * DEST_GROUP + tt
            for k in range(TOP_K):
                pltpu.make_async_copy(_row(ys_ref, dest_ref[g, k, tt]), _row(buf_ref, k * t + tok),
                                      sem).start(priority=k % 2)
            return carry
        lax.fori_loop(0, DEST_GROUP, body, 0)
    pltpu.make_async_copy(ys_ref.at[pl.ds(0, TOP_K * t * ROW_TILES), :], buf_ref, sem).wait()

    gates = gate_ref[0]
    gates_t = jnp.transpose(jnp.concatenate([gates, jnp.zeros_like(gates)], axis=0))
    ff = jnp.zeros((t, D_MODEL), F32)
    for k in range(TOP_K):
        ff = ff + gates_t[:, k:k + 1] * _from_row_tiles(buf_ref, k * t, t)
    g2 = mod_ref[0][5:6, :]
    o_ref[0] = _layernorm(alpha * x1_ref[0] + (1.0 + g2) * ff, ln2g_ref[...], ln2b_ref[...])


def _combine(dest, group_offset, x1, mod, gates, ln2_g, ln2_b, ys, *, tile, alpha):
    b, s, d = x1.shape
    n_t = s // tile
    groups = tile // DEST_GROUP
    goff = group_offset // groups
    per_gate_row = gates.shape[2] // tile
    kern = functools.partial(_combine_kernel, tile=tile, alpha=alpha)
    return pl.pallas_call(
        kern,
        out_shape=jax.ShapeDtypeStruct((b, s, d), F32),
        grid=(b, n_t),
        in_specs=[
            pl.BlockSpec((groups, TOP_K, DEST_GROUP), lambda bb, ii: (goff + bb * n_t + ii, 0, 0),
                         memory_space=pltpu.SMEM),
            pl.BlockSpec((1, tile, d), lambda bb, ii: (bb, ii, 0)),
            pl.BlockSpec((1, 6, d), lambda bb, ii: (bb, 0, 0)),
            pl.BlockSpec((1, TOP_K, tile), lambda bb, ii: ((bb * n_t + ii) // per_gate_row, 0,
                                                           (bb * n_t + ii) % per_gate_row)),
            pl.BlockSpec((1, d), lambda bb, ii: (0, 0)),
            pl.BlockSpec((1, d), lambda bb, ii: (0, 0)),
            pl.BlockSpec(memory_space=pl.ANY),
        ],
        out_specs=pl.BlockSpec((1, tile, d), lambda bb, ii: (bb, ii, 0)),
        scratch_shapes=[pltpu.VMEM((TOP_K * tile * ROW_TILES, LANES), F32), pltpu.SemaphoreType.DMA],
        compiler_params=pltpu.CompilerParams(dimension_semantics=("arbitrary", "arbitrary"),
                                             vmem_limit_bytes=_vmem_limit(32)),
        name="combine",
    )(dest, x1, mod, gates, ln2_g, ln2_b, ys)


def kernel(x_prompt, x_sample, c_prompt, c_sample, cache_k, cache_v, state_conv, rel_table, ln0_g, ln0_b, w_ada, b_ada, w_in, b_in, sinks, conv_w, w_oa, w_ob, w_o, ln1_g, ln1_b, w_router, b_router, w_gu, b_gu, w_down, b_down, ln2_g, ln2_b):
    depth = w_ada.shape[0]
    bp, sp, d = x_prompt.shape
    bs, ss, _ = x_sample.shape
    alpha = (2 * depth) ** 0.25
    n_prompt, n_sample = bp * sp, bs * ss
    n_tok = n_prompt + n_sample
    assert n_prompt % ROUTE_TILE == 0 and n_sample % ROUTE_TILE == 0 and sp % PROMPT_TILE == 0
    assert ss % DEST_GROUP == 0 and ss <= WINDOW
    n_blocks = -(-(n_tok * TOP_K) // MOE_BLOCK) + N_EXPERTS

    mod_all = _ada(jnp.concatenate([c_prompt, c_sample], axis=0), w_ada, b_ada)
    mod_all = mod_all.reshape(depth, bp + bs, 6, d)
    bias = _bias_table(rel_table)
    row = lambda a: a.reshape(1, -1)
    zeros_kv = jnp.zeros((bp, WINDOW, KV_WIDTH), F32)
    zeros_u = jnp.zeros((bp, SUBLANES, CONV_WIDTH), F32)

    y_p, y_s = x_prompt, x_sample
    outs = {name: [] for name in ("kp", "vp", "up", "ks", "vs", "us")}
    for l in range(depth):
        shared = (w_in[l].astype(BF16), row(b_in[l]))
        tail = (sinks[l], conv_w[l], w_oa[l].astype(BF16), w_ob[l].astype(BF16), w_o[l].astype(BF16),
                row(ln1_g[l]), row(ln1_b[l]), w_router[l].T.astype(BF16), b_router[l].reshape(N_EXPERTS, 1))
        mod_p, mod_s = mod_all[l, :bp], mod_all[l, bp:]
        x1_p, h2_p, idx_p, gate_p, k_p, v_p, u_p = _mix(
            y_p, mod_p, row(ln0_g), row(ln0_b), *shared, bias, *tail, zeros_kv, zeros_kv, zeros_u,
            tile=PROMPT_TILE, apply_ln0=(l == 0), mask_first=True, alpha=alpha)
        u0 = jnp.pad(state_conv[l], ((0, 0), (SUBLANES - (CONV_K - 1), 0), (0, 0)))
        x1_s, h2_s, idx_s, gate_s, k_s, v_s, u_s = _mix(
            y_s, mod_s, row(ln0_g), row(ln0_b), *shared, bias, *tail,
            cache_k[l].reshape(bs, WINDOW, KV_WIDTH), cache_v[l].reshape(bs, WINDOW, KV_WIDTH), u0,
            tile=ss, apply_ln0=(l == 0), mask_first=False, alpha=alpha)

        idx_all = jnp.concatenate([jnp.transpose(idx_p, (1, 0, 2)).reshape(TOP_K, n_prompt),
                                   jnp.transpose(idx_s, (1, 0, 2)).reshape(TOP_K, n_sample)], axis=1)
        dest, cnt = _route(idx_all)
        counts = cnt[:, 0].astype(jnp.int32)
        padded = (counts + MOE_BLOCK - 1) // MOE_BLOCK * MOE_BLOCK
        pad_end = jnp.cumsum(padded)
        starts = pad_end - padded
        n_used = (pad_end[-1:] // MOE_BLOCK).astype(jnp.int32)
        block_start = jnp.arange(n_blocks, dtype=jnp.int32)[:, None] * MOE_BLOCK
        block_e = jnp.minimum(jnp.sum((pad_end[None, :] <= block_start).astype(jnp.int32), axis=1),
                              N_EXPERTS - 1)

        xs = _scatter(counts, starts, n_used, dest, h2_p, h2_s, n_blocks)
        ys = _ffn(l, block_e, n_used, xs, w_gu, b_gu, w_down, b_down, n_blocks)
        y_p = _combine(dest, 0, x1_p, mod_p, gate_p, row(ln2_g[l]), row(ln2_b[l]), ys,
                       tile=COMBINE_TILE, alpha=alpha)
        y_s = _combine(dest, n_prompt // DEST_GROUP, x1_s, mod_s, gate_s, row(ln2_g[l]), row(ln2_b[l]), ys,
                       tile=ss, alpha=alpha)

        outs["kp"].append(k_p.reshape(bp, WINDOW, N_KV_HEADS, HEAD_DIM))
        outs["vp"].append(v_p.reshape(bp, WINDOW, N_KV_HEADS, HEAD_DIM))
        outs["up"].append(u_p[:, SUBLANES - (CONV_K - 1):, :])
        outs["ks"].append(k_s.reshape(bs, WINDOW, N_KV_HEADS, HEAD_DIM))
        outs["vs"].append(v_s.reshape(bs, WINDOW, N_KV_HEADS, HEAD_DIM))
        outs["us"].append(u_s[:, SUBLANES - (CONV_K - 1):, :])
    return (y_p, y_s, jnp.stack(outs["kp"]), jnp.stack(outs["vp"]), jnp.stack(outs["up"]),
            jnp.stack(outs["ks"]), jnp.stack(outs["vs"]), jnp.stack(outs["us"]))
```

```python
import functools
import math

import jax
import jax.numpy as jnp
import numpy as np
from jax import lax
from jax.experimental import pallas as pl
from jax.experimental.pallas import tpu as pltpu

D_MODEL = 1024
CHUNK = 64
N_HEADS = 8
N_KV_HEADS = 2
HEAD_DIM = 64
GROUP = N_HEADS // N_KV_HEADS
ATT_WIDTH = N_HEADS * HEAD_DIM
KV_WIDTH = N_KV_HEADS * HEAD_DIM
WINDOW = 128
CONV_WIDTH = 512
CONV_K = 3
NUM_BUCKETS = 32
MAX_DISTANCE = 128
N_EXPERTS = 32
TOP_K = 4
D_FF = 1024
SWIGLU_LIMIT = 7.0
SWIGLU_ALPHA = 1.702
MOE_BLOCK = 256
LN_EPS = 1e-5
NEG_INF = -1e30
IN_SIZES = (ATT_WIDTH, KV_WIDTH, KV_WIDTH, CONV_WIDTH, CONV_WIDTH, CONV_WIDTH, D_MODEL, D_MODEL)
IN_WIDTH = sum(IN_SIZES)
IN_OFFS = tuple(int(s) for s in np.cumsum((0,) + IN_SIZES))

SUBLANES = 8
LANES = 128
ROW_TILES = D_MODEL // LANES
assert ROW_TILES == SUBLANES

PROMPT_TILE = 512
COMBINE_TILE = 256
ROUTE_TILE = 512
DEST_GROUP = 64

F32 = jnp.float32
BF16 = jnp.bfloat16
HIGHEST = lax.Precision.HIGHEST
NT_DIMS = (((1,), (1,)), ((), ()))


def _vmem_limit(mib):
    return mib * 1024 * 1024


def _layernorm(x, g, b):
    mu = jnp.mean(x, axis=-1, keepdims=True)
    xc = x - mu
    var = jnp.mean(xc * xc, axis=-1, keepdims=True)
    return xc * lax.rsqrt(var + LN_EPS) * g + b


def _to_row_tiles(ref, x, rows):
    for s in range(ROW_TILES):
        ref[pl.ds(s, rows, stride=ROW_TILES), :] = x[:, s * LANES:(s + 1) * LANES]


def _from_row_tiles(ref, base, rows):
    return jnp.concatenate(
        [ref[pl.ds(base * ROW_TILES + s, rows, stride=ROW_TILES), :] for s in range(ROW_TILES)], axis=-1)


def _ada_kernel(c_ref, w_ref, b_ref, o_ref):
    c = c_ref[...]
    s = c * jax.nn.sigmoid(c)
    o_ref[0] = jnp.dot(s, w_ref[0], precision=HIGHEST, preferred_element_type=F32) + b_ref[0]


def _ada(c_all, w_ada, b_ada):
    depth = w_ada.shape[0]
    nb = c_all.shape[0]
    n_col = 6 * D_MODEL // D_MODEL
    return pl.pallas_call(
        _ada_kernel,
        out_shape=jax.ShapeDtypeStruct((depth, nb, 6 * D_MODEL), F32),
        grid=(depth, n_col),
        in_specs=[
            pl.BlockSpec((nb, D_MODEL), lambda l, j: (0, 0)),
            pl.BlockSpec((1, D_MODEL, D_MODEL), lambda l, j: (l, 0, j)),
            pl.BlockSpec((1, 1, D_MODEL), lambda l, j: (l, 0, j)),
        ],
        out_specs=pl.BlockSpec((1, nb, D_MODEL), lambda l, j: (l, 0, j)),
        compiler_params=pltpu.CompilerParams(dimension_semantics=("arbitrary", "arbitrary"),
                                             vmem_limit_bytes=_vmem_limit(32)),
        name="ada",
    )(c_all, w_ada, b_ada.reshape(depth, 1, 6 * D_MODEL))


def _rel_bucket(rel):
    half = NUM_BUCKETS // 2
    max_exact = half // 2
    n = jnp.abs(rel)
    n_f = jnp.maximum(n, 1).astype(jnp.float32)
    large = max_exact + (jnp.log(n_f / max_exact) / math.log(MAX_DISTANCE / max_exact)
                         * (half - max_exact)).astype(jnp.int32)
    large = jnp.minimum(large, half - 1)
    return jnp.where(rel > 0, half, 0) + jnp.where(n < max_exact, n, large)


BAND = WINDOW + CHUNK
PAIR_ROWS = 2 * CHUNK
PAIR_COLS = 2 * BAND


def _band_codes():
    r = jnp.arange(PAIR_ROWS)[:, None]
    j = jnp.arange(PAIR_COLS)[None, :]
    bucket = _rel_bucket(j % BAND - WINDOW - r % CHUNK)
    head = 2 * (r // CHUNK) + j // BAND
    return (bucket + NUM_BUCKETS * head).astype(jnp.int32)


def _bias_kernel(table_ref, code_ref, o_ref):
    g = pl.program_id(0)
    code = code_ref[...]
    acc = jnp.zeros(code.shape, F32)
    for hq in range(GROUP):
        for i in range(NUM_BUCKETS):
            acc = jnp.where(code == hq * NUM_BUCKETS + i, table_ref[i, g * GROUP + hq], acc)
    o_ref[0] = acc


def _bias_table(rel_table):
    return pl.pallas_call(
        _bias_kernel,
        out_shape=jax.ShapeDtypeStruct((N_KV_HEADS, PAIR_ROWS, PAIR_COLS), F32),
        grid=(N_KV_HEADS,),
        in_specs=[
            pl.BlockSpec(memory_space=pltpu.SMEM),
            pl.BlockSpec((PAIR_ROWS, PAIR_COLS), lambda g: (0, 0)),
        ],
        out_specs=pl.BlockSpec((1, PAIR_ROWS, PAIR_COLS), lambda g: (g, 0, 0)),
        compiler_params=pltpu.CompilerParams(dimension_semantics=("arbitrary",)),
        name="rel_bias",
    )(rel_table, _band_codes())


def _mix_kernel(x_ref, mod_ref, ln0g_ref, ln0b_ref, win_ref, bin_ref, bias_ref, sink_ref, convw_ref,
                woa_ref, wob_ref, wo_ref, ln1g_ref, ln1b_ref, wr_ref, br_ref, k0_ref, v0_ref, u0_ref,
                x1_ref, h2_ref, idx_ref, gate_ref, newk_ref, newv_ref, newu_ref,
                kc_ref, vc_ref, ubuf_ref, *, tile, apply_ln0, mask_first, alpha):
    i = pl.program_id(1)
    t = tile

    @pl.when(i == 0)
    def _():
        kc_ref[...] = k0_ref[0]
        vc_ref[...] = v0_ref[0]
        ubuf_ref[0:SUBLANES, :] = u0_ref[0]

    x = x_ref[0]
    if apply_ln0:
        x = _layernorm(x, ln0g_ref[...], ln0b_ref[...])
    mod = mod_ref[0]
    sh1, sc1, g1, sh2, sc2, g2 = [mod[j:j + 1, :] for j in range(6)]
    h = (x * (1.0 + sc1) + sh1).astype(BF16)

    def proj(j0, j1):
        lo, hi = IN_OFFS[j0], IN_OFFS[j1]
        return jnp.dot(h, win_ref[:, lo:hi], preferred_element_type=F32) + bin_ref[:, lo:hi]

    q = proj(0, 1)
    kv = proj(1, 3)
    kfull = jnp.concatenate([kc_ref[...], kv[:, :KV_WIDTH]], axis=0)
    vfull = jnp.concatenate([vc_ref[...], kv[:, KV_WIDTH:]], axis=0)
    kc_ref[...] = kfull[t:, :]
    vc_ref[...] = vfull[t:, :]
    newk_ref[0] = kfull[t:, :]
    newv_ref[0] = vfull[t:, :]
    low = lax.broadcasted_iota(jnp.int32, kfull.shape, 1) < HEAD_DIM
    k_sw = pltpu.roll(kfull, HEAD_DIM, axis=1)
    v_sw = pltpu.roll(vfull, HEAD_DIM, axis=1)
    k_even = [jnp.where(low, kfull, 0.0).astype(BF16), jnp.where(low, k_sw, 0.0).astype(BF16)]
    k_odd = [jnp.where(low, 0.0, k_sw).astype(BF16), jnp.where(low, 0.0, kfull).astype(BF16)]
    v_even = [jnp.where(low, vfull, 0.0).astype(BF16), jnp.where(low, v_sw, 0.0).astype(BF16)]
    v_odd = [jnp.where(low, 0.0, v_sw).astype(BF16), jnp.where(low, 0.0, vfull).astype(BF16)]
    col = lax.broadcasted_iota(jnp.int32, (PAIR_ROWS, PAIR_COLS), 1)
    even = col < BAND
    band_col = jnp.where(even, col, col - BAND)
    first_pair = lax.broadcasted_iota(jnp.int32, (PAIR_ROWS, 1), 0) < CHUNK
    out_low = lax.broadcasted_iota(jnp.int32, (PAIR_ROWS, LANES), 1) < HEAD_DIM
    sink_even = [jnp.where(first_pair, sink_ref[g * GROUP], sink_ref[g * GROUP + 2]) for g in range(N_KV_HEADS)]
    sink_odd = [jnp.where(first_pair, sink_ref[g * GROUP + 1], sink_ref[g * GROUP + 3])
                for g in range(N_KV_HEADS)]
    units = [(c, g) for c in range(t // CHUNK) for g in range(N_KV_HEADS)]
    scores = []
    for c, g in units:
        r0 = c * CHUNK
        ql = jnp.concatenate([q[r0:r0 + CHUNK, (2 * g) * LANES:(2 * g + 1) * LANES],
                              q[r0:r0 + CHUNK, (2 * g + 1) * LANES:(2 * g + 2) * LANES]], axis=0)
        kp = jnp.concatenate([k_even[g][r0:r0 + BAND], k_odd[g][r0:r0 + BAND]], axis=0)
        s = lax.dot_general(ql.astype(BF16), kp, NT_DIMS, preferred_element_type=F32)
        s = s * (HEAD_DIM ** -0.5) + bias_ref[g]
        if mask_first and c < WINDOW // CHUNK:
            s = jnp.where(band_col < jnp.where(i == 0, WINDOW - r0, 0), NEG_INF, s)
        scores.append(s)
    maxes = []
    for (c, g), s in zip(units, scores):
        m_e = jnp.maximum(jnp.max(jnp.where(even, s, -jnp.inf), axis=-1, keepdims=True), sink_even[g])
        m_o = jnp.maximum(jnp.max(jnp.where(even, -jnp.inf, s), axis=-1, keepdims=True), sink_odd[g])
        maxes.append((m_e, m_o))
    exps, scales = [], []
    for (c, g), s, (m_e, m_o) in zip(units, scores, maxes):
        e = jnp.exp(s - jnp.where(even, m_e, m_o))
        d_e = jnp.sum(jnp.where(even, e, 0.0), axis=-1, keepdims=True) + jnp.exp(sink_even[g] - m_e)
        d_o = jnp.sum(jnp.where(even, 0.0, e), axis=-1, keepdims=True) + jnp.exp(sink_odd[g] - m_o)
        exps.append(e.astype(BF16))
        scales.append(jnp.where(out_low, 1.0 / d_e, 1.0 / d_o))
    outs = {}
    for (c, g), e, scale in zip(units, exps, scales):
        r0 = c * CHUNK
        vp = jnp.concatenate([v_even[g][r0:r0 + BAND], v_odd[g][r0:r0 + BAND]], axis=0)
        outs[c, g] = jnp.dot(e, vp, preferred_element_type=F32) * scale
    ya = jnp.concatenate(
        [jnp.concatenate([outs[c, g][half * CHUNK:(half + 1) * CHUNK] for g in range(N_KV_HEADS)
                          for half in range(2)], axis=-1) for c in range(t // CHUNK)], axis=0)

    cb = proj(3, 4)
    u = proj(4, 5) * proj(5, 6)
    ubuf_ref[SUBLANES:t + SUBLANES, :] = u
    cw = convw_ref[...]
    yc = (cw[0:1, :] * ubuf_ref[SUBLANES - 2:t + SUBLANES - 2, :]
          + cw[1:2, :] * ubuf_ref[SUBLANES - 1:t + SUBLANES - 1, :] + cw[2:3, :] * u)
    yb = cb * yc
    tail = ubuf_ref[t:t + SUBLANES, :]
    newu_ref[0] = tail
    ubuf_ref[0:SUBLANES, :] = tail

    a_out = jnp.dot(ya.astype(BF16), woa_ref[...], preferred_element_type=F32)
    b_out = jnp.dot(yb.astype(BF16), wob_ref[...], preferred_element_type=F32)
    mixin = jax.nn.sigmoid(proj(6, 7)) * a_out + jax.nn.sigmoid(proj(7, 8)) * b_out
    mix = jnp.dot(mixin.astype(BF16), wo_ref[...], preferred_element_type=F32)
    x1 = _layernorm(alpha * x + (1.0 + g1) * mix, ln1g_ref[...], ln1b_ref[...])
    x1_ref[0] = x1

    h2 = x1 * (1.0 + sc2) + sh2
    _to_row_tiles(h2_ref, h2, t)
    logits = lax.dot_general(wr_ref[...], h2.astype(BF16), NT_DIMS, preferred_element_type=F32) + br_ref[...]
    eid = lax.broadcasted_iota(jnp.int32, (N_EXPERTS, t), 0)
    vals, ids = [], []
    for _ in range(TOP_K):
        mx = jnp.max(logits, axis=0, keepdims=True)
        sel = jnp.min(jnp.where(logits == mx, eid, N_EXPERTS), axis=0, keepdims=True)
        vals.append(mx)
        ids.append(sel)
        logits = jnp.where(eid == sel, -jnp.inf, logits)
    ex = [jnp.exp(v - vals[0]) for v in vals]
    tot = ex[0] + ex[1] + ex[2] + ex[3]
    idx_ref[0] = jnp.concatenate(ids, axis=0)
    gate_ref[0] = jnp.concatenate([e_ / tot for e_ in ex], axis=0)


def _mix(x, mod, ln0_g, ln0_b, w_in, b_in, bias, sink, conv_w, w_oa, w_ob, w_o, ln1_g, ln1_b,
         w_rt, b_r, k0, v0, u0, *, tile, apply_ln0, mask_first, alpha):
    b, s, d = x.shape
    n_t = s // tile
    const = lambda shape: pl.BlockSpec(shape, lambda bb, ii: (0,) * len(shape), pipeline_mode=pl.Buffered(1))
    per_b = lambda shape: pl.BlockSpec((1,) + shape, lambda bb, ii: (bb,) + (0,) * len(shape))
    kern = functools.partial(_mix_kernel, tile=tile, apply_ln0=apply_ln0, mask_first=mask_first, alpha=alpha)
    return pl.pallas_call(
        kern,
        out_shape=(
            jax.ShapeDtypeStruct((b, s, d), F32),
            jax.ShapeDtypeStruct((b * s * ROW_TILES, LANES), F32),
            jax.ShapeDtypeStruct((b * n_t, TOP_K, tile), jnp.int32),
            jax.ShapeDtypeStruct((b * n_t, TOP_K, tile), F32),
            jax.ShapeDtypeStruct((b, WINDOW, KV_WIDTH), F32),
            jax.ShapeDtypeStruct((b, WINDOW, KV_WIDTH), F32),
            jax.ShapeDtypeStruct((b, SUBLANES, CONV_WIDTH), F32),
        ),
        grid=(b, n_t),
        in_specs=[
            pl.BlockSpec((1, tile, d), lambda bb, ii: (bb, ii, 0)),
            per_b((6, d)),
            const((1, d)), const((1, d)),
            const((d, IN_WIDTH)), const((1, IN_WIDTH)),
            const((N_KV_HEADS, PAIR_ROWS, PAIR_COLS)),
            pl.BlockSpec(memory_space=pltpu.SMEM),
            const((CONV_K, CONV_WIDTH)),
            const((ATT_WIDTH, d)), const((CONV_WIDTH, d)), const((d, d)),
            const((1, d)), const((1, d)),
            const((N_EXPERTS, d)), const((N_EXPERTS, 1)),
            per_b((WINDOW, KV_WIDTH)), per_b((WINDOW, KV_WIDTH)), per_b((SUBLANES, CONV_WIDTH)),
        ],
        out_specs=(
            pl.BlockSpec((1, tile, d), lambda bb, ii: (bb, ii, 0)),
            pl.BlockSpec((tile * ROW_TILES, LANES), lambda bb, ii: (bb * n_t + ii, 0)),
            pl.BlockSpec((1, TOP_K, tile), lambda bb, ii: (bb * n_t + ii, 0, 0)),
            pl.BlockSpec((1, TOP_K, tile), lambda bb, ii: (bb * n_t + ii, 0, 0)),
            per_b((WINDOW, KV_WIDTH)), per_b((WINDOW, KV_WIDTH)), per_b((SUBLANES, CONV_WIDTH)),
        ),
        scratch_shapes=[
            pltpu.VMEM((WINDOW, KV_WIDTH), F32),
            pltpu.VMEM((WINDOW, KV_WIDTH), F32),
            pltpu.VMEM((tile + SUBLANES, CONV_WIDTH), F32),
        ],
        compiler_params=pltpu.CompilerParams(dimension_semantics=("arbitrary", "arbitrary"),
                                             vmem_limit_bytes=_vmem_limit(56)),
        name="mix",
    )(x, mod, ln0_g, ln0_b, w_in, b_in, bias, sink, conv_w, w_oa, w_ob, w_o, ln1_g, ln1_b, w_rt, b_r,
      k0, v0, u0)


def _route_kernel(idx_ref, dest_ref, cnt_ref, run_ref, start_ref):
    phase = pl.program_id(0)
    j = pl.program_id(1)
    t = ROUTE_TILE
    ids = idx_ref[...]
    eid = lax.broadcasted_iota(jnp.int32, (N_EXPERTS, t), 0)
    hits = [(eid == ids[k:k + 1, :]).astype(F32) for k in range(TOP_K)]
    hit = hits[0] + hits[1] + hits[2] + hits[3]

    @pl.when(jnp.logical_and(phase == 0, j == 0))
    def _():
        run_ref[...] = jnp.zeros_like(run_ref)

    @pl.when(phase == 0)
    def _():
        run_ref[...] += jnp.sum(hit, axis=1, keepdims=True)

    @pl.when(jnp.logical_and(phase == 1, j == 0))
    def _():
        cnt = run_ref[...]
        cnt_ref[...] = jnp.broadcast_to(cnt, cnt_ref.shape)
        padded = jnp.floor((cnt + (MOE_BLOCK - 1)) * (1.0 / MOE_BLOCK)) * MOE_BLOCK
        r = lax.broadcasted_iota(jnp.int32, (N_EXPERTS, N_EXPERTS), 0)
        c = lax.broadcasted_iota(jnp.int32, (N_EXPERTS, N_EXPERTS), 1)
        before = (c < r).astype(F32)
        start_ref[...] = jnp.dot(before, jnp.broadcast_to(padded, start_ref.shape), precision=HIGHEST,
                                 preferred_element_type=F32)
        run_ref[...] = jnp.zeros_like(run_ref)

    @pl.when(phase == 1)
    def _():
        r = lax.broadcasted_iota(jnp.int32, (t, t), 0)
        c = lax.broadcasted_iota(jnp.int32, (t, t), 1)
        upto = (r <= c).astype(BF16)
        incl = jnp.dot(hit.astype(BF16), upto, preferred_element_type=F32)
        pos = incl - hit + (start_ref[:, 0:1] + run_ref[...])
        for k in range(TOP_K):
            d = jnp.sum(hits[k] * pos, axis=0, keepdims=True).astype(jnp.int32)
            for gidx in range(t // DEST_GROUP):
                dest_ref[gidx, k:k + 1, :] = d[:, gidx * DEST_GROUP:(gidx + 1) * DEST_GROUP]
        run_ref[...] += jnp.sum(hit, axis=1, keepdims=True)


def _route(idx_all):
    n = idx_all.shape[1]
    n_t = n // ROUTE_TILE
    groups = ROUTE_TILE // DEST_GROUP
    return pl.pallas_call(
        _route_kernel,
        out_shape=(
            jax.ShapeDtypeStruct((n // DEST_GROUP, TOP_K, DEST_GROUP), jnp.int32),
            jax.ShapeDtypeStruct((N_EXPERTS, LANES), F32),
        ),
        grid=(2, n_t),
        in_specs=[pl.BlockSpec((TOP_K, ROUTE_TILE), lambda p, j: (0, j))],
        out_specs=(
            pl.BlockSpec((groups, TOP_K, DEST_GROUP), lambda p, j: (p * j, 0, 0)),
            pl.BlockSpec((N_EXPERTS, LANES), lambda p, j: (0, 0)),
        ),
        scratch_shapes=[pltpu.VMEM((N_EXPERTS, 1), F32), pltpu.VMEM((N_EXPERTS, LANES), F32)],
        compiler_params=pltpu.CompilerParams(dimension_semantics=("arbitrary", "arbitrary")),
        name="route",
    )(idx_all)


def _row(ref, r):
    return ref.at[pl.ds(pl.multiple_of(r * ROW_TILES, ROW_TILES), ROW_TILES), :]


def _scatter_kernel(cnt_ref, start_ref, nused_ref, dest_ref, hp_ref, hs_ref, xs_ref, zero_ref, sem, zsem,
                    *, n_prompt_tiles, n_blocks):
    j = pl.program_id(0)
    block_rows = MOE_BLOCK * ROW_TILES

    def zero_block_copy(blk):
        return pltpu.make_async_copy(
            zero_ref, xs_ref.at[pl.ds(pl.multiple_of(blk * block_rows, block_rows), block_rows), :], zsem)

    def zero_row_copy(r):
        return pltpu.make_async_copy(zero_ref.at[pl.ds(0, ROW_TILES), :], _row(xs_ref, r), zsem)

    @pl.when(j == 0)
    def _():
        zero_ref[...] = jnp.zeros_like(zero_ref)
        n_used = nused_ref[0]

        def per_expert(e, carry):
            lo = start_ref[e] + cnt_ref[e]
            hi = start_ref[e] + ((cnt_ref[e] + (MOE_BLOCK - 1)) // MOE_BLOCK) * MOE_BLOCK
            lax.fori_loop(lo, hi, lambda r, c: (zero_row_copy(r).start(), c)[1], 0)
            lax.fori_loop(lo, hi, lambda r, c: (zero_row_copy(r).wait(), c)[1], 0)
            return carry

        lax.fori_loop(0, N_EXPERTS, per_expert, 0)
        lax.fori_loop(n_used, n_blocks, lambda b_, c: (zero_block_copy(b_).start(), c)[1], 0)
        lax.fori_loop(n_used, n_blocks, lambda b_, c: (zero_block_copy(b_).wait(), c)[1], 0)

    def issue(src_ref):
        for g in range(ROUTE_TILE // DEST_GROUP):
            def body(tt, carry, g=g):
                tok = g * DEST_GROUP + tt
                for k in range(TOP_K):
                    pltpu.make_async_copy(_row(src_ref, tok), _row(xs_ref, dest_ref[g, k, tt]),
                                          sem).start(priority=k % 2)
                return carry
            lax.fori_loop(0, DEST_GROUP, body, 0)

    @pl.when(j < n_prompt_tiles)
    def _():
        issue(hp_ref)

    @pl.when(j >= n_prompt_tiles)
    def _():
        issue(hs_ref)

    for _ in range(TOP_K):
        pltpu.make_async_copy(hp_ref, xs_ref.at[pl.ds(0, ROUTE_TILE * ROW_TILES), :], sem).wait()


def _scatter(counts, starts, n_used, dest, h_prompt, h_sample, n_blocks):
    n_prompt_tiles = h_prompt.shape[0] // (ROUTE_TILE * ROW_TILES)
    n_sample_tiles = h_sample.shape[0] // (ROUTE_TILE * ROW_TILES)
    groups = ROUTE_TILE // DEST_GROUP
    kern = functools.partial(_scatter_kernel, n_prompt_tiles=n_prompt_tiles, n_blocks=n_blocks)
    return pl.pallas_call(
        kern,
        out_shape=jax.ShapeDtypeStruct((n_blocks * MOE_BLOCK * ROW_TILES, LANES), F32),
        grid_spec=pltpu.PrefetchScalarGridSpec(
            num_scalar_prefetch=3,
            grid=(n_prompt_tiles + n_sample_tiles,),
            in_specs=[
                pl.BlockSpec((groups, TOP_K, DEST_GROUP), lambda j, *_: (j, 0, 0), memory_space=pltpu.SMEM),
                pl.BlockSpec((ROUTE_TILE * ROW_TILES, LANES),
                             lambda j, *_: (jnp.minimum(j, n_prompt_tiles - 1), 0)),
                pl.BlockSpec((ROUTE_TILE * ROW_TILES, LANES),
                             lambda j, *_: (jnp.maximum(j - n_prompt_tiles, 0), 0)),
            ],
            out_specs=pl.BlockSpec(memory_space=pl.ANY),
            scratch_shapes=[
                pltpu.VMEM((MOE_BLOCK * ROW_TILES, LANES), F32),
                pltpu.SemaphoreType.DMA,
                pltpu.SemaphoreType.DMA,
            ],
        ),
        compiler_params=pltpu.CompilerParams(dimension_semantics=("arbitrary",)),
        name="scatter",
    )(counts, starts, n_used, dest, h_prompt, h_sample)


def _ffn_kernel(first_ref, nblk_ref, nused_ref, xs_ref, wgu_ref, bgu_ref, wd_ref, bd_ref, ys_ref,
                wgu_bf, wd_bf, xbuf, ybuf, in_sem, out_sem, *, n_blocks):
    e = pl.program_id(0)
    first = first_ref[e]
    n_used = nused_ref[0]
    block_rows = MOE_BLOCK * ROW_TILES

    def blk(ref, b):
        return ref.at[pl.ds(pl.multiple_of(b * block_rows, block_rows), block_rows), :]

    def in_copy(b, slot):
        return pltpu.make_async_copy(blk(xs_ref, b), xbuf.at[slot], in_sem.at[slot])

    def out_copy(b, slot):
        return pltpu.make_async_copy(ybuf.at[slot], blk(ys_ref, b), out_sem.at[slot])

    @pl.when(e == 0)
    def _():
        in_copy(0, 0).start()

    @pl.when(nblk_ref[e] > 0)
    def _():
        wgu_bf[...] = wgu_ref[0, 0].astype(BF16)
        wd_bf[...] = wd_ref[0, 0].astype(BF16)

    def body(j, carry):
        b = first + j
        slot = lax.rem(b, 2)
        in_copy(b, slot).wait()

        @pl.when(b + 1 < n_used)
        def _():
            in_copy(b + 1, 1 - slot).start()

        @pl.when(b >= 2)
        def _():
            out_copy(b - 2, slot).wait()

        x = _from_row_tiles(xbuf.at[slot], 0, MOE_BLOCK).astype(BF16)
        gu = jnp.dot(x, wgu_bf[...], preferred_element_type=F32) + bgu_ref[0, 0]
        g = jnp.minimum(gu[:, :D_FF], SWIGLU_LIMIT)
        lin = jnp.clip(gu[:, D_FF:], -SWIGLU_LIMIT, SWIGLU_LIMIT)
        a = g * jax.nn.sigmoid(SWIGLU_ALPHA * g) * (lin + 1.0)
        y = jnp.dot(a.astype(BF16), wd_bf[...], preferred_element_type=F32) + bd_ref[0, 0]
        _to_row_tiles(ybuf.at[slot], y, MOE_BLOCK)
        out_copy(b, slot).start()
        return carry

    lax.fori_loop(0, nblk_ref[e], body, 0)

    @pl.when(e == N_EXPERTS - 1)
    def _():
        @pl.when(n_used >= 2)
        def _():
            out_copy(n_used - 2, lax.rem(n_used, 2)).wait()

        out_copy(n_used - 1, lax.rem(n_used - 1, 2)).wait()
        ybuf[0] = jnp.zeros(ybuf.shape[1:], F32)
        lax.fori_loop(n_used, n_blocks, lambda b, c: (out_copy(b, 0).start(), c)[1], 0)
        lax.fori_loop(n_used, n_blocks, lambda b, c: (out_copy(b, 0).wait(), c)[1], 0)


def _ffn(layer, first_blk, n_blk, n_used, xs, w_gu, b_gu, w_down, b_down, n_blocks):
    block_rows = MOE_BLOCK * ROW_TILES
    depth = w_gu.shape[0]

    def expert(e, *_):
        return (layer, e, 0, 0)

    return pl.pallas_call(
        functools.partial(_ffn_kernel, n_blocks=n_blocks),
        out_shape=jax.ShapeDtypeStruct(xs.shape, F32),
        grid_spec=pltpu.PrefetchScalarGridSpec(
            num_scalar_prefetch=3,
            grid=(N_EXPERTS,),
            in_specs=[
                pl.BlockSpec(memory_space=pl.ANY),
                pl.BlockSpec((1, 1, D_MODEL, 2 * D_FF), expert),
                pl.BlockSpec((1, 1, 1, 2 * D_FF), expert),
                pl.BlockSpec((1, 1, D_FF, D_MODEL), expert),
                pl.BlockSpec((1, 1, 1, D_MODEL), expert),
            ],
            out_specs=pl.BlockSpec(memory_space=pl.ANY),
            scratch_shapes=[
                pltpu.VMEM((D_MODEL, 2 * D_FF), BF16),
                pltpu.VMEM((D_FF, D_MODEL), BF16),
                pltpu.VMEM((2, block_rows, LANES), F32),
                pltpu.VMEM((2, block_rows, LANES), F32),
                pltpu.SemaphoreType.DMA((2,)),
                pltpu.SemaphoreType.DMA((2,)),
            ],
        ),
        compiler_params=pltpu.CompilerParams(dimension_semantics=("arbitrary",),
                                             vmem_limit_bytes=_vmem_limit(56)),
        name="ffn",
    )(first_blk, n_blk, n_used, xs, w_gu, b_gu.reshape(depth, N_EXPERTS, 1, 2 * D_FF), w_down,
      b_down.reshape(depth, N_EXPERTS, 1, D_MODEL))


def _combine_kernel(dest_ref, next_ref, x1_ref, mod_ref, gate_ref, ln2g_ref, ln2b_ref, ys_ref, o_ref,
                    buf0, buf1, sem0, sem1, *, tile, alpha, n_steps):
    t = tile
    step = pl.program_id(0) * pl.num_programs(1) + pl.program_id(1)

    def fetch(d_ref, g, tt, buf, sem):
        tok = g * DEST_GROUP + tt
        for k in range(TOP_K):
            pltpu.make_async_copy(_row(ys_ref, d_ref[g, k, tt]), _row(buf, k * t + tok),
                                  sem).start(priority=k % 2)

    def wait(buf, sem):
        pltpu.make_async_copy(ys_ref.at[pl.ds(0, TOP_K * t * ROW_TILES), :], buf, sem).wait()

    @pl.when(step == 0)
    def _():
        for g in range(t // DEST_GROUP):
            lax.fori_loop(0, DEST_GROUP, lambda tt, c, g=g: (fetch(dest_ref, g, tt, buf0, sem0), c)[1], 0)

    def phase(buf, sem, nxt_buf, nxt_sem):
        wait(buf, sem)
        for g in range(t // DEST_GROUP):
            for tt in range(DEST_GROUP):
                fetch(next_ref, g, tt, nxt_buf, nxt_sem)
        gates = gate_ref[0]
        gates_t = jnp.transpose(jnp.concatenate([gates, jnp.zeros_like(gates)], axis=0))
        ff = jnp.zeros((t, D_MODEL), F32)
        for k in range(TOP_K):
            ff = ff + gates_t[:, k:k + 1] * _from_row_tiles(buf, k * t, t)
        g2 = mod_ref[0][5:6, :]
        o_ref[0] = _layernorm(alpha * x1_ref[0] + (1.0 + g2) * ff, ln2g_ref[...], ln2b_ref[...])

    @pl.when(step % 2 == 0)
    def _():
        phase(buf0, sem0, buf1, sem1)

    @pl.when(step % 2 == 1)
    def _():
        phase(buf1, sem1, buf0, sem0)

    @pl.when(step == n_steps - 1)
    def _():
        if (n_steps - 1) % 2 == 0:
            wait(buf1, sem1)
        else:
            wait(buf0, sem0)


def _combine(dest, group_offset, x1, mod, gates, ln2_g, ln2_b, ys, *, tile, alpha):
    b, s, d = x1.shape
    n_t = s // tile
    n_steps = b * n_t
    groups = tile // DEST_GROUP
    goff = group_offset // groups
    per_gate_row = gates.shape[2] // tile
    kern = functools.partial(_combine_kernel, tile=tile, alpha=alpha, n_steps=n_steps)
    buf = pltpu.VMEM((TOP_K * tile * ROW_TILES, LANES), F32)
    return pl.pallas_call(
        kern,
        out_shape=jax.ShapeDtypeStruct((b, s, d), F32),
        grid=(b, n_t),
        in_specs=[
            pl.BlockSpec((groups, TOP_K, DEST_GROUP), lambda bb, ii: (goff + bb * n_t + ii, 0, 0),
                         memory_space=pltpu.SMEM),
            pl.BlockSpec((groups, TOP_K, DEST_GROUP),
                         lambda bb, ii: (goff + jnp.minimum(bb * n_t + ii + 1, n_steps - 1), 0, 0),
                         memory_space=pltpu.SMEM),
            pl.BlockSpec((1, tile, d), lambda bb, ii: (bb, ii, 0)),
            pl.BlockSpec((1, 6, d), lambda bb, ii: (bb, 0, 0)),
            pl.BlockSpec((1, TOP_K, tile), lambda bb, ii: ((bb * n_t + ii) // per_gate_row, 0,
                                                           (bb * n_t + ii) % per_gate_row)),
            pl.BlockSpec((1, d), lambda bb, ii: (0, 0)),
            pl.BlockSpec((1, d), lambda bb, ii: (0, 0)),
            pl.BlockSpec(memory_space=pl.ANY),
        ],
        out_specs=pl.BlockSpec((1, tile, d), lambda bb, ii: (bb, ii, 0)),
        scratch_shapes=[buf, buf, pltpu.SemaphoreType.DMA, pltpu.SemaphoreType.DMA],
        compiler_params=pltpu.CompilerParams(dimension_semantics=("arbitrary", "arbitrary"),
                                             vmem_limit_bytes=_vmem_limit(32)),
        name="combine",
    )(dest, dest, x1, mod, gates, ln2_g, ln2_b, ys)


def kernel(x_prompt, x_sample, c_prompt, c_sample, cache_k, cache_v, state_conv, rel_table, ln0_g, ln0_b, w_ada, b_ada, w_in, b_in, sinks, conv_w, w_oa, w_ob, w_o, ln1_g, ln1_b, w_router, b_router, w_gu, b_gu, w_down, b_down, ln2_g, ln2_b):
    depth = w_ada.shape[0]
    bp, sp, d = x_prompt.shape
    bs, ss, _ = x_sample.shape
    alpha = (2 * depth) ** 0.25
    n_prompt, n_sample = bp * sp, bs * ss
    n_tok = n_prompt + n_sample
    assert n_prompt % ROUTE_TILE == 0 and n_sample % ROUTE_TILE == 0 and sp % PROMPT_TILE == 0
    assert ss % DEST_GROUP == 0 and ss <= WINDOW
    n_blocks = -(-(n_tok * TOP_K) // MOE_BLOCK) + N_EXPERTS

    mod_all = _ada(jnp.concatenate([c_prompt, c_sample], axis=0), w_ada, b_ada)
    mod_all = mod_all.reshape(depth, bp + bs, 6, d)
    bias = _bias_table(rel_table)
    row = lambda a: a.reshape(1, -1)
    zeros_kv = jnp.zeros((bp, WINDOW, KV_WIDTH), F32)
    zeros_u = jnp.zeros((bp, SUBLANES, CONV_WIDTH), F32)

    y_p, y_s = x_prompt, x_sample
    outs = {name: [] for name in ("kp", "vp", "up", "ks", "vs", "us")}
    for l in range(depth):
        shared = (w_in[l].astype(BF16), row(b_in[l]))
        tail = (sinks[l], conv_w[l], w_oa[l].astype(BF16), w_ob[l].astype(BF16), w_o[l].astype(BF16),
                row(ln1_g[l]), row(ln1_b[l]), w_router[l].T.astype(BF16), b_router[l].reshape(N_EXPERTS, 1))
        mod_p, mod_s = mod_all[l, :bp], mod_all[l, bp:]
        x1_p, h2_p, idx_p, gate_p, k_p, v_p, u_p = _mix(
            y_p, mod_p, row(ln0_g), row(ln0_b), *shared, bias, *tail, zeros_kv, zeros_kv, zeros_u,
            tile=PROMPT_TILE, apply_ln0=(l == 0), mask_first=True, alpha=alpha)
        u0 = jnp.pad(state_conv[l], ((0, 0), (SUBLANES - (CONV_K - 1), 0), (0, 0)))
        x1_s, h2_s, idx_s, gate_s, k_s, v_s, u_s = _mix(
            y_s, mod_s, row(ln0_g), row(ln0_b), *shared, bias, *tail,
            cache_k[l].reshape(bs, WINDOW, KV_WIDTH), cache_v[l].reshape(bs, WINDOW, KV_WIDTH), u0,
            tile=ss, apply_ln0=(l == 0), mask_first=False, alpha=alpha)

        idx_all = jnp.concatenate([jnp.transpose(idx_p, (1, 0, 2)).reshape(TOP_K, n_prompt),
                                   jnp.transpose(idx_s, (1, 0, 2)).reshape(TOP_K, n_sample)], axis=1)
        dest, cnt = _route(idx_all)
        counts = cnt[:, 0].astype(jnp.int32)
        padded = (counts + MOE_BLOCK - 1) // MOE_BLOCK * MOE_BLOCK
        pad_end = jnp.cumsum(padded)
        starts = pad_end - padded
        n_used = (pad_end[-1:] // MOE_BLOCK).astype(jnp.int32)
        xs = _scatter(counts, starts, n_used, dest, h2_p, h2_s, n_blocks)
        ys = _ffn(l, starts // MOE_BLOCK, padded // MOE_BLOCK, n_used, xs, w_gu, b_gu, w_down, b_down,
                  n_blocks)
        y_p = _combine(dest, 0, x1_p, mod_p, gate_p, row(ln2_g[l]), row(ln2_b[l]), ys,
                       tile=COMBINE_TILE, alpha=alpha)
        y_s = _combine(dest, n_prompt // DEST_GROUP, x1_s, mod_s, gate_s, row(ln2_g[l]), row(ln2_b[l]), ys,
                       tile=ss, alpha=alpha)

        outs["kp"].append(k_p.reshape(bp, WINDOW, N_KV_HEADS, HEAD_DIM))
        outs["vp"].append(v_p.reshape(bp, WINDOW, N_KV_HEADS, HEAD_DIM))
        outs["up"].append(u_p[:, SUBLANES - (CONV_K - 1):, :])
        outs["ks"].append(k_s.reshape(bs, WINDOW, N_KV_HEADS, HEAD_DIM))
        outs["vs"].append(v_s.reshape(bs, WINDOW, N_KV_HEADS, HEAD_DIM))
        outs["us"].append(u_s[:, SUBLANES - (CONV_K - 1):, :])
    return (y_p, y_s, jnp.stack(outs["kp"]), jnp.stack(outs["vp"]), jnp.stack(outs["up"]),
            jnp.stack(outs["ks"]), jnp.stack(outs["vs"]), jnp.stack(outs["us"]))
```

```python
import functools
import math

import jax
import jax.numpy as jnp
import numpy as np
from jax import lax
from jax.experimental import pallas as pl
from jax.experimental.pallas import tpu as pltpu

D_MODEL = 1024
CHUNK = 64
N_HEADS = 8
N_KV_HEADS = 2
HEAD_DIM = 64
GROUP = N_HEADS // N_KV_HEADS
ATT_WIDTH = N_HEADS * HEAD_DIM
KV_WIDTH = N_KV_HEADS * HEAD_DIM
WINDOW = 128
CONV_WIDTH = 512
CONV_K = 3
NUM_BUCKETS = 32
MAX_DISTANCE = 128
N_EXPERTS = 32
TOP_K = 4
D_FF = 1024
SWIGLU_LIMIT = 7.0
SWIGLU_ALPHA = 1.702
MOE_BLOCK = 256
LN_EPS = 1e-5
NEG_INF = -1e30
IN_SIZES = (ATT_WIDTH, KV_WIDTH, KV_WIDTH, CONV_WIDTH, CONV_WIDTH, CONV_WIDTH, D_MODEL, D_MODEL)
IN_WIDTH = sum(IN_SIZES)
IN_OFFS = tuple(int(s) for s in np.cumsum((0,) + IN_SIZES))

SUBLANES = 8
LANES = 128
ROW_TILES = D_MODEL // LANES
assert ROW_TILES == SUBLANES

PROMPT_TILE = 512
COMBINE_TILE = 256
DEST_GROUP = 64
SPARE_BLOCKS = PROMPT_TILE * TOP_K // MOE_BLOCK
TABLE_LANES = 256

F32 = jnp.float32
BF16 = jnp.bfloat16
HIGHEST = lax.Precision.HIGHEST
NT_DIMS = (((1,), (1,)), ((), ()))


def _vmem_limit(mib):
    return mib * 1024 * 1024


def _layernorm(x, g, b):
    mu = jnp.mean(x, axis=-1, keepdims=True)
    xc = x - mu
    var = jnp.mean(xc * xc, axis=-1, keepdims=True)
    return xc * lax.rsqrt(var + LN_EPS) * g + b


def _to_row_tiles(ref, x, rows):
    for s in range(ROW_TILES):
        ref[pl.ds(s, rows, stride=ROW_TILES), :] = x[:, s * LANES:(s + 1) * LANES]


def _from_row_tiles(ref, base, rows):
    return jnp.concatenate(
        [ref[pl.ds(base * ROW_TILES + s, rows, stride=ROW_TILES), :] for s in range(ROW_TILES)], axis=-1)


def _ada_kernel(c_ref, w_ref, b_ref, o_ref):
    c = c_ref[...]
    s = c * jax.nn.sigmoid(c)
    o_ref[0] = jnp.dot(s, w_ref[0], precision=HIGHEST, preferred_element_type=F32) + b_ref[0]


def _ada(c_all, w_ada, b_ada):
    depth = w_ada.shape[0]
    nb = c_all.shape[0]
    n_col = 6 * D_MODEL // D_MODEL
    return pl.pallas_call(
        _ada_kernel,
        out_shape=jax.ShapeDtypeStruct((depth, nb, 6 * D_MODEL), F32),
        grid=(depth, n_col),
        in_specs=[
            pl.BlockSpec((nb, D_MODEL), lambda l, j: (0, 0)),
            pl.BlockSpec((1, D_MODEL, D_MODEL), lambda l, j: (l, 0, j)),
            pl.BlockSpec((1, 1, D_MODEL), lambda l, j: (l, 0, j)),
        ],
        out_specs=pl.BlockSpec((1, nb, D_MODEL), lambda l, j: (l, 0, j)),
        compiler_params=pltpu.CompilerParams(dimension_semantics=("arbitrary", "arbitrary"),
                                             vmem_limit_bytes=_vmem_limit(32)),
        name="ada",
    )(c_all, w_ada, b_ada.reshape(depth, 1, 6 * D_MODEL))


def _rel_bucket(rel):
    half = NUM_BUCKETS // 2
    max_exact = half // 2
    n = jnp.abs(rel)
    n_f = jnp.maximum(n, 1).astype(jnp.float32)
    large = max_exact + (jnp.log(n_f / max_exact) / math.log(MAX_DISTANCE / max_exact)
                         * (half - max_exact)).astype(jnp.int32)
    large = jnp.minimum(large, half - 1)
    return jnp.where(rel > 0, half, 0) + jnp.where(n < max_exact, n, large)


BAND = WINDOW + CHUNK
PAIR_ROWS = 2 * CHUNK
PAIR_COLS = 2 * BAND


def _band_codes():
    r = jnp.arange(PAIR_ROWS)[:, None]
    j = jnp.arange(PAIR_COLS)[None, :]
    bucket = _rel_bucket(j % BAND - WINDOW - r % CHUNK)
    head = 2 * (r // CHUNK) + j // BAND
    return (bucket + NUM_BUCKETS * head).astype(jnp.int32)


def _bias_kernel(table_ref, code_ref, o_ref):
    g = pl.program_id(0)
    code = code_ref[...]
    acc = jnp.zeros(code.shape, F32)
    for hq in range(GROUP):
        for i in range(NUM_BUCKETS):
            acc = jnp.where(code == hq * NUM_BUCKETS + i, table_ref[i, g * GROUP + hq], acc)
    o_ref[0] = acc


def _bias_table(rel_table):
    return pl.pallas_call(
        _bias_kernel,
        out_shape=jax.ShapeDtypeStruct((N_KV_HEADS, PAIR_ROWS, PAIR_COLS), F32),
        grid=(N_KV_HEADS,),
        in_specs=[
            pl.BlockSpec(memory_space=pltpu.SMEM),
            pl.BlockSpec((PAIR_ROWS, PAIR_COLS), lambda g: (0, 0)),
        ],
        out_specs=pl.BlockSpec((1, PAIR_ROWS, PAIR_COLS), lambda g: (g, 0, 0)),
        compiler_params=pltpu.CompilerParams(dimension_semantics=("arbitrary",)),
        name="rel_bias",
    )(rel_table, _band_codes())


def _row(ref, r):
    return ref.at[pl.ds(pl.multiple_of(r * ROW_TILES, ROW_TILES), ROW_TILES), :]


def _mix_kernel(*refs, tile, apply_ln0, mask_first, alpha, n_steps, n_blocks, has_xs_in, finalize):
    (x_ref, mod_ref, ln0g_ref, ln0b_ref, win_ref, bin_ref, bias_ref, sink_ref, convw_ref,
     woa_ref, wob_ref, wo_ref, ln1g_ref, ln1b_ref, wr_ref, br_ref, k0_ref, v0_ref, u0_ref,
     upto_ref, st0_ref, tab0_ref) = refs[:22]
    refs = refs[22 + (1 if has_xs_in else 0):]
    (x1_ref, gate_ref, dest_ref, newk_ref, newv_ref, newu_ref, st_ref, tab_ref, xs_ref,
     kc_ref, vc_ref, ubuf_ref, run_ref, cur_ref, nfree_ref, h2buf, dest_v, dest_s, fin_v, fin_s,
     ssem, dsem, zsem) = refs
    i = pl.program_id(1)
    t = tile
    lin = pl.program_id(0) * pl.num_programs(1) + i
    now = lax.rem(lin, 2)
    prev = 1 - now
    slot_rows = t * ROW_TILES

    def dest_copy(slot):
        return pltpu.make_async_copy(dest_v.at[slot], dest_s.at[slot], dsem.at[slot])

    def scatter_wait(slot):
        for _ in range(TOP_K):
            pltpu.make_async_copy(h2buf.at[slot], xs_ref.at[pl.ds(0, slot_rows), :], ssem.at[slot]).wait()

    def scatter_row(slot, tok):
        for k in range(TOP_K):
            pltpu.make_async_copy(_row(h2buf.at[slot], tok), _row(xs_ref, dest_s[slot, k, tok]),
                                  ssem.at[slot]).start(priority=k % 2)

    @pl.when(lin == 0)
    def _():
        run_ref[...] = st0_ref[:, 0:1]
        cur_ref[...] = st0_ref[:, 1:2]
        nfree_ref[...] = st0_ref[0:1, 2:3]
        tab_ref[...] = tab0_ref[...]
        h2buf[1] = jnp.zeros(h2buf.shape[1:], F32)
        for k in range(TOP_K):
            def fill(tok, c, k=k):
                dest_s[1, k, tok] = n_blocks * MOE_BLOCK + k * t + tok
                return c
            lax.fori_loop(0, t, fill, 0)

    @pl.when(lin >= 1)
    def _():
        dest_copy(prev).wait()
        scatter_wait(now)

    @pl.when(i == 0)
    def _():
        kc_ref[...] = k0_ref[0]
        vc_ref[...] = v0_ref[0]
        ubuf_ref[0:SUBLANES, :] = u0_ref[0]

    for tok in range(t):
        scatter_row(prev, tok)

    x = x_ref[0]
    if apply_ln0:
        x = _layernorm(x, ln0g_ref[...], ln0b_ref[...])
    mod = mod_ref[0]
    sh1, sc1, g1, sh2, sc2, g2 = [mod[j:j + 1, :] for j in range(6)]
    h = (x * (1.0 + sc1) + sh1).astype(BF16)

    def proj(j0, j1):
        lo, hi = IN_OFFS[j0], IN_OFFS[j1]
        return jnp.dot(h, win_ref[:, lo:hi], preferred_element_type=F32) + bin_ref[:, lo:hi]

    q = proj(0, 1)
    kv = proj(1, 3)
    kfull = jnp.concatenate([kc_ref[...], kv[:, :KV_WIDTH]], axis=0)
    vfull = jnp.concatenate([vc_ref[...], kv[:, KV_WIDTH:]], axis=0)
    kc_ref[...] = kfull[t:, :]
    vc_ref[...] = vfull[t:, :]
    newk_ref[0] = kfull[t:, :]
    newv_ref[0] = vfull[t:, :]
    low = lax.broadcasted_iota(jnp.int32, kfull.shape, 1) < HEAD_DIM
    k_sw = pltpu.roll(kfull, HEAD_DIM, axis=1)
    v_sw = pltpu.roll(vfull, HEAD_DIM, axis=1)
    k_even = [jnp.where(low, kfull, 0.0).astype(BF16), jnp.where(low, k_sw, 0.0).astype(BF16)]
    k_odd = [jnp.where(low, 0.0, k_sw).astype(BF16), jnp.where(low, 0.0, kfull).astype(BF16)]
    v_even = [jnp.where(low, vfull, 0.0).astype(BF16), jnp.where(low, v_sw, 0.0).astype(BF16)]
    v_odd = [jnp.where(low, 0.0, v_sw).astype(BF16), jnp.where(low, 0.0, vfull).astype(BF16)]
    col = lax.broadcasted_iota(jnp.int32, (PAIR_ROWS, PAIR_COLS), 1)
    even = col < BAND
    band_col = jnp.where(even, col, col - BAND)
    first_pair = lax.broadcasted_iota(jnp.int32, (PAIR_ROWS, 1), 0) < CHUNK
    out_low = lax.broadcasted_iota(jnp.int32, (PAIR_ROWS, LANES), 1) < HEAD_DIM
    sink_even = [jnp.where(first_pair, sink_ref[g * GROUP], sink_ref[g * GROUP + 2]) for g in range(N_KV_HEADS)]
    sink_odd = [jnp.where(first_pair, sink_ref[g * GROUP + 1], sink_ref[g * GROUP + 3])
                for g in range(N_KV_HEADS)]
    units = [(c, g) for c in range(t // CHUNK) for g in range(N_KV_HEADS)]
    scores = []
    for c, g in units:
        r0 = c * CHUNK
        ql = jnp.concatenate([q[r0:r0 + CHUNK, (2 * g) * LANES:(2 * g + 1) * LANES],
                              q[r0:r0 + CHUNK, (2 * g + 1) * LANES:(2 * g + 2) * LANES]], axis=0)
        kp = jnp.concatenate([k_even[g][r0:r0 + BAND], k_odd[g][r0:r0 + BAND]], axis=0)
        s = lax.dot_general(ql.astype(BF16), kp, NT_DIMS, preferred_element_type=F32)
        s = s * (HEAD_DIM ** -0.5) + bias_ref[g]
        if mask_first and c < WINDOW // CHUNK:
            s = jnp.where(band_col < jnp.where(i == 0, WINDOW - r0, 0), NEG_INF, s)
        scores.append(s)
    maxes = []
    for (c, g), s in zip(units, scores):
        m_e = jnp.maximum(jnp.max(jnp.where(even, s, -jnp.inf), axis=-1, keepdims=True), sink_even[g])
        m_o = jnp.maximum(jnp.max(jnp.where(even, -jnp.inf, s), axis=-1, keepdims=True), sink_odd[g])
        maxes.append((m_e, m_o))
    exps, scales = [], []
    for (c, g), s, (m_e, m_o) in zip(units, scores, maxes):
        e = jnp.exp(s - jnp.where(even, m_e, m_o))
        d_e = jnp.sum(jnp.where(even, e, 0.0), axis=-1, keepdims=True) + jnp.exp(sink_even[g] - m_e)
        d_o = jnp.sum(jnp.where(even, 0.0, e), axis=-1, keepdims=True) + jnp.exp(sink_odd[g] - m_o)
        exps.append(e.astype(BF16))
        scales.append(jnp.where(out_low, 1.0 / d_e, 1.0 / d_o))
    outs = {}
    for (c, g), e, scale in zip(units, exps, scales):
        r0 = c * CHUNK
        vp = jnp.concatenate([v_even[g][r0:r0 + BAND], v_odd[g][r0:r0 + BAND]], axis=0)
        outs[c, g] = jnp.dot(e, vp, preferred_element_type=F32) * scale
    ya = jnp.concatenate(
        [jnp.concatenate([outs[c, g][half * CHUNK:(half + 1) * CHUNK] for g in range(N_KV_HEADS)
                          for half in range(2)], axis=-1) for c in range(t // CHUNK)], axis=0)

    cb = proj(3, 4)
    u = proj(4, 5) * proj(5, 6)
    ubuf_ref[SUBLANES:t + SUBLANES, :] = u
    cw = convw_ref[...]
    yc = (cw[0:1, :] * ubuf_ref[SUBLANES - 2:t + SUBLANES - 2, :]
          + cw[1:2, :] * ubuf_ref[SUBLANES - 1:t + SUBLANES - 1, :] + cw[2:3, :] * u)
    yb = cb * yc
    tail = ubuf_ref[t:t + SUBLANES, :]
    newu_ref[0] = tail
    ubuf_ref[0:SUBLANES, :] = tail

    a_out = jnp.dot(ya.astype(BF16), woa_ref[...], preferred_element_type=F32)
    b_out = jnp.dot(yb.astype(BF16), wob_ref[...], preferred_element_type=F32)
    mixin = jax.nn.sigmoid(proj(6, 7)) * a_out + jax.nn.sigmoid(proj(7, 8)) * b_out
    mix = jnp.dot(mixin.astype(BF16), wo_ref[...], preferred_element_type=F32)
    x1 = _layernorm(alpha * x + (1.0 + g1) * mix, ln1g_ref[...], ln1b_ref[...])
    x1_ref[0] = x1

    h2 = x1 * (1.0 + sc2) + sh2
    _to_row_tiles(h2buf.at[now], h2, t)
    logits = lax.dot_general(wr_ref[...], h2.astype(BF16), NT_DIMS, preferred_element_type=F32) + br_ref[...]
    eid = lax.broadcasted_iota(jnp.int32, (N_EXPERTS, t), 0)
    vals, ids = [], []
    for _ in range(TOP_K):
        mx = jnp.max(logits, axis=0, keepdims=True)
        sel = jnp.min(jnp.where(logits == mx, eid, N_EXPERTS), axis=0, keepdims=True)
        vals.append(mx)
        ids.append(sel)
        logits = jnp.where(eid == sel, -jnp.inf, logits)
    ex = [jnp.exp(v - vals[0]) for v in vals]
    tot = ex[0] + ex[1] + ex[2] + ex[3]
    gate_ref[0] = jnp.concatenate([e_ / tot for e_ in ex], axis=0)

    per_block = 1.0 / MOE_BLOCK
    hits = [(eid == ids[k]).astype(F32) for k in range(TOP_K)]
    hit = hits[0] + hits[1] + hits[2] + hits[3]
    count = jnp.sum(hit, axis=1, keepdims=True)
    incl = jnp.dot(hit.astype(BF16), upto_ref[...], preferred_element_type=F32)
    run = run_ref[...]
    cur = cur_ref[...]
    run_blk = jnp.floor(run * per_block)
    is_open = (run - run_blk * MOE_BLOCK > 0).astype(F32)
    end = run + count
    blocks_before = jnp.ceil(run * per_block)
    n_new = jnp.ceil(end * per_block) - blocks_before
    er = lax.broadcasted_iota(jnp.int32, (N_EXPERTS, N_EXPERTS), 0)
    ec = lax.broadcasted_iota(jnp.int32, (N_EXPERTS, N_EXPERTS), 1)
    earlier = jnp.dot((ec < er).astype(BF16), jnp.broadcast_to(n_new, (N_EXPERTS, LANES)).astype(BF16),
                      preferred_element_type=F32)[:, 0:1]
    base = nfree_ref[...] + earlier
    r = run + incl - hit
    r_blk = jnp.floor(r * per_block)
    ordinal = r_blk - run_blk
    block = jnp.where(jnp.logical_and(is_open > 0, ordinal == 0), cur, base + ordinal - is_open)
    row = block * MOE_BLOCK + (r - r_blk * MOE_BLOCK)
    dests = [jnp.sum(hits[k] * row, axis=0, keepdims=True).astype(jnp.int32) for k in range(TOP_K)]
    dest_v[now] = jnp.concatenate(dests, axis=0)
    for k in range(TOP_K):
        for gidx in range(t // DEST_GROUP):
            dest_ref[gidx, k:k + 1, :] = dests[k][:, gidx * DEST_GROUP:(gidx + 1) * DEST_GROUP]
    lane = lax.broadcasted_iota(jnp.int32, tab_ref.shape, 1).astype(F32)
    table = tab_ref[...]
    for j in range(-(-t // MOE_BLOCK)):
        table = jnp.where(jnp.logical_and(lane == blocks_before + j, n_new > j), base + j, table)
    tab_ref[...] = table
    cur = jnp.where(n_new > 0, base + n_new - 1, cur)
    nfree = nfree_ref[...] + jnp.sum(n_new, axis=0, keepdims=True)
    run_ref[...] = end
    cur_ref[...] = cur
    nfree_ref[...] = nfree
    st_lane = lax.broadcasted_iota(jnp.int32, st_ref.shape, 1)
    st_ref[...] = jnp.where(st_lane == 0, end, jnp.where(st_lane == 1, cur, nfree))
    dest_copy(now).start()

    @pl.when(lin == n_steps - 1)
    def _():
        dest_copy(now).wait()
        lax.fori_loop(0, t, lambda tok, c: (scatter_row(now, tok), c)[1], 0)
        scatter_wait(prev)
        scatter_wait(now)
        if finalize:
            rem = end - jnp.floor(end * per_block) * MOE_BLOCK
            tail_lo = cur * MOE_BLOCK + rem
            tail_hi = jnp.where(rem > 0, (cur + 1.0) * MOE_BLOCK, tail_lo)
            fin_v[...] = jnp.where(st_lane == 0, tail_lo, jnp.where(st_lane == 1, tail_hi, nfree)).astype(jnp.int32)
            fin = pltpu.make_async_copy(fin_v, fin_s, dsem.at[0])
            fin.start()
            fin.wait()
            h2buf[0] = jnp.zeros(h2buf.shape[1:], F32)

            def zero_row(r_):
                return pltpu.make_async_copy(_row(h2buf.at[0], 0), _row(xs_ref, r_), zsem)

            piece = min(slot_rows, MOE_BLOCK * ROW_TILES)

            def zero_part(c_):
                return pltpu.make_async_copy(
                    h2buf.at[0].at[pl.ds(0, piece), :],
                    xs_ref.at[pl.ds(pl.multiple_of(c_ * piece, piece), piece), :], zsem)

            def per_expert(e, carry):
                lax.fori_loop(fin_s[e, 0], fin_s[e, 1], lambda r_, c: (zero_row(r_).start(), c)[1], 0)
                lax.fori_loop(fin_s[e, 0], fin_s[e, 1], lambda r_, c: (zero_row(r_).wait(), c)[1], 0)
                return carry

            lax.fori_loop(0, N_EXPERTS, per_expert, 0)
            parts = MOE_BLOCK * ROW_TILES // piece
            lo, hi = fin_s[0, 2] * parts, n_blocks * parts
            lax.fori_loop(lo, hi, lambda c_, c: (zero_part(c_).start(), c)[1], 0)
            lax.fori_loop(lo, hi, lambda c_, c: (zero_part(c_).wait(), c)[1], 0)


def _mix(x, mod, ln0_g, ln0_b, w_in, b_in, bias, sink, conv_w, w_oa, w_ob, w_o, ln1_g, ln1_b,
         w_rt, b_r, k0, v0, u0, state, table, xs_in, *, tile, apply_ln0, mask_first, alpha, n_blocks, finalize):
    b, s, d = x.shape
    n_t = s // tile
    const = lambda shape: pl.BlockSpec(shape, lambda bb, ii: (0,) * len(shape), pipeline_mode=pl.Buffered(1))
    per_b = lambda shape: pl.BlockSpec((1,) + shape, lambda bb, ii: (bb,) + (0,) * len(shape))
    whole = lambda shape: pl.BlockSpec(shape, lambda bb, ii: (0,) * len(shape))
    kern = functools.partial(_mix_kernel, tile=tile, apply_ln0=apply_ln0, mask_first=mask_first, alpha=alpha,
                             n_steps=b * n_t, n_blocks=n_blocks, has_xs_in=xs_in is not None, finalize=finalize)
    upto = (jnp.arange(tile)[:, None] <= jnp.arange(tile)[None, :]).astype(BF16)
    xs_rows = (n_blocks + SPARE_BLOCKS) * MOE_BLOCK * ROW_TILES
    groups = tile // DEST_GROUP
    operands = [x, mod, ln0_g, ln0_b, w_in, b_in, bias, sink, conv_w, w_oa, w_ob, w_o, ln1_g, ln1_b, w_rt, b_r,
                k0, v0, u0, upto, state, table]
    in_specs = [
        pl.BlockSpec((1, tile, d), lambda bb, ii: (bb, ii, 0)),
        per_b((6, d)),
        const((1, d)), const((1, d)),
        const((d, IN_WIDTH)), const((1, IN_WIDTH)),
        const((N_KV_HEADS, PAIR_ROWS, PAIR_COLS)),
        pl.BlockSpec(memory_space=pltpu.SMEM),
        const((CONV_K, CONV_WIDTH)),
        const((ATT_WIDTH, d)), const((CONV_WIDTH, d)), const((d, d)),
        const((1, d)), const((1, d)),
        const((N_EXPERTS, d)), const((N_EXPERTS, 1)),
        per_b((WINDOW, KV_WIDTH)), per_b((WINDOW, KV_WIDTH)), per_b((SUBLANES, CONV_WIDTH)),
        const((tile, tile)), const((N_EXPERTS, LANES)), const((N_EXPERTS, TABLE_LANES)),
    ]
    aliases = {}
    if xs_in is not None:
        operands.append(xs_in)
        in_specs.append(pl.BlockSpec(memory_space=pl.ANY))
        aliases = {len(operands) - 1: 8}
    return pl.pallas_call(
        kern,
        out_shape=(
            jax.ShapeDtypeStruct((b, s, d), F32),
            jax.ShapeDtypeStruct((b * n_t, TOP_K, tile), F32),
            jax.ShapeDtypeStruct((b * s // DEST_GROUP, TOP_K, DEST_GROUP), jnp.int32),
            jax.ShapeDtypeStruct((b, WINDOW, KV_WIDTH), F32),
            jax.ShapeDtypeStruct((b, WINDOW, KV_WIDTH), F32),
            jax.ShapeDtypeStruct((b, SUBLANES, CONV_WIDTH), F32),
            jax.ShapeDtypeStruct((N_EXPERTS, LANES), F32),
            jax.ShapeDtypeStruct((N_EXPERTS, TABLE_LANES), F32),
            jax.ShapeDtypeStruct((xs_rows, LANES), F32),
        ),
        grid=(b, n_t),
        in_specs=in_specs,
        out_specs=(
            pl.BlockSpec((1, tile, d), lambda bb, ii: (bb, ii, 0)),
            pl.BlockSpec((1, TOP_K, tile), lambda bb, ii: (bb * n_t + ii, 0, 0)),
            pl.BlockSpec((groups, TOP_K, DEST_GROUP), lambda bb, ii: (bb * n_t + ii, 0, 0)),
            per_b((WINDOW, KV_WIDTH)), per_b((WINDOW, KV_WIDTH)), per_b((SUBLANES, CONV_WIDTH)),
            whole((N_EXPERTS, LANES)), whole((N_EXPERTS, TABLE_LANES)),
            pl.BlockSpec(memory_space=pl.ANY),
        ),
        scratch_shapes=[
            pltpu.VMEM((WINDOW, KV_WIDTH), F32),
            pltpu.VMEM((WINDOW, KV_WIDTH), F32),
            pltpu.VMEM((tile + SUBLANES, CONV_WIDTH), F32),
            pltpu.VMEM((N_EXPERTS, 1), F32),
            pltpu.VMEM((N_EXPERTS, 1), F32),
            pltpu.VMEM((1, 1), F32),
            pltpu.VMEM((2, tile * ROW_TILES, LANES), F32),
            pltpu.VMEM((2, TOP_K, tile), jnp.int32),
            pltpu.SMEM((2, TOP_K, tile), jnp.int32),
            pltpu.VMEM((N_EXPERTS, LANES), jnp.int32),
            pltpu.SMEM((N_EXPERTS, LANES), jnp.int32),
            pltpu.SemaphoreType.DMA((2,)),
            pltpu.SemaphoreType.DMA((2,)),
            pltpu.SemaphoreType.DMA,
        ],
        input_output_aliases=aliases,
        compiler_params=pltpu.CompilerParams(dimension_semantics=("arbitrary", "arbitrary"),
                                             vmem_limit_bytes=_vmem_limit(56)),
        name="mix",
    )(*operands)


def _ffn_kernel(first_ref, nblk_ref, nused_ref, order_ref, xs_ref, wgu_ref, bgu_ref, wd_ref, bd_ref, ys_ref,
                wgu_bf, wd_bf, xbuf, ybuf, in_sem, out_sem, *, n_blocks):
    e = pl.program_id(0)
    first = first_ref[e]
    n_used = nused_ref[0]
    block_rows = MOE_BLOCK * ROW_TILES

    def blk(ref, b):
        return ref.at[pl.ds(pl.multiple_of(b * block_rows, block_rows), block_rows), :]

    def in_copy(b, slot):
        return pltpu.make_async_copy(blk(xs_ref, order_ref[b]), xbuf.at[slot], in_sem.at[slot])

    def out_copy(b, slot):
        return pltpu.make_async_copy(ybuf.at[slot], blk(ys_ref, order_ref[b]), out_sem.at[slot])

    @pl.when(e == 0)
    def _():
        in_copy(0, 0).start()

    @pl.when(nblk_ref[e] > 0)
    def _():
        wgu_bf[...] = wgu_ref[0, 0].astype(BF16)
        wd_bf[...] = wd_ref[0, 0].astype(BF16)

    def body(j, carry):
        b = first + j
        slot = lax.rem(b, 2)
        in_copy(b, slot).wait()

        @pl.when(b + 1 < n_used)
        def _():
            in_copy(b + 1, 1 - slot).start()

        @pl.when(b >= 2)
        def _():
            out_copy(b - 2, slot).wait()

        x = _from_row_tiles(xbuf.at[slot], 0, MOE_BLOCK).astype(BF16)
        gu = jnp.dot(x, wgu_bf[...], preferred_element_type=F32) + bgu_ref[0, 0]
        g = jnp.minimum(gu[:, :D_FF], SWIGLU_LIMIT)
        lin = jnp.clip(gu[:, D_FF:], -SWIGLU_LIMIT, SWIGLU_LIMIT)
        a = g * jax.nn.sigmoid(SWIGLU_ALPHA * g) * (lin + 1.0)
        y = jnp.dot(a.astype(BF16), wd_bf[...], preferred_element_type=F32) + bd_ref[0, 0]
        _to_row_tiles(ybuf.at[slot], y, MOE_BLOCK)
        out_copy(b, slot).start()
        return carry

    lax.fori_loop(0, nblk_ref[e], body, 0)

    @pl.when(e == N_EXPERTS - 1)
    def _():
        @pl.when(n_used >= 2)
        def _():
            out_copy(n_used - 2, lax.rem(n_used, 2)).wait()

        out_copy(n_used - 1, lax.rem(n_used - 1, 2)).wait()
        ybuf[0] = jnp.zeros(ybuf.shape[1:], F32)
        lax.fori_loop(n_used, n_blocks, lambda b, c: (out_copy(b, 0).start(), c)[1], 0)
        lax.fori_loop(n_used, n_blocks, lambda b, c: (out_copy(b, 0).wait(), c)[1], 0)


def _ffn(layer, first_blk, n_blk, n_used, order, xs, w_gu, b_gu, w_down, b_down, n_blocks):
    block_rows = MOE_BLOCK * ROW_TILES
    depth = w_gu.shape[0]

    def expert(e, *_):
        return (layer, e, 0, 0)

    return pl.pallas_call(
        functools.partial(_ffn_kernel, n_blocks=n_blocks),
        out_shape=jax.ShapeDtypeStruct((n_blocks * block_rows, LANES), F32),
        grid_spec=pltpu.PrefetchScalarGridSpec(
            num_scalar_prefetch=4,
            grid=(N_EXPERTS,),
            in_specs=[
                pl.BlockSpec(memory_space=pl.ANY),
                pl.BlockSpec((1, 1, D_MODEL, 2 * D_FF), expert),
                pl.BlockSpec((1, 1, 1, 2 * D_FF), expert),
                pl.BlockSpec((1, 1, D_FF, D_MODEL), expert),
                pl.BlockSpec((1, 1, 1, D_MODEL), expert),
            ],
            out_specs=pl.BlockSpec(memory_space=pl.ANY),
            scratch_shapes=[
                pltpu.VMEM((D_MODEL, 2 * D_FF), BF16),
                pltpu.VMEM((D_FF, D_MODEL), BF16),
                pltpu.VMEM((2, block_rows, LANES), F32),
                pltpu.VMEM((2, block_rows, LANES), F32),
                pltpu.SemaphoreType.DMA((2,)),
                pltpu.SemaphoreType.DMA((2,)),
            ],
        ),
        compiler_params=pltpu.CompilerParams(dimension_semantics=("arbitrary",),
                                             vmem_limit_bytes=_vmem_limit(56)),
        name="ffn",
    )(first_blk, n_blk, n_used, order, xs, w_gu, b_gu.reshape(depth, N_EXPERTS, 1, 2 * D_FF), w_down,
      b_down.reshape(depth, N_EXPERTS, 1, D_MODEL))


def _combine_kernel(dest_ref, next_ref, x1_ref, mod_ref, gate_ref, ln2g_ref, ln2b_ref, ys_ref, o_ref,
                    buf0, buf1, sem0, sem1, *, tile, alpha, n_steps):
    t = tile
    step = pl.program_id(0) * pl.num_programs(1) + pl.program_id(1)

    def fetch(d_ref, g, tt, buf, sem):
        tok = g * DEST_GROUP + tt
        for k in range(TOP_K):
            pltpu.make_async_copy(_row(ys_ref, d_ref[g, k, tt]), _row(buf, k * t + tok),
                                  sem).start(priority=k % 2)

    def wait(buf, sem):
        pltpu.make_async_copy(ys_ref.at[pl.ds(0, TOP_K * t * ROW_TILES), :], buf, sem).wait()

    @pl.when(step == 0)
    def _():
        for g in range(t // DEST_GROUP):
            lax.fori_loop(0, DEST_GROUP, lambda tt, c, g=g: (fetch(dest_ref, g, tt, buf0, sem0), c)[1], 0)

    def phase(buf, sem, nxt_buf, nxt_sem):
        wait(buf, sem)
        for g in range(t // DEST_GROUP):
            for tt in range(DEST_GROUP):
                fetch(next_ref, g, tt, nxt_buf, nxt_sem)
        gates = gate_ref[0]
        gates_t = jnp.transpose(jnp.concatenate([gates, jnp.zeros_like(gates)], axis=0))
        ff = jnp.zeros((t, D_MODEL), F32)
        for k in range(TOP_K):
            ff = ff + gates_t[:, k:k + 1] * _from_row_tiles(buf, k * t, t)
        g2 = mod_ref[0][5:6, :]
        o_ref[0] = _layernorm(alpha * x1_ref[0] + (1.0 + g2) * ff, ln2g_ref[...], ln2b_ref[...])

    @pl.when(step % 2 == 0)
    def _():
        phase(buf0, sem0, buf1, sem1)

    @pl.when(step % 2 == 1)
    def _():
        phase(buf1, sem1, buf0, sem0)

    @pl.when(step == n_steps - 1)
    def _():
        if (n_steps - 1) % 2 == 0:
            wait(buf1, sem1)
        else:
            wait(buf0, sem0)


def _combine(dest, group_offset, x1, mod, gates, ln2_g, ln2_b, ys, *, tile, alpha):
    b, s, d = x1.shape
    n_t = s // tile
    n_steps = b * n_t
    groups = tile // DEST_GROUP
    goff = group_offset // groups
    per_gate_row = gates.shape[2] // tile
    kern = functools.partial(_combine_kernel, tile=tile, alpha=alpha, n_steps=n_steps)
    buf = pltpu.VMEM((TOP_K * tile * ROW_TILES, LANES), F32)
    return pl.pallas_call(
        kern,
        out_shape=jax.ShapeDtypeStruct((b, s, d), F32),
        grid=(b, n_t),
        in_specs=[
            pl.BlockSpec((groups, TOP_K, DEST_GROUP), lambda bb, ii: (goff + bb * n_t + ii, 0, 0),
                         memory_space=pltpu.SMEM),
            pl.BlockSpec((groups, TOP_K, DEST_GROUP),
                         lambda bb, ii: (goff + jnp.minimum(bb * n_t + ii + 1, n_steps - 1), 0, 0),
                         memory_space=pltpu.SMEM),
            pl.BlockSpec((1, tile, d), lambda bb, ii: (bb, ii, 0)),
            pl.BlockSpec((1, 6, d), lambda bb, ii: (bb, 0, 0)),
            pl.BlockSpec((1, TOP_K, tile), lambda bb, ii: ((bb * n_t + ii) // per_gate_row, 0,
                                                           (bb * n_t + ii) % per_gate_row)),
            pl.BlockSpec((1, d), lambda bb, ii: (0, 0)),
            pl.BlockSpec((1, d), lambda bb, ii: (0, 0)),
            pl.BlockSpec(memory_space=pl.ANY),
        ],
        out_specs=pl.BlockSpec((1, tile, d), lambda bb, ii: (bb, ii, 0)),
        scratch_shapes=[buf, buf, pltpu.SemaphoreType.DMA, pltpu.SemaphoreType.DMA],
        compiler_params=pltpu.CompilerParams(dimension_semantics=("arbitrary", "arbitrary"),
                                             vmem_limit_bytes=_vmem_limit(32)),
        name="combine",
    )(dest, dest, x1, mod, gates, ln2_g, ln2_b, ys)


def kernel(x_prompt, x_sample, c_prompt, c_sample, cache_k, cache_v, state_conv, rel_table, ln0_g, ln0_b, w_ada, b_ada, w_in, b_in, sinks, conv_w, w_oa, w_ob, w_o, ln1_g, ln1_b, w_router, b_router, w_gu, b_gu, w_down, b_down, ln2_g, ln2_b):
    depth = w_ada.shape[0]
    bp, sp, d = x_prompt.shape
    bs, ss, _ = x_sample.shape
    alpha = (2 * depth) ** 0.25
    n_tok = bp * sp + bs * ss
    assert sp % PROMPT_TILE == 0 and ss % DEST_GROUP == 0 and ss <= WINDOW
    assert -(-n_tok // MOE_BLOCK) <= TABLE_LANES
    n_blocks = -(-(n_tok * TOP_K) // MOE_BLOCK) + N_EXPERTS

    mod_all = _ada(jnp.concatenate([c_prompt, c_sample], axis=0), w_ada, b_ada)
    mod_all = mod_all.reshape(depth, bp + bs, 6, d)
    bias = _bias_table(rel_table)
    row = lambda a: a.reshape(1, -1)
    zeros_kv = jnp.zeros((bp, WINDOW, KV_WIDTH), F32)
    zeros_u = jnp.zeros((bp, SUBLANES, CONV_WIDTH), F32)
    state0 = jnp.zeros((N_EXPERTS, LANES), F32)
    table0 = jnp.zeros((N_EXPERTS, TABLE_LANES), F32)
    slot = jnp.arange(n_blocks, dtype=jnp.int32)

    y_p, y_s = x_prompt, x_sample
    outs = {name: [] for name in ("kp", "vp", "up", "ks", "vs", "us")}
    for l in range(depth):
        shared = (w_in[l].astype(BF16), row(b_in[l]))
        tail = (sinks[l], conv_w[l], w_oa[l].astype(BF16), w_ob[l].astype(BF16), w_o[l].astype(BF16),
                row(ln1_g[l]), row(ln1_b[l]), w_router[l].T.astype(BF16), b_router[l].reshape(N_EXPERTS, 1))
        mod_p, mod_s = mod_all[l, :bp], mod_all[l, bp:]
        x1_p, gate_p, dest_p, k_p, v_p, u_p, state, table, xs = _mix(
            y_p, mod_p, row(ln0_g), row(ln0_b), *shared, bias, *tail, zeros_kv, zeros_kv, zeros_u,
            state0, table0, None, tile=PROMPT_TILE, apply_ln0=(l == 0), mask_first=True, alpha=alpha,
            n_blocks=n_blocks, finalize=True)
        u0 = jnp.pad(state_conv[l], ((0, 0), (SUBLANES - (CONV_K - 1), 0), (0, 0)))
        x1_s, gate_s, dest_s, k_s, v_s, u_s, state, table, xs = _mix(
            y_s, mod_s, row(ln0_g), row(ln0_b), *shared, bias, *tail,
            cache_k[l].reshape(bs, WINDOW, KV_WIDTH), cache_v[l].reshape(bs, WINDOW, KV_WIDTH), u0,
            state, table, xs, tile=ss, apply_ln0=(l == 0), mask_first=False, alpha=alpha,
            n_blocks=n_blocks, finalize=False)

        counts = state[:, 0].astype(jnp.int32)
        n_used = state[0:1, 2].astype(jnp.int32)
        n_blk = (counts + MOE_BLOCK - 1) // MOE_BLOCK
        blk_end = jnp.cumsum(n_blk)
        first_blk = blk_end - n_blk
        owner = jnp.minimum(jnp.sum((blk_end[None, :] <= slot[:, None]).astype(jnp.int32), axis=1), N_EXPERTS - 1)
        nth = jnp.clip(slot - first_blk[owner], 0, TABLE_LANES - 1)
        order = jnp.where(slot < n_used, table[owner, nth].astype(jnp.int32), slot)

        ys = _ffn(l, first_blk, n_blk, n_used, order, xs, w_gu, b_gu, w_down, b_down, n_blocks)
        y_p = _combine(dest_p, 0, x1_p, mod_p, gate_p, row(ln2_g[l]), row(ln2_b[l]), ys,
                       tile=COMBINE_TILE, alpha=alpha)
        y_s = _combine(dest_s, 0, x1_s, mod_s, gate_s, row(ln2_g[l]), row(ln2_b[l]), ys,
                       tile=ss, alpha=alpha)

        outs["kp"].append(k_p.reshape(bp, WINDOW, N_KV_HEADS, HEAD_DIM))
        outs["vp"].append(v_p.reshape(bp, WINDOW, N_KV_HEADS, HEAD_DIM))
        outs["up"].append(u_p[:, SUBLANES - (CONV_K - 1):, :])
        outs["ks"].append(k_s.reshape(bs, WINDOW, N_KV_HEADS, HEAD_DIM))
        outs["vs"].append(v_s.reshape(bs, WINDOW, N_KV_HEADS, HEAD_DIM))
        outs["us"].append(u_s[:, SUBLANES - (CONV_K - 1):, :])
    return (y_p, y_s, jnp.stack(outs["kp"]), jnp.stack(outs["vp"]), jnp.stack(outs["up"]),
            jnp.stack(outs["ks"]), jnp.stack(outs["vs"]), jnp.stack(outs["us"]))
```

```python
import functools
import math

import jax
import jax.numpy as jnp
import numpy as np
from jax import lax
from jax.experimental import pallas as pl
from jax.experimental.pallas import tpu as pltpu

D_MODEL = 1024
CHUNK = 64
N_HEADS = 8
N_KV_HEADS = 2
HEAD_DIM = 64
GROUP = N_HEADS // N_KV_HEADS
ATT_WIDTH = N_HEADS * HEAD_DIM
KV_WIDTH = N_KV_HEADS * HEAD_DIM
WINDOW = 128
CONV_WIDTH = 512
CONV_K = 3
NUM_BUCKETS = 32
MAX_DISTANCE = 128
N_EXPERTS = 32
TOP_K = 4
D_FF = 1024
SWIGLU_LIMIT = 7.0
SWIGLU_ALPHA = 1.702
MOE_BLOCK = 512
LN_EPS = 1e-5
NEG_INF = -1e30
IN_SIZES = (ATT_WIDTH, KV_WIDTH, KV_WIDTH, CONV_WIDTH, CONV_WIDTH, CONV_WIDTH, D_MODEL, D_MODEL)
IN_WIDTH = sum(IN_SIZES)
IN_OFFS = tuple(int(s) for s in np.cumsum((0,) + IN_SIZES))

SUBLANES = 8
LANES = 128
ROW_TILES = D_MODEL // LANES
assert ROW_TILES == SUBLANES

PROMPT_TILE = 512
COMBINE_TILE = 256
DEST_GROUP = 64
LAG = 2
RING = 2 * LAG
SPARE_BLOCKS = LAG * PROMPT_TILE * TOP_K // MOE_BLOCK
TABLE_LANES = 256

F32 = jnp.float32
BF16 = jnp.bfloat16
HIGHEST = lax.Precision.HIGHEST
NT_DIMS = (((1,), (1,)), ((), ()))


def _vmem_limit(mib):
    return mib * 1024 * 1024


def _layernorm(x, g, b):
    mu = jnp.mean(x, axis=-1, keepdims=True)
    xc = x - mu
    var = jnp.mean(xc * xc, axis=-1, keepdims=True)
    return xc * lax.rsqrt(var + LN_EPS) * g + b


def _to_row_tiles(ref, x, rows):
    for s in range(ROW_TILES):
        ref[pl.ds(s, rows, stride=ROW_TILES), :] = x[:, s * LANES:(s + 1) * LANES]


def _from_row_tiles(ref, base, rows):
    return jnp.concatenate(
        [ref[pl.ds(base * ROW_TILES + s, rows, stride=ROW_TILES), :] for s in range(ROW_TILES)], axis=-1)


def _ada_kernel(c_ref, w_ref, b_ref, o_ref):
    c = c_ref[...]
    s = c * jax.nn.sigmoid(c)
    o_ref[0] = jnp.dot(s, w_ref[0], precision=HIGHEST, preferred_element_type=F32) + b_ref[0]


def _ada(c_all, w_ada, b_ada):
    depth = w_ada.shape[0]
    nb = c_all.shape[0]
    n_col = 6 * D_MODEL // D_MODEL
    return pl.pallas_call(
        _ada_kernel,
        out_shape=jax.ShapeDtypeStruct((depth, nb, 6 * D_MODEL), F32),
        grid=(depth, n_col),
        in_specs=[
            pl.BlockSpec((nb, D_MODEL), lambda l, j: (0, 0)),
            pl.BlockSpec((1, D_MODEL, D_MODEL), lambda l, j: (l, 0, j)),
            pl.BlockSpec((1, 1, D_MODEL), lambda l, j: (l, 0, j)),
        ],
        out_specs=pl.BlockSpec((1, nb, D_MODEL), lambda l, j: (l, 0, j)),
        compiler_params=pltpu.CompilerParams(dimension_semantics=("arbitrary", "arbitrary"),
                                             vmem_limit_bytes=_vmem_limit(32)),
        name="ada",
    )(c_all, w_ada, b_ada.reshape(depth, 1, 6 * D_MODEL))


def _rel_bucket(rel):
    half = NUM_BUCKETS // 2
    max_exact = half // 2
    n = jnp.abs(rel)
    n_f = jnp.maximum(n, 1).astype(jnp.float32)
    large = max_exact + (jnp.log(n_f / max_exact) / math.log(MAX_DISTANCE / max_exact)
                         * (half - max_exact)).astype(jnp.int32)
    large = jnp.minimum(large, half - 1)
    return jnp.where(rel > 0, half, 0) + jnp.where(n < max_exact, n, large)


BAND = WINDOW + CHUNK
PAIR_ROWS = 2 * CHUNK
PAIR_COLS = 2 * BAND


def _band_codes():
    r = jnp.arange(PAIR_ROWS)[:, None]
    j = jnp.arange(PAIR_COLS)[None, :]
    bucket = _rel_bucket(j % BAND - WINDOW - r % CHUNK)
    head = 2 * (r // CHUNK) + j // BAND
    return (bucket + NUM_BUCKETS * head).astype(jnp.int32)


def _bias_kernel(table_ref, code_ref, o_ref):
    g = pl.program_id(0)
    code = code_ref[...]
    acc = jnp.zeros(code.shape, F32)
    for hq in range(GROUP):
        for i in range(NUM_BUCKETS):
            acc = jnp.where(code == hq * NUM_BUCKETS + i, table_ref[i, g * GROUP + hq], acc)
    o_ref[0] = acc


def _bias_table(rel_table):
    return pl.pallas_call(
        _bias_kernel,
        out_shape=jax.ShapeDtypeStruct((N_KV_HEADS, PAIR_ROWS, PAIR_COLS), F32),
        grid=(N_KV_HEADS,),
        in_specs=[
            pl.BlockSpec(memory_space=pltpu.SMEM),
            pl.BlockSpec((PAIR_ROWS, PAIR_COLS), lambda g: (0, 0)),
        ],
        out_specs=pl.BlockSpec((1, PAIR_ROWS, PAIR_COLS), lambda g: (g, 0, 0)),
        compiler_params=pltpu.CompilerParams(dimension_semantics=("arbitrary",)),
        name="rel_bias",
    )(rel_table, _band_codes())


def _row(ref, r):
    return ref.at[pl.ds(pl.multiple_of(r * ROW_TILES, ROW_TILES), ROW_TILES), :]


def _mix_kernel(*refs, tile, apply_ln0, mask_first, alpha, n_steps, n_blocks, has_xs_in, finalize):
    (x_ref, mod_ref, ln0g_ref, ln0b_ref, win_ref, bin_ref, bias_ref, sink_ref, convw_ref,
     woa_ref, wob_ref, wo_ref, ln1g_ref, ln1b_ref, wr_ref, br_ref, k0_ref, v0_ref, u0_ref,
     upto_ref, st0_ref, tab0_ref) = refs[:22]
    refs = refs[22 + (1 if has_xs_in else 0):]
    (x1_ref, gate_ref, dest_ref, newk_ref, newv_ref, newu_ref, st_ref, tab_ref, xs_ref,
     kc_ref, vc_ref, ubuf_ref, run_ref, cur_ref, nfree_ref, h2buf, dest_v, dest_s, fin_v, fin_s,
     ssem, dsem, zsem) = refs
    i = pl.program_id(1)
    t = tile
    lin = pl.program_id(0) * pl.num_programs(1) + i
    now = lax.rem(lin, RING)
    src = lax.rem(lin + RING - LAG, RING)
    slot_rows = t * ROW_TILES

    def dest_copy(slot):
        return pltpu.make_async_copy(dest_v.at[slot], dest_s.at[slot], dsem.at[slot])

    def scatter_wait(slot):
        for _ in range(TOP_K):
            pltpu.make_async_copy(h2buf.at[slot], xs_ref.at[pl.ds(0, slot_rows), :], ssem.at[slot]).wait()

    def scatter_row(slot, tok):
        for k in range(TOP_K):
            pltpu.make_async_copy(_row(h2buf.at[slot], tok), _row(xs_ref, dest_s[slot, k, tok]),
                                  ssem.at[slot]).start(priority=k % 2)

    @pl.when(lin == 0)
    def _():
        run_ref[...] = st0_ref[:, 0:1]
        cur_ref[...] = st0_ref[:, 1:2]
        nfree_ref[...] = st0_ref[0:1, 2:3]
        tab_ref[...] = tab0_ref[...]
        for back in range(1, LAG + 1):
            h2buf[RING - back] = jnp.zeros(h2buf.shape[1:], F32)
            for k in range(TOP_K):
                def fill(tok, c, k=k, back=back):
                    dest_s[RING - back, k, tok] = n_blocks * MOE_BLOCK + ((back - 1) * TOP_K + k) * t + tok
                    return c
                lax.fori_loop(0, t, fill, 0)

    @pl.when(lin >= LAG)
    def _():
        dest_copy(src).wait()
        scatter_wait(now)

    @pl.when(i == 0)
    def _():
        kc_ref[...] = k0_ref[0]
        vc_ref[...] = v0_ref[0]
        ubuf_ref[0:SUBLANES, :] = u0_ref[0]

    for tok in range(t):
        scatter_row(src, tok)

    x = x_ref[0]
    if apply_ln0:
        x = _layernorm(x, ln0g_ref[...], ln0b_ref[...])
    mod = mod_ref[0]
    sh1, sc1, g1, sh2, sc2, g2 = [mod[j:j + 1, :] for j in range(6)]
    h = (x * (1.0 + sc1) + sh1).astype(BF16)

    def proj(j0, j1):
        lo, hi = IN_OFFS[j0], IN_OFFS[j1]
        return jnp.dot(h, win_ref[:, lo:hi], preferred_element_type=F32) + bin_ref[:, lo:hi]

    q = proj(0, 1)
    kv = proj(1, 3)
    kfull = jnp.concatenate([kc_ref[...], kv[:, :KV_WIDTH]], axis=0)
    vfull = jnp.concatenate([vc_ref[...], kv[:, KV_WIDTH:]], axis=0)
    kc_ref[...] = kfull[t:, :]
    vc_ref[...] = vfull[t:, :]
    newk_ref[0] = kfull[t:, :]
    newv_ref[0] = vfull[t:, :]
    low = lax.broadcasted_iota(jnp.int32, kfull.shape, 1) < HEAD_DIM
    k_sw = pltpu.roll(kfull, HEAD_DIM, axis=1)
    v_sw = pltpu.roll(vfull, HEAD_DIM, axis=1)
    k_even = [jnp.where(low, kfull, 0.0).astype(BF16), jnp.where(low, k_sw, 0.0).astype(BF16)]
    k_odd = [jnp.where(low, 0.0, k_sw).astype(BF16), jnp.where(low, 0.0, kfull).astype(BF16)]
    v_even = [jnp.where(low, vfull, 0.0).astype(BF16), jnp.where(low, v_sw, 0.0).astype(BF16)]
    v_odd = [jnp.where(low, 0.0, v_sw).astype(BF16), jnp.where(low, 0.0, vfull).astype(BF16)]
    col = lax.broadcasted_iota(jnp.int32, (PAIR_ROWS, PAIR_COLS), 1)
    even = col < BAND
    band_col = jnp.where(even, col, col - BAND)
    first_pair = lax.broadcasted_iota(jnp.int32, (PAIR_ROWS, 1), 0) < CHUNK
    out_low = lax.broadcasted_iota(jnp.int32, (PAIR_ROWS, LANES), 1) < HEAD_DIM
    sink_even = [jnp.where(first_pair, sink_ref[g * GROUP], sink_ref[g * GROUP + 2]) for g in range(N_KV_HEADS)]
    sink_odd = [jnp.where(first_pair, sink_ref[g * GROUP + 1], sink_ref[g * GROUP + 3])
                for g in range(N_KV_HEADS)]
    units = [(c, g) for c in range(t // CHUNK) for g in range(N_KV_HEADS)]
    scores = []
    for c, g in units:
        r0 = c * CHUNK
        ql = jnp.concatenate([q[r0:r0 + CHUNK, (2 * g) * LANES:(2 * g + 1) * LANES],
                              q[r0:r0 + CHUNK, (2 * g + 1) * LANES:(2 * g + 2) * LANES]], axis=0)
        kp = jnp.concatenate([k_even[g][r0:r0 + BAND], k_odd[g][r0:r0 + BAND]], axis=0)
        s = lax.dot_general(ql.astype(BF16), kp, NT_DIMS, preferred_element_type=F32)
        s = s * (HEAD_DIM ** -0.5) + bias_ref[g]
        if mask_first and c < WINDOW // CHUNK:
            s = jnp.where(band_col < jnp.where(i == 0, WINDOW - r0, 0), NEG_INF, s)
        scores.append(s)
    maxes = []
    for (c, g), s in zip(units, scores):
        m_e = jnp.maximum(jnp.max(jnp.where(even, s, -jnp.inf), axis=-1, keepdims=True), sink_even[g])
        m_o = jnp.maximum(jnp.max(jnp.where(even, -jnp.inf, s), axis=-1, keepdims=True), sink_odd[g])
        maxes.append((m_e, m_o))
    exps, scales = [], []
    for (c, g), s, (m_e, m_o) in zip(units, scores, maxes):
        e = jnp.exp(s - jnp.where(even, m_e, m_o))
        d_e = jnp.sum(jnp.where(even, e, 0.0), axis=-1, keepdims=True) + jnp.exp(sink_even[g] - m_e)
        d_o = jnp.sum(jnp.where(even, 0.0, e), axis=-1, keepdims=True) + jnp.exp(sink_odd[g] - m_o)
        exps.append(e.astype(BF16))
        scales.append(jnp.where(out_low, 1.0 / d_e, 1.0 / d_o))
    outs = {}
    for (c, g), e, scale in zip(units, exps, scales):
        r0 = c * CHUNK
        vp = jnp.concatenate([v_even[g][r0:r0 + BAND], v_odd[g][r0:r0 + BAND]], axis=0)
        outs[c, g] = jnp.dot(e, vp, preferred_element_type=F32) * scale
    ya = jnp.concatenate(
        [jnp.concatenate([outs[c, g][half * CHUNK:(half + 1) * CHUNK] for g in range(N_KV_HEADS)
                          for half in range(2)], axis=-1) for c in range(t // CHUNK)], axis=0)

    cb = proj(3, 4)
    u = proj(4, 5) * proj(5, 6)
    ubuf_ref[SUBLANES:t + SUBLANES, :] = u
    cw = convw_ref[...]
    yc = (cw[0:1, :] * ubuf_ref[SUBLANES - 2:t + SUBLANES - 2, :]
          + cw[1:2, :] * ubuf_ref[SUBLANES - 1:t + SUBLANES - 1, :] + cw[2:3, :] * u)
    yb = cb * yc
    tail = ubuf_ref[t:t + SUBLANES, :]
    newu_ref[0] = tail
    ubuf_ref[0:SUBLANES, :] = tail

    a_out = jnp.dot(ya.astype(BF16), woa_ref[...], preferred_element_type=F32)
    b_out = jnp.dot(yb.astype(BF16), wob_ref[...], preferred_element_type=F32)
    mixin = jax.nn.sigmoid(proj(6, 7)) * a_out + jax.nn.sigmoid(proj(7, 8)) * b_out
    mix = jnp.dot(mixin.astype(BF16), wo_ref[...], preferred_element_type=F32)
    x1 = _layernorm(alpha * x + (1.0 + g1) * mix, ln1g_ref[...], ln1b_ref[...])
    x1_ref[0] = x1

    h2 = x1 * (1.0 + sc2) + sh2
    _to_row_tiles(h2buf.at[now], h2, t)
    logits = lax.dot_general(wr_ref[...], h2.astype(BF16), NT_DIMS, preferred_element_type=F32) + br_ref[...]
    eid = lax.broadcasted_iota(jnp.int32, (N_EXPERTS, t), 0)
    vals, ids = [], []
    for _ in range(TOP_K):
        mx = jnp.max(logits, axis=0, keepdims=True)
        sel = jnp.min(jnp.where(logits == mx, eid, N_EXPERTS), axis=0, keepdims=True)
        vals.append(mx)
        ids.append(sel)
        logits = jnp.where(eid == sel, -jnp.inf, logits)
    ex = [jnp.exp(v - vals[0]) for v in vals]
    tot = ex[0] + ex[1] + ex[2] + ex[3]
    gate_ref[0] = jnp.concatenate([e_ / tot for e_ in ex], axis=0)

    per_block = 1.0 / MOE_BLOCK
    hits = [(eid == ids[k]).astype(F32) for k in range(TOP_K)]
    hit = hits[0] + hits[1] + hits[2] + hits[3]
    count = jnp.sum(hit, axis=1, keepdims=True)
    incl = jnp.dot(hit.astype(BF16), upto_ref[...], preferred_element_type=F32)
    run = run_ref[...]
    cur = cur_ref[...]
    run_blk = jnp.floor(run * per_block)
    is_open = (run - run_blk * MOE_BLOCK > 0).astype(F32)
    end = run + count
    blocks_before = jnp.ceil(run * per_block)
    n_new = jnp.ceil(end * per_block) - blocks_before
    er = lax.broadcasted_iota(jnp.int32, (N_EXPERTS, N_EXPERTS), 0)
    ec = lax.broadcasted_iota(jnp.int32, (N_EXPERTS, N_EXPERTS), 1)
    earlier = jnp.dot((ec < er).astype(BF16), jnp.broadcast_to(n_new, (N_EXPERTS, LANES)).astype(BF16),
                      preferred_element_type=F32)[:, 0:1]
    base = nfree_ref[...] + earlier
    r = run + incl - hit
    r_blk = jnp.floor(r * per_block)
    ordinal = r_blk - run_blk
    block = jnp.where(jnp.logical_and(is_open > 0, ordinal == 0), cur, base + ordinal - is_open)
    row = block * MOE_BLOCK + (r - r_blk * MOE_BLOCK)
    dests = [jnp.sum(hits[k] * row, axis=0, keepdims=True).astype(jnp.int32) for k in range(TOP_K)]
    dest_v[now] = jnp.concatenate(dests, axis=0)
    for k in range(TOP_K):
        for gidx in range(t // DEST_GROUP):
            dest_ref[gidx, k:k + 1, :] = dests[k][:, gidx * DEST_GROUP:(gidx + 1) * DEST_GROUP]
    lane = lax.broadcasted_iota(jnp.int32, tab_ref.shape, 1).astype(F32)
    table = tab_ref[...]
    for j in range(-(-t // MOE_BLOCK)):
        table = jnp.where(jnp.logical_and(lane == blocks_before + j, n_new > j), base + j, table)
    tab_ref[...] = table
    cur = jnp.where(n_new > 0, base + n_new - 1, cur)
    nfree = nfree_ref[...] + jnp.sum(n_new, axis=0, keepdims=True)
    run_ref[...] = end
    cur_ref[...] = cur
    nfree_ref[...] = nfree
    st_lane = lax.broadcasted_iota(jnp.int32, st_ref.shape, 1)
    st_ref[...] = jnp.where(st_lane == 0, end, jnp.where(st_lane == 1, cur, nfree))
    dest_copy(now).start()

    @pl.when(lin == n_steps - 1)
    def _():
        for back in range(LAG):
            late = lax.rem(lin + RING - back, RING)
            dest_copy(late).wait()
            lax.fori_loop(0, t, lambda tok, c, late=late: (scatter_row(late, tok), c)[1], 0)
        for slot in range(RING):
            scatter_wait(slot)
        if finalize:
            rem = end - jnp.floor(end * per_block) * MOE_BLOCK
            tail_lo = cur * MOE_BLOCK + rem
            tail_hi = jnp.where(rem > 0, (cur + 1.0) * MOE_BLOCK, tail_lo)
            fin_v[...] = jnp.where(st_lane == 0, tail_lo, jnp.where(st_lane == 1, tail_hi, nfree)).astype(jnp.int32)
            fin = pltpu.make_async_copy(fin_v, fin_s, dsem.at[0])
            fin.start()
            fin.wait()
            h2buf[0] = jnp.zeros(h2buf.shape[1:], F32)

            def zero_row(r_):
                return pltpu.make_async_copy(_row(h2buf.at[0], 0), _row(xs_ref, r_), zsem)

            piece = min(slot_rows, MOE_BLOCK * ROW_TILES)

            def zero_part(c_):
                return pltpu.make_async_copy(
                    h2buf.at[0].at[pl.ds(0, piece), :],
                    xs_ref.at[pl.ds(pl.multiple_of(c_ * piece, piece), piece), :], zsem)

            def per_expert(e, carry):
                lax.fori_loop(fin_s[e, 0], fin_s[e, 1], lambda r_, c: (zero_row(r_).start(), c)[1], 0)
                lax.fori_loop(fin_s[e, 0], fin_s[e, 1], lambda r_, c: (zero_row(r_).wait(), c)[1], 0)
                return carry

            lax.fori_loop(0, N_EXPERTS, per_expert, 0)
            parts = MOE_BLOCK * ROW_TILES // piece
            lo, hi = fin_s[0, 2] * parts, n_blocks * parts
            lax.fori_loop(lo, hi, lambda c_, c: (zero_part(c_).start(), c)[1], 0)
            lax.fori_loop(lo, hi, lambda c_, c: (zero_part(c_).wait(), c)[1], 0)


def _mix(x, mod, ln0_g, ln0_b, w_in, b_in, bias, sink, conv_w, w_oa, w_ob, w_o, ln1_g, ln1_b,
         w_rt, b_r, k0, v0, u0, state, table, xs_in, *, tile, apply_ln0, mask_first, alpha, n_blocks, finalize):
    b, s, d = x.shape
    n_t = s // tile
    assert b * n_t >= RING
    const = lambda shape: pl.BlockSpec(shape, lambda bb, ii: (0,) * len(shape), pipeline_mode=pl.Buffered(1))
    per_b = lambda shape: pl.BlockSpec((1,) + shape, lambda bb, ii: (bb,) + (0,) * len(shape))
    whole = lambda shape: pl.BlockSpec(shape, lambda bb, ii: (0,) * len(shape))
    kern = functools.partial(_mix_kernel, tile=tile, apply_ln0=apply_ln0, mask_first=mask_first, alpha=alpha,
                             n_steps=b * n_t, n_blocks=n_blocks, has_xs_in=xs_in is not None, finalize=finalize)
    upto = (jnp.arange(tile)[:, None] <= jnp.arange(tile)[None, :]).astype(BF16)
    xs_rows = (n_blocks + SPARE_BLOCKS) * MOE_BLOCK * ROW_TILES
    groups = tile // DEST_GROUP
    operands = [x, mod, ln0_g, ln0_b, w_in, b_in, bias, sink, conv_w, w_oa, w_ob, w_o, ln1_g, ln1_b, w_rt, b_r,
                k0, v0, u0, upto, state, table]
    in_specs = [
        pl.BlockSpec((1, tile, d), lambda bb, ii: (bb, ii, 0)),
        per_b((6, d)),
        const((1, d)), const((1, d)),
        const((d, IN_WIDTH)), const((1, IN_WIDTH)),
        const((N_KV_HEADS, PAIR_ROWS, PAIR_COLS)),
        pl.BlockSpec(memory_space=pltpu.SMEM),
        const((CONV_K, CONV_WIDTH)),
        const((ATT_WIDTH, d)), const((CONV_WIDTH, d)), const((d, d)),
        const((1, d)), const((1, d)),
        const((N_EXPERTS, d)), const((N_EXPERTS, 1)),
        per_b((WINDOW, KV_WIDTH)), per_b((WINDOW, KV_WIDTH)), per_b((SUBLANES, CONV_WIDTH)),
        const((tile, tile)), const((N_EXPERTS, LANES)), const((N_EXPERTS, TABLE_LANES)),
    ]
    aliases = {}
    if xs_in is not None:
        operands.append(xs_in)
        in_specs.append(pl.BlockSpec(memory_space=pl.ANY))
        aliases = {len(operands) - 1: 8}
    return pl.pallas_call(
        kern,
        out_shape=(
            jax.ShapeDtypeStruct((b, s, d), F32),
            jax.ShapeDtypeStruct((b * n_t, TOP_K, tile), F32),
            jax.ShapeDtypeStruct((b * s // DEST_GROUP, TOP_K, DEST_GROUP), jnp.int32),
            jax.ShapeDtypeStruct((b, WINDOW, KV_WIDTH), F32),
            jax.ShapeDtypeStruct((b, WINDOW, KV_WIDTH), F32),
            jax.ShapeDtypeStruct((b, SUBLANES, CONV_WIDTH), F32),
            jax.ShapeDtypeStruct((N_EXPERTS, LANES), F32),
            jax.ShapeDtypeStruct((N_EXPERTS, TABLE_LANES), F32),
            jax.ShapeDtypeStruct((xs_rows, LANES), F32),
        ),
        grid=(b, n_t),
        in_specs=in_specs,
        out_specs=(
            pl.BlockSpec((1, tile, d), lambda bb, ii: (bb, ii, 0)),
            pl.BlockSpec((1, TOP_K, tile), lambda bb, ii: (bb * n_t + ii, 0, 0)),
            pl.BlockSpec((groups, TOP_K, DEST_GROUP), lambda bb, ii: (bb * n_t + ii, 0, 0)),
            per_b((WINDOW, KV_WIDTH)), per_b((WINDOW, KV_WIDTH)), per_b((SUBLANES, CONV_WIDTH)),
            whole((N_EXPERTS, LANES)), whole((N_EXPERTS, TABLE_LANES)),
            pl.BlockSpec(memory_space=pl.ANY),
        ),
        scratch_shapes=[
            pltpu.VMEM((WINDOW, KV_WIDTH), F32),
            pltpu.VMEM((WINDOW, KV_WIDTH), F32),
            pltpu.VMEM((tile + SUBLANES, CONV_WIDTH), F32),
            pltpu.VMEM((N_EXPERTS, 1), F32),
            pltpu.VMEM((N_EXPERTS, 1), F32),
            pltpu.VMEM((1, 1), F32),
            pltpu.VMEM((RING, tile * ROW_TILES, LANES), F32),
            pltpu.VMEM((RING, TOP_K, tile), jnp.int32),
            pltpu.SMEM((RING, TOP_K, tile), jnp.int32),
            pltpu.VMEM((N_EXPERTS, LANES), jnp.int32),
            pltpu.SMEM((N_EXPERTS, LANES), jnp.int32),
            pltpu.SemaphoreType.DMA((RING,)),
            pltpu.SemaphoreType.DMA((RING,)),
            pltpu.SemaphoreType.DMA,
        ],
        input_output_aliases=aliases,
        compiler_params=pltpu.CompilerParams(dimension_semantics=("arbitrary", "arbitrary"),
                                             vmem_limit_bytes=_vmem_limit(56)),
        name="mix",
    )(*operands)


def _ffn_kernel(first_ref, nblk_ref, nused_ref, order_ref, xs_ref, wgu_ref, bgu_ref, wd_ref, bd_ref, ys_ref,
                wgu_bf, wd_bf, xbuf, ybuf, in_sem, out_sem, *, n_blocks):
    e = pl.program_id(0)
    first = first_ref[e]
    n_used = nused_ref[0]
    block_rows = MOE_BLOCK * ROW_TILES

    def blk(ref, b):
        return ref.at[pl.ds(pl.multiple_of(b * block_rows, block_rows), block_rows), :]

    def in_copy(b, slot):
        return pltpu.make_async_copy(blk(xs_ref, order_ref[b]), xbuf.at[slot], in_sem.at[slot])

    def out_copy(b, slot):
        return pltpu.make_async_copy(ybuf.at[slot], blk(ys_ref, order_ref[b]), out_sem.at[slot])

    @pl.when(e == 0)
    def _():
        in_copy(0, 0).start()

    @pl.when(nblk_ref[e] > 0)
    def _():
        wgu_bf[...] = wgu_ref[0, 0].astype(BF16)
        wd_bf[...] = wd_ref[0, 0].astype(BF16)

    def body(j, carry):
        b = first + j
        slot = lax.rem(b, 2)
        in_copy(b, slot).wait()

        @pl.when(b + 1 < n_used)
        def _():
            in_copy(b + 1, 1 - slot).start()

        @pl.when(b >= 2)
        def _():
            out_copy(b - 2, slot).wait()

        x = _from_row_tiles(xbuf.at[slot], 0, MOE_BLOCK).astype(BF16)
        gu = jnp.dot(x, wgu_bf[...], preferred_element_type=F32) + bgu_ref[0, 0]
        g = jnp.minimum(gu[:, :D_FF], SWIGLU_LIMIT)
        lin = jnp.clip(gu[:, D_FF:], -SWIGLU_LIMIT, SWIGLU_LIMIT)
        a = g * jax.nn.sigmoid(SWIGLU_ALPHA * g) * (lin + 1.0)
        y = jnp.dot(a.astype(BF16), wd_bf[...], preferred_element_type=F32) + bd_ref[0, 0]
        _to_row_tiles(ybuf.at[slot], y, MOE_BLOCK)
        out_copy(b, slot).start()
        return carry

    lax.fori_loop(0, nblk_ref[e], body, 0)

    @pl.when(e == N_EXPERTS - 1)
    def _():
        @pl.when(n_used >= 2)
        def _():
            out_copy(n_used - 2, lax.rem(n_used, 2)).wait()

        out_copy(n_used - 1, lax.rem(n_used - 1, 2)).wait()
        ybuf[0] = jnp.zeros(ybuf.shape[1:], F32)
        lax.fori_loop(n_used, n_blocks, lambda b, c: (out_copy(b, 0).start(), c)[1], 0)
        lax.fori_loop(n_used, n_blocks, lambda b, c: (out_copy(b, 0).wait(), c)[1], 0)


def _ffn(layer, first_blk, n_blk, n_used, order, xs, w_gu, b_gu, w_down, b_down, n_blocks):
    block_rows = MOE_BLOCK * ROW_TILES
    depth = w_gu.shape[0]

    def expert(e, *_):
        return (layer, e, 0, 0)

    return pl.pallas_call(
        functools.partial(_ffn_kernel, n_blocks=n_blocks),
        out_shape=jax.ShapeDtypeStruct((n_blocks * block_rows, LANES), F32),
        grid_spec=pltpu.PrefetchScalarGridSpec(
            num_scalar_prefetch=4,
            grid=(N_EXPERTS,),
            in_specs=[
                pl.BlockSpec(memory_space=pl.ANY),
                pl.BlockSpec((1, 1, D_MODEL, 2 * D_FF), expert),
                pl.BlockSpec((1, 1, 1, 2 * D_FF), expert),
                pl.BlockSpec((1, 1, D_FF, D_MODEL), expert),
                pl.BlockSpec((1, 1, 1, D_MODEL), expert),
            ],
            out_specs=pl.BlockSpec(memory_space=pl.ANY),
            scratch_shapes=[
                pltpu.VMEM((D_MODEL, 2 * D_FF), BF16),
                pltpu.VMEM((D_FF, D_MODEL), BF16),
                pltpu.VMEM((2, block_rows, LANES), F32),
                pltpu.VMEM((2, block_rows, LANES), F32),
                pltpu.SemaphoreType.DMA((2,)),
                pltpu.SemaphoreType.DMA((2,)),
            ],
        ),
        compiler_params=pltpu.CompilerParams(dimension_semantics=("arbitrary",),
                                             vmem_limit_bytes=_vmem_limit(56)),
        name="ffn",
    )(first_blk, n_blk, n_used, order, xs, w_gu, b_gu.reshape(depth, N_EXPERTS, 1, 2 * D_FF), w_down,
      b_down.reshape(depth, N_EXPERTS, 1, D_MODEL))


def _combine_kernel(dest_ref, next_ref, x1_ref, mod_ref, gate_ref, ln2g_ref, ln2b_ref, ys_ref, o_ref,
                    buf0, buf1, sem0, sem1, *, tile, alpha, n_steps):
    t = tile
    step = pl.program_id(0) * pl.num_programs(1) + pl.program_id(1)

    def fetch(d_ref, g, tt, buf, sem):
        tok = g * DEST_GROUP + tt
        for k in range(TOP_K):
            pltpu.make_async_copy(_row(ys_ref, d_ref[g, k, tt]), _row(buf, k * t + tok),
                                  sem).start(priority=k % 2)

    def wait(buf, sem):
        pltpu.make_async_copy(ys_ref.at[pl.ds(0, TOP_K * t * ROW_TILES), :], buf, sem).wait()

    @pl.when(step == 0)
    def _():
        for g in range(t // DEST_GROUP):
            lax.fori_loop(0, DEST_GROUP, lambda tt, c, g=g: (fetch(dest_ref, g, tt, buf0, sem0), c)[1], 0)

    def phase(buf, sem, nxt_buf, nxt_sem):
        wait(buf, sem)
        for g in range(t // DEST_GROUP):
            for tt in range(DEST_GROUP):
                fetch(next_ref, g, tt, nxt_buf, nxt_sem)
        gates = gate_ref[0]
        gates_t = jnp.transpose(jnp.concatenate([gates, jnp.zeros_like(gates)], axis=0))
        ff = jnp.zeros((t, D_MODEL), F32)
        for k in range(TOP_K):
            ff = ff + gates_t[:, k:k + 1] * _from_row_tiles(buf, k * t, t)
        g2 = mod_ref[0][5:6, :]
        o_ref[0] = _layernorm(alpha * x1_ref[0] + (1.0 + g2) * ff, ln2g_ref[...], ln2b_ref[...])

    @pl.when(step % 2 == 0)
    def _():
        phase(buf0, sem0, buf1, sem1)

    @pl.when(step % 2 == 1)
    def _():
        phase(buf1, sem1, buf0, sem0)

    @pl.when(step == n_steps - 1)
    def _():
        if (n_steps - 1) % 2 == 0:
            wait(buf1, sem1)
        else:
            wait(buf0, sem0)


def _combine(dest, group_offset, x1, mod, gates, ln2_g, ln2_b, ys, *, tile, alpha):
    b, s, d = x1.shape
    n_t = s // tile
    n_steps = b * n_t
    groups = tile // DEST_GROUP
    goff = group_offset // groups
    per_gate_row = gates.shape[2] // tile
    kern = functools.partial(_combine_kernel, tile=tile, alpha=alpha, n_steps=n_steps)
    buf = pltpu.VMEM((TOP_K * tile * ROW_TILES, LANES), F32)
    return pl.pallas_call(
        kern,
        out_shape=jax.ShapeDtypeStruct((b, s, d), F32),
        grid=(b, n_t),
        in_specs=[
            pl.BlockSpec((groups, TOP_K, DEST_GROUP), lambda bb, ii: (goff + bb * n_t + ii, 0, 0),
                         memory_space=pltpu.SMEM),
            pl.BlockSpec((groups, TOP_K, DEST_GROUP),
                         lambda bb, ii: (goff + jnp.minimum(bb * n_t + ii + 1, n_steps - 1), 0, 0),
                         memory_space=pltpu.SMEM),
            pl.BlockSpec((1, tile, d), lambda bb, ii: (bb, ii, 0)),
            pl.BlockSpec((1, 6, d), lambda bb, ii: (bb, 0, 0)),
            pl.BlockSpec((1, TOP_K, tile), lambda bb, ii: ((bb * n_t + ii) // per_gate_row, 0,
                                                           (bb * n_t + ii) % per_gate_row)),
            pl.BlockSpec((1, d), lambda bb, ii: (0, 0)),
            pl.BlockSpec((1, d), lambda bb, ii: (0, 0)),
            pl.BlockSpec(memory_space=pl.ANY),
        ],
        out_specs=pl.BlockSpec((1, tile, d), lambda bb, ii: (bb, ii, 0)),
        scratch_shapes=[buf, buf, pltpu.SemaphoreType.DMA, pltpu.SemaphoreType.DMA],
        compiler_params=pltpu.CompilerParams(dimension_semantics=("arbitrary", "arbitrary"),
                                             vmem_limit_bytes=_vmem_limit(32)),
        name="combine",
    )(dest, dest, x1, mod, gates, ln2_g, ln2_b, ys)


def kernel(x_prompt, x_sample, c_prompt, c_sample, cache_k, cache_v, state_conv, rel_table, ln0_g, ln0_b, w_ada, b_ada, w_in, b_in, sinks, conv_w, w_oa, w_ob, w_o, ln1_g, ln1_b, w_router, b_router, w_gu, b_gu, w_down, b_down, ln2_g, ln2_b):
    depth = w_ada.shape[0]
    bp, sp, d = x_prompt.shape
    bs, ss, _ = x_sample.shape
    alpha = (2 * depth) ** 0.25
    n_tok = bp * sp + bs * ss
    assert sp % PROMPT_TILE == 0 and ss % DEST_GROUP == 0 and ss <= WINDOW
    assert -(-n_tok // MOE_BLOCK) <= TABLE_LANES
    n_blocks = -(-(n_tok * TOP_K) // MOE_BLOCK) + N_EXPERTS

    mod_all = _ada(jnp.concatenate([c_prompt, c_sample], axis=0), w_ada, b_ada)
    mod_all = mod_all.reshape(depth, bp + bs, 6, d)
    bias = _bias_table(rel_table)
    row = lambda a: a.reshape(1, -1)
    zeros_kv = jnp.zeros((bp, WINDOW, KV_WIDTH), F32)
    zeros_u = jnp.zeros((bp, SUBLANES, CONV_WIDTH), F32)
    state0 = jnp.zeros((N_EXPERTS, LANES), F32)
    table0 = jnp.zeros((N_EXPERTS, TABLE_LANES), F32)
    slot = jnp.arange(n_blocks, dtype=jnp.int32)

    y_p, y_s = x_prompt, x_sample
    outs = {name: [] for name in ("kp", "vp", "up", "ks", "vs", "us")}
    for l in range(depth):
        shared = (w_in[l].astype(BF16), row(b_in[l]))
        tail = (sinks[l], conv_w[l], w_oa[l].astype(BF16), w_ob[l].astype(BF16), w_o[l].astype(BF16),
                row(ln1_g[l]), row(ln1_b[l]), w_router[l].T.astype(BF16), b_router[l].reshape(N_EXPERTS, 1))
        mod_p, mod_s = mod_all[l, :bp], mod_all[l, bp:]
        x1_p, gate_p, dest_p, k_p, v_p, u_p, state, table, xs = _mix(
            y_p, mod_p, row(ln0_g), row(ln0_b), *shared, bias, *tail, zeros_kv, zeros_kv, zeros_u,
            state0, table0, None, tile=PROMPT_TILE, apply_ln0=(l == 0), mask_first=True, alpha=alpha,
            n_blocks=n_blocks, finalize=True)
        u0 = jnp.pad(state_conv[l], ((0, 0), (SUBLANES - (CONV_K - 1), 0), (0, 0)))
        x1_s, gate_s, dest_s, k_s, v_s, u_s, state, table, xs = _mix(
            y_s, mod_s, row(ln0_g), row(ln0_b), *shared, bias, *tail,
            cache_k[l].reshape(bs, WINDOW, KV_WIDTH), cache_v[l].reshape(bs, WINDOW, KV_WIDTH), u0,
            state, table, xs, tile=ss, apply_ln0=(l == 0), mask_first=False, alpha=alpha,
            n_blocks=n_blocks, finalize=False)

        counts = state[:, 0].astype(jnp.int32)
        n_used = state[0:1, 2].astype(jnp.int32)
        n_blk = (counts + MOE_BLOCK - 1) // MOE_BLOCK
        blk_end = jnp.cumsum(n_blk)
        first_blk = blk_end - n_blk
        owner = jnp.minimum(jnp.sum((blk_end[None, :] <= slot[:, None]).astype(jnp.int32), axis=1), N_EXPERTS - 1)
        nth = jnp.clip(slot - first_blk[owner], 0, TABLE_LANES - 1)
        order = jnp.where(slot < n_used, table[owner, nth].astype(jnp.int32), slot)

        ys = _ffn(l, first_blk, n_blk, n_used, order, xs, w_gu, b_gu, w_down, b_down, n_blocks)
        y_p = _combine(dest_p, 0, x1_p, mod_p, gate_p, row(ln2_g[l]), row(ln2_b[l]), ys,
                       tile=COMBINE_TILE, alpha=alpha)
        y_s = _combine(dest_s, 0, x1_s, mod_s, gate_s, row(ln2_g[l]), row(ln2_b[l]), ys,
                       tile=ss, alpha=alpha)

        outs["kp"].append(k_p.reshape(bp, WINDOW, N_KV_HEADS, HEAD_DIM))
        outs["vp"].append(v_p.reshape(bp, WINDOW, N_KV_HEADS, HEAD_DIM))
        outs["up"].append(u_p[:, SUBLANES - (CONV_K - 1):, :])
        outs["ks"].append(k_s.reshape(bs, WINDOW, N_KV_HEADS, HEAD_DIM))
        outs["vs"].append(v_s.reshape(bs, WINDOW, N_KV_HEADS, HEAD_DIM))
        outs["us"].append(u_s[:, SUBLANES - (CONV_K - 1):, :])
    return (y_p, y_s, jnp.stack(outs["kp"]), jnp.stack(outs["vp"]), jnp.stack(outs["up"]),
            jnp.stack(outs["ks"]), jnp.stack(outs["vs"]), jnp.stack(outs["us"]))
```

```python
import functools
import math

import jax
import jax.numpy as jnp
import numpy as np
from jax import lax
from jax.experimental import pallas as pl
from jax.experimental.pallas import tpu as pltpu

D_MODEL = 1024
CHUNK = 64
N_HEADS = 8
N_KV_HEADS = 2
HEAD_DIM = 64
GROUP = N_HEADS // N_KV_HEADS
ATT_WIDTH = N_HEADS * HEAD_DIM
KV_WIDTH = N_KV_HEADS * HEAD_DIM
WINDOW = 128
CONV_WIDTH = 512
CONV_K = 3
NUM_BUCKETS = 32
MAX_DISTANCE = 128
N_EXPERTS = 32
TOP_K = 4
D_FF = 1024
SWIGLU_LIMIT = 7.0
SWIGLU_ALPHA = 1.702
MOE_BLOCK = 512
LN_EPS = 1e-5
NEG_INF = -1e30
IN_SIZES = (ATT_WIDTH, KV_WIDTH, KV_WIDTH, CONV_WIDTH, CONV_WIDTH, CONV_WIDTH, D_MODEL, D_MODEL)
IN_WIDTH = sum(IN_SIZES)
IN_OFFS = tuple(int(s) for s in np.cumsum((0,) + IN_SIZES))

SUBLANES = 8
LANES = 128
ROW_TILES = D_MODEL // LANES
assert ROW_TILES == SUBLANES

PROMPT_TILE = 512
COMBINE_TILE = 256
DEST_GROUP = 64
N_PACED = 16
LAG = 2
RING = 2 * LAG
SPARE_BLOCKS = LAG * PROMPT_TILE * TOP_K // MOE_BLOCK
TABLE_LANES = 256

F32 = jnp.float32
BF16 = jnp.bfloat16
HIGHEST = lax.Precision.HIGHEST
NT_DIMS = (((1,), (1,)), ((), ()))


def _vmem_limit(mib):
    return mib * 1024 * 1024


def _layernorm(x, g, b):
    mu = jnp.mean(x, axis=-1, keepdims=True)
    xc = x - mu
    var = jnp.mean(xc * xc, axis=-1, keepdims=True)
    return xc * lax.rsqrt(var + LN_EPS) * g + b


def _to_row_tiles(ref, x, rows):
    for s in range(ROW_TILES):
        ref[pl.ds(s, rows, stride=ROW_TILES), :] = x[:, s * LANES:(s + 1) * LANES]


def _from_row_tiles(ref, base, rows):
    return jnp.concatenate(
        [ref[pl.ds(base * ROW_TILES + s, rows, stride=ROW_TILES), :] for s in range(ROW_TILES)], axis=-1)


def _ada_kernel(c_ref, w_ref, b_ref, o_ref):
    c = c_ref[...]
    s = c * jax.nn.sigmoid(c)
    o_ref[0] = jnp.dot(s, w_ref[0], precision=HIGHEST, preferred_element_type=F32) + b_ref[0]


def _ada(c_all, w_ada, b_ada):
    depth = w_ada.shape[0]
    nb = c_all.shape[0]
    n_col = 6 * D_MODEL // D_MODEL
    return pl.pallas_call(
        _ada_kernel,
        out_shape=jax.ShapeDtypeStruct((depth, nb, 6 * D_MODEL), F32),
        grid=(depth, n_col),
        in_specs=[
            pl.BlockSpec((nb, D_MODEL), lambda l, j: (0, 0)),
            pl.BlockSpec((1, D_MODEL, D_MODEL), lambda l, j: (l, 0, j)),
            pl.BlockSpec((1, 1, D_MODEL), lambda l, j: (l, 0, j)),
        ],
        out_specs=pl.BlockSpec((1, nb, D_MODEL), lambda l, j: (l, 0, j)),
        compiler_params=pltpu.CompilerParams(dimension_semantics=("arbitrary", "arbitrary"),
                                             vmem_limit_bytes=_vmem_limit(32)),
        name="ada",
    )(c_all, w_ada, b_ada.reshape(depth, 1, 6 * D_MODEL))


def _rel_bucket(rel):
    half = NUM_BUCKETS // 2
    max_exact = half // 2
    n = jnp.abs(rel)
    n_f = jnp.maximum(n, 1).astype(jnp.float32)
    large = max_exact + (jnp.log(n_f / max_exact) / math.log(MAX_DISTANCE / max_exact)
                         * (half - max_exact)).astype(jnp.int32)
    large = jnp.minimum(large, half - 1)
    return jnp.where(rel > 0, half, 0) + jnp.where(n < max_exact, n, large)


BAND = WINDOW + CHUNK
PAIR_ROWS = 2 * CHUNK
PAIR_COLS = 2 * BAND


def _band_codes():
    r = jnp.arange(PAIR_ROWS)[:, None]
    j = jnp.arange(PAIR_COLS)[None, :]
    bucket = _rel_bucket(j % BAND - WINDOW - r % CHUNK)
    head = 2 * (r // CHUNK) + j // BAND
    return (bucket + NUM_BUCKETS * head).astype(jnp.int32)


def _bias_kernel(table_ref, code_ref, o_ref):
    g = pl.program_id(0)
    code = code_ref[...]
    acc = jnp.zeros(code.shape, F32)
    for hq in range(GROUP):
        for i in range(NUM_BUCKETS):
            acc = jnp.where(code == hq * NUM_BUCKETS + i, table_ref[i, g * GROUP + hq], acc)
    o_ref[0] = acc


def _bias_table(rel_table):
    return pl.pallas_call(
        _bias_kernel,
        out_shape=jax.ShapeDtypeStruct((N_KV_HEADS, PAIR_ROWS, PAIR_COLS), F32),
        grid=(N_KV_HEADS,),
        in_specs=[
            pl.BlockSpec(memory_space=pltpu.SMEM),
            pl.BlockSpec((PAIR_ROWS, PAIR_COLS), lambda g: (0, 0)),
        ],
        out_specs=pl.BlockSpec((1, PAIR_ROWS, PAIR_COLS), lambda g: (g, 0, 0)),
        compiler_params=pltpu.CompilerParams(dimension_semantics=("arbitrary",)),
        name="rel_bias",
    )(rel_table, _band_codes())


def _row(ref, r):
    return ref.at[pl.ds(pl.multiple_of(r * ROW_TILES, ROW_TILES), ROW_TILES), :]


def _mix_kernel(*refs, tile, apply_ln0, mask_first, alpha, n_steps, n_blocks, has_xs_in, finalize):
    (x_ref, mod_ref, ln0g_ref, ln0b_ref, win_ref, bin_ref, bias_ref, sink_ref, convw_ref,
     woa_ref, wob_ref, wo_ref, ln1g_ref, ln1b_ref, wr_ref, br_ref, k0_ref, v0_ref, u0_ref,
     upto_ref, st0_ref, tab0_ref) = refs[:22]
    refs = refs[22 + (1 if has_xs_in else 0):]
    (x1_ref, gate_ref, dest_ref, newk_ref, newv_ref, newu_ref, st_ref, tab_ref, xs_ref,
     kc_ref, vc_ref, ubuf_ref, run_ref, cur_ref, nfree_ref, h2buf, dest_v, dest_s, fin_v, fin_s,
     ssem, dsem, zsem) = refs
    i = pl.program_id(1)
    t = tile
    lin = pl.program_id(0) * pl.num_programs(1) + i
    now = lax.rem(lin, RING)
    src = lax.rem(lin + RING - LAG, RING)
    slot_rows = t * ROW_TILES

    def dest_copy(slot):
        return pltpu.make_async_copy(dest_v.at[slot], dest_s.at[slot], dsem.at[slot])

    def scatter_wait(slot):
        for _ in range(TOP_K):
            pltpu.make_async_copy(h2buf.at[slot], xs_ref.at[pl.ds(0, slot_rows), :], ssem.at[slot]).wait()

    def scatter_row(slot, tok, offset=0):
        for k in range(TOP_K):
            pltpu.make_async_copy(_row(h2buf.at[slot], tok), _row(xs_ref, dest_s[slot, k, tok] + offset),
                                  ssem.at[slot]).start(priority=k % 2)

    paced = iter(range(N_PACED))

    def dispatch_after(value):
        j = next(paced)
        bits = pltpu.bitcast(value[0:SUBLANES, 0:LANES], jnp.int32)
        zero = lax.shift_right_logical(lax.shift_right_logical(bits, 16), 16)[0, 0]
        for tok in range(j * t // N_PACED, (j + 1) * t // N_PACED):
            scatter_row(src, tok, zero)

    @pl.when(lin == 0)
    def _():
        run_ref[...] = st0_ref[:, 0:1]
        cur_ref[...] = st0_ref[:, 1:2]
        nfree_ref[...] = st0_ref[0:1, 2:3]
        tab_ref[...] = tab0_ref[...]
        for back in range(1, LAG + 1):
            h2buf[RING - back] = jnp.zeros(h2buf.shape[1:], F32)
            for k in range(TOP_K):
                def fill(tok, c, k=k, back=back):
                    dest_s[RING - back, k, tok] = n_blocks * MOE_BLOCK + ((back - 1) * TOP_K + k) * t + tok
                    return c
                lax.fori_loop(0, t, fill, 0)

    @pl.when(lin >= LAG)
    def _():
        dest_copy(src).wait()
        scatter_wait(now)

    @pl.when(i == 0)
    def _():
        kc_ref[...] = k0_ref[0]
        vc_ref[...] = v0_ref[0]
        ubuf_ref[0:SUBLANES, :] = u0_ref[0]

    x = x_ref[0]
    if apply_ln0:
        x = _layernorm(x, ln0g_ref[...], ln0b_ref[...])
    dispatch_after(x)
    mod = mod_ref[0]
    sh1, sc1, g1, sh2, sc2, g2 = [mod[j:j + 1, :] for j in range(6)]
    h = (x * (1.0 + sc1) + sh1).astype(BF16)

    def proj(j0, j1):
        lo, hi = IN_OFFS[j0], IN_OFFS[j1]
        return jnp.dot(h, win_ref[:, lo:hi], preferred_element_type=F32) + bin_ref[:, lo:hi]

    q = proj(0, 1)
    dispatch_after(q)
    kv = proj(1, 3)
    dispatch_after(kv)
    kfull = jnp.concatenate([kc_ref[...], kv[:, :KV_WIDTH]], axis=0)
    vfull = jnp.concatenate([vc_ref[...], kv[:, KV_WIDTH:]], axis=0)
    kc_ref[...] = kfull[t:, :]
    vc_ref[...] = vfull[t:, :]
    newk_ref[0] = kfull[t:, :]
    newv_ref[0] = vfull[t:, :]
    low = lax.broadcasted_iota(jnp.int32, kfull.shape, 1) < HEAD_DIM
    k_sw = pltpu.roll(kfull, HEAD_DIM, axis=1)
    v_sw = pltpu.roll(vfull, HEAD_DIM, axis=1)
    k_even = [jnp.where(low, kfull, 0.0).astype(BF16), jnp.where(low, k_sw, 0.0).astype(BF16)]
    k_odd = [jnp.where(low, 0.0, k_sw).astype(BF16), jnp.where(low, 0.0, kfull).astype(BF16)]
    v_even = [jnp.where(low, vfull, 0.0).astype(BF16), jnp.where(low, v_sw, 0.0).astype(BF16)]
    v_odd = [jnp.where(low, 0.0, v_sw).astype(BF16), jnp.where(low, 0.0, vfull).astype(BF16)]
    col = lax.broadcasted_iota(jnp.int32, (PAIR_ROWS, PAIR_COLS), 1)
    even = col < BAND
    band_col = jnp.where(even, col, col - BAND)
    first_pair = lax.broadcasted_iota(jnp.int32, (PAIR_ROWS, 1), 0) < CHUNK
    out_low = lax.broadcasted_iota(jnp.int32, (PAIR_ROWS, LANES), 1) < HEAD_DIM
    sink_even = [jnp.where(first_pair, sink_ref[g * GROUP], sink_ref[g * GROUP + 2]) for g in range(N_KV_HEADS)]
    sink_odd = [jnp.where(first_pair, sink_ref[g * GROUP + 1], sink_ref[g * GROUP + 3])
                for g in range(N_KV_HEADS)]
    units = [(c, g) for c in range(t // CHUNK) for g in range(N_KV_HEADS)]
    scores = []
    for c, g in units:
        r0 = c * CHUNK
        ql = jnp.concatenate([q[r0:r0 + CHUNK, (2 * g) * LANES:(2 * g + 1) * LANES],
                              q[r0:r0 + CHUNK, (2 * g + 1) * LANES:(2 * g + 2) * LANES]], axis=0)
        kp = jnp.concatenate([k_even[g][r0:r0 + BAND], k_odd[g][r0:r0 + BAND]], axis=0)
        s = lax.dot_general(ql.astype(BF16), kp, NT_DIMS, preferred_element_type=F32)
        s = s * (HEAD_DIM ** -0.5) + bias_ref[g]
        if mask_first and c < WINDOW // CHUNK:
            s = jnp.where(band_col < jnp.where(i == 0, WINDOW - r0, 0), NEG_INF, s)
        scores.append(s)
    for s in (scores[0], scores[len(units) // 2], scores[-1]):
        dispatch_after(s)
    maxes = []
    for (c, g), s in zip(units, scores):
        m_e = jnp.maximum(jnp.max(jnp.where(even, s, -jnp.inf), axis=-1, keepdims=True), sink_even[g])
        m_o = jnp.maximum(jnp.max(jnp.where(even, -jnp.inf, s), axis=-1, keepdims=True), sink_odd[g])
        maxes.append((m_e, m_o))
    exps, scales = [], []
    for (c, g), s, (m_e, m_o) in zip(units, scores, maxes):
        e = jnp.exp(s - jnp.where(even, m_e, m_o))
        d_e = jnp.sum(jnp.where(even, e, 0.0), axis=-1, keepdims=True) + jnp.exp(sink_even[g] - m_e)
        d_o = jnp.sum(jnp.where(even, 0.0, e), axis=-1, keepdims=True) + jnp.exp(sink_odd[g] - m_o)
        exps.append(e.astype(BF16))
        scales.append(jnp.where(out_low, 1.0 / d_e, 1.0 / d_o))
    dispatch_after(scales[0])
    dispatch_after(scales[-1])
    outs = {}
    for (c, g), e, scale in zip(units, exps, scales):
        r0 = c * CHUNK
        vp = jnp.concatenate([v_even[g][r0:r0 + BAND], v_odd[g][r0:r0 + BAND]], axis=0)
        outs[c, g] = jnp.dot(e, vp, preferred_element_type=F32) * scale
    dispatch_after(outs[units[0]])
    dispatch_after(outs[units[-1]])
    ya = jnp.concatenate(
        [jnp.concatenate([outs[c, g][half * CHUNK:(half + 1) * CHUNK] for g in range(N_KV_HEADS)
                          for half in range(2)], axis=-1) for c in range(t // CHUNK)], axis=0)
    dispatch_after(ya)

    cb = proj(3, 4)
    u = proj(4, 5) * proj(5, 6)
    ubuf_ref[SUBLANES:t + SUBLANES, :] = u
    cw = convw_ref[...]
    yc = (cw[0:1, :] * ubuf_ref[SUBLANES - 2:t + SUBLANES - 2, :]
          + cw[1:2, :] * ubuf_ref[SUBLANES - 1:t + SUBLANES - 1, :] + cw[2:3, :] * u)
    yb = cb * yc
    tail = ubuf_ref[t:t + SUBLANES, :]
    newu_ref[0] = tail
    ubuf_ref[0:SUBLANES, :] = tail

    a_out = jnp.dot(ya.astype(BF16), woa_ref[...], preferred_element_type=F32)
    dispatch_after(a_out)
    b_out = jnp.dot(yb.astype(BF16), wob_ref[...], preferred_element_type=F32)
    dispatch_after(b_out)
    mixin = jax.nn.sigmoid(proj(6, 7)) * a_out + jax.nn.sigmoid(proj(7, 8)) * b_out
    dispatch_after(mixin)
    mix = jnp.dot(mixin.astype(BF16), wo_ref[...], preferred_element_type=F32)
    dispatch_after(mix)
    x1 = _layernorm(alpha * x + (1.0 + g1) * mix, ln1g_ref[...], ln1b_ref[...])
    x1_ref[0] = x1
    dispatch_after(x1)
    assert next(paced, None) is None

    h2 = x1 * (1.0 + sc2) + sh2
    _to_row_tiles(h2buf.at[now], h2, t)
    logits = lax.dot_general(wr_ref[...], h2.astype(BF16), NT_DIMS, preferred_element_type=F32) + br_ref[...]
    eid = lax.broadcasted_iota(jnp.int32, (N_EXPERTS, t), 0)
    vals, ids = [], []
    for _ in range(TOP_K):
        mx = jnp.max(logits, axis=0, keepdims=True)
        sel = jnp.min(jnp.where(logits == mx, eid, N_EXPERTS), axis=0, keepdims=True)
        vals.append(mx)
        ids.append(sel)
        logits = jnp.where(eid == sel, -jnp.inf, logits)
    ex = [jnp.exp(v - vals[0]) for v in vals]
    tot = ex[0] + ex[1] + ex[2] + ex[3]
    gate_ref[0] = jnp.concatenate([e_ / tot for e_ in ex], axis=0)

    per_block = 1.0 / MOE_BLOCK
    hits = [(eid == ids[k]).astype(F32) for k in range(TOP_K)]
    hit = hits[0] + hits[1] + hits[2] + hits[3]
    count = jnp.sum(hit, axis=1, keepdims=True)
    incl = jnp.dot(hit.astype(BF16), upto_ref[...], preferred_element_type=F32)
    run = run_ref[...]
    cur = cur_ref[...]
    run_blk = jnp.floor(run * per_block)
    is_open = (run - run_blk * MOE_BLOCK > 0).astype(F32)
    end = run + count
    blocks_before = jnp.ceil(run * per_block)
    n_new = jnp.ceil(end * per_block) - blocks_before
    er = lax.broadcasted_iota(jnp.int32, (N_EXPERTS, N_EXPERTS), 0)
    ec = lax.broadcasted_iota(jnp.int32, (N_EXPERTS, N_EXPERTS), 1)
    earlier = jnp.dot((ec < er).astype(BF16), jnp.broadcast_to(n_new, (N_EXPERTS, LANES)).astype(BF16),
                      preferred_element_type=F32)[:, 0:1]
    base = nfree_ref[...] + earlier
    r = run + incl - hit
    r_blk = jnp.floor(r * per_block)
    ordinal = r_blk - run_blk
    block = jnp.where(jnp.logical_and(is_open > 0, ordinal == 0), cur, base + ordinal - is_open)
    row = block * MOE_BLOCK + (r - r_blk * MOE_BLOCK)
    dests = [jnp.sum(hits[k] * row, axis=0, keepdims=True).astype(jnp.int32) for k in range(TOP_K)]
    dest_v[now] = jnp.concatenate(dests, axis=0)
    for k in range(TOP_K):
        for gidx in range(t // DEST_GROUP):
            dest_ref[gidx, k:k + 1, :] = dests[k][:, gidx * DEST_GROUP:(gidx + 1) * DEST_GROUP]
    lane = lax.broadcasted_iota(jnp.int32, tab_ref.shape, 1).astype(F32)
    table = tab_ref[...]
    for j in range(-(-t // MOE_BLOCK)):
        table = jnp.where(jnp.logical_and(lane == blocks_before + j, n_new > j), base + j, table)
    tab_ref[...] = table
    cur = jnp.where(n_new > 0, base + n_new - 1, cur)
    nfree = nfree_ref[...] + jnp.sum(n_new, axis=0, keepdims=True)
    run_ref[...] = end
    cur_ref[...] = cur
    nfree_ref[...] = nfree
    st_lane = lax.broadcasted_iota(jnp.int32, st_ref.shape, 1)
    st_ref[...] = jnp.where(st_lane == 0, end, jnp.where(st_lane == 1, cur, nfree))
    dest_copy(now).start()

    @pl.when(lin == n_steps - 1)
    def _():
        for back in range(LAG):
            late = lax.rem(lin + RING - back, RING)
            dest_copy(late).wait()
            lax.fori_loop(0, t, lambda tok, c, late=late: (scatter_row(late, tok), c)[1], 0)
        for slot in range(RING):
            scatter_wait(slot)
        if finalize:
            rem = end - jnp.floor(end * per_block) * MOE_BLOCK
            tail_lo = cur * MOE_BLOCK + rem
            tail_hi = jnp.where(rem > 0, (cur + 1.0) * MOE_BLOCK, tail_lo)
            fin_v[...] = jnp.where(st_lane == 0, tail_lo, jnp.where(st_lane == 1, tail_hi, nfree)).astype(jnp.int32)
            fin = pltpu.make_async_copy(fin_v, fin_s, dsem.at[0])
            fin.start()
            fin.wait()
            h2buf[0] = jnp.zeros(h2buf.shape[1:], F32)

            def zero_row(r_):
                return pltpu.make_async_copy(_row(h2buf.at[0], 0), _row(xs_ref, r_), zsem)

            piece = min(slot_rows, MOE_BLOCK * ROW_TILES)

            def zero_part(c_):
                return pltpu.make_async_copy(
                    h2buf.at[0].at[pl.ds(0, piece), :],
                    xs_ref.at[pl.ds(pl.multiple_of(c_ * piece, piece), piece), :], zsem)

            def per_expert(e, carry):
                lax.fori_loop(fin_s[e, 0], fin_s[e, 1], lambda r_, c: (zero_row(r_).start(), c)[1], 0)
                lax.fori_loop(fin_s[e, 0], fin_s[e, 1], lambda r_, c: (zero_row(r_).wait(), c)[1], 0)
                return carry

            lax.fori_loop(0, N_EXPERTS, per_expert, 0)
            parts = MOE_BLOCK * ROW_TILES // piece
            lo, hi = fin_s[0, 2] * parts, n_blocks * parts
            lax.fori_loop(lo, hi, lambda c_, c: (zero_part(c_).start(), c)[1], 0)
            lax.fori_loop(lo, hi, lambda c_, c: (zero_part(c_).wait(), c)[1], 0)


def _mix(x, mod, ln0_g, ln0_b, w_in, b_in, bias, sink, conv_w, w_oa, w_ob, w_o, ln1_g, ln1_b,
         w_rt, b_r, k0, v0, u0, state, table, xs_in, *, tile, apply_ln0, mask_first, alpha, n_blocks, finalize):
    b, s, d = x.shape
    n_t = s // tile
    assert b * n_t >= RING
    const = lambda shape: pl.BlockSpec(shape, lambda bb, ii: (0,) * len(shape), pipeline_mode=pl.Buffered(1))
    per_b = lambda shape: pl.BlockSpec((1,) + shape, lambda bb, ii: (bb,) + (0,) * len(shape))
    whole = lambda shape: pl.BlockSpec(shape, lambda bb, ii: (0,) * len(shape))
    kern = functools.partial(_mix_kernel, tile=tile, apply_ln0=apply_ln0, mask_first=mask_first, alpha=alpha,
                             n_steps=b * n_t, n_blocks=n_blocks, has_xs_in=xs_in is not None, finalize=finalize)
    upto = (jnp.arange(tile)[:, None] <= jnp.arange(tile)[None, :]).astype(BF16)
    xs_rows = (n_blocks + SPARE_BLOCKS) * MOE_BLOCK * ROW_TILES
    groups = tile // DEST_GROUP
    operands = [x, mod, ln0_g, ln0_b, w_in, b_in, bias, sink, conv_w, w_oa, w_ob, w_o, ln1_g, ln1_b, w_rt, b_r,
                k0, v0, u0, upto, state, table]
    in_specs = [
        pl.BlockSpec((1, tile, d), lambda bb, ii: (bb, ii, 0)),
        per_b((6, d)),
        const((1, d)), const((1, d)),
        const((d, IN_WIDTH)), const((1, IN_WIDTH)),
        const((N_KV_HEADS, PAIR_ROWS, PAIR_COLS)),
        pl.BlockSpec(memory_space=pltpu.SMEM),
        const((CONV_K, CONV_WIDTH)),
        const((ATT_WIDTH, d)), const((CONV_WIDTH, d)), const((d, d)),
        const((1, d)), const((1, d)),
        const((N_EXPERTS, d)), const((N_EXPERTS, 1)),
        per_b((WINDOW, KV_WIDTH)), per_b((WINDOW, KV_WIDTH)), per_b((SUBLANES, CONV_WIDTH)),
        const((tile, tile)), const((N_EXPERTS, LANES)), const((N_EXPERTS, TABLE_LANES)),
    ]
    aliases = {}
    if xs_in is not None:
        operands.append(xs_in)
        in_specs.append(pl.BlockSpec(memory_space=pl.ANY))
        aliases = {len(operands) - 1: 8}
    return pl.pallas_call(
        kern,
        out_shape=(
            jax.ShapeDtypeStruct((b, s, d), F32),
            jax.ShapeDtypeStruct((b * n_t, TOP_K, tile), F32),
            jax.ShapeDtypeStruct((b * s // DEST_GROUP, TOP_K, DEST_GROUP), jnp.int32),
            jax.ShapeDtypeStruct((b, WINDOW, KV_WIDTH), F32),
            jax.ShapeDtypeStruct((b, WINDOW, KV_WIDTH), F32),
            jax.ShapeDtypeStruct((b, SUBLANES, CONV_WIDTH), F32),
            jax.ShapeDtypeStruct((N_EXPERTS, LANES), F32),
            jax.ShapeDtypeStruct((N_EXPERTS, TABLE_LANES), F32),
            jax.ShapeDtypeStruct((xs_rows, LANES), F32),
        ),
        grid=(b, n_t),
        in_specs=in_specs,
        out_specs=(
            pl.BlockSpec((1, tile, d), lambda bb, ii: (bb, ii, 0)),
            pl.BlockSpec((1, TOP_K, tile), lambda bb, ii: (bb * n_t + ii, 0, 0)),
            pl.BlockSpec((groups, TOP_K, DEST_GROUP), lambda bb, ii: (bb * n_t + ii, 0, 0)),
            per_b((WINDOW, KV_WIDTH)), per_b((WINDOW, KV_WIDTH)), per_b((SUBLANES, CONV_WIDTH)),
            whole((N_EXPERTS, LANES)), whole((N_EXPERTS, TABLE_LANES)),
            pl.BlockSpec(memory_space=pl.ANY),
        ),
        scratch_shapes=[
            pltpu.VMEM((WINDOW, KV_WIDTH), F32),
            pltpu.VMEM((WINDOW, KV_WIDTH), F32),
            pltpu.VMEM((tile + SUBLANES, CONV_WIDTH), F32),
            pltpu.VMEM((N_EXPERTS, 1), F32),
            pltpu.VMEM((N_EXPERTS, 1), F32),
            pltpu.VMEM((1, 1), F32),
            pltpu.VMEM((RING, tile * ROW_TILES, LANES), F32),
            pltpu.VMEM((RING, TOP_K, tile), jnp.int32),
            pltpu.SMEM((RING, TOP_K, tile), jnp.int32),
            pltpu.VMEM((N_EXPERTS, LANES), jnp.int32),
            pltpu.SMEM((N_EXPERTS, LANES), jnp.int32),
            pltpu.SemaphoreType.DMA((RING,)),
            pltpu.SemaphoreType.DMA((RING,)),
            pltpu.SemaphoreType.DMA,
        ],
        input_output_aliases=aliases,
        compiler_params=pltpu.CompilerParams(dimension_semantics=("arbitrary", "arbitrary"),
                                             vmem_limit_bytes=_vmem_limit(56)),
        name="mix",
    )(*operands)


def _ffn_kernel(first_ref, nblk_ref, nused_ref, order_ref, xs_ref, wgu_ref, bgu_ref, wd_ref, bd_ref, ys_ref,
                wgu_bf, wd_bf, xbuf, ybuf, in_sem, out_sem, *, n_blocks):
    e = pl.program_id(0)
    first = first_ref[e]
    n_used = nused_ref[0]
    block_rows = MOE_BLOCK * ROW_TILES

    def blk(ref, b):
        return ref.at[pl.ds(pl.multiple_of(b * block_rows, block_rows), block_rows), :]

    def in_copy(b, slot):
        return pltpu.make_async_copy(blk(xs_ref, order_ref[b]), xbuf.at[slot], in_sem.at[slot])

    def out_copy(b, slot):
        return pltpu.make_async_copy(ybuf.at[slot], blk(ys_ref, order_ref[b]), out_sem.at[slot])

    @pl.when(e == 0)
    def _():
        in_copy(0, 0).start()

    @pl.when(nblk_ref[e] > 0)
    def _():
        wgu_bf[...] = wgu_ref[0, 0].astype(BF16)
        wd_bf[...] = wd_ref[0, 0].astype(BF16)

    def body(j, carry):
        b = first + j
        slot = lax.rem(b, 2)
        in_copy(b, slot).wait()

        @pl.when(b + 1 < n_used)
        def _():
            in_copy(b + 1, 1 - slot).start()

        @pl.when(b >= 2)
        def _():
            out_copy(b - 2, slot).wait()

        x = _from_row_tiles(xbuf.at[slot], 0, MOE_BLOCK).astype(BF16)
        gu = jnp.dot(x, wgu_bf[...], preferred_element_type=F32) + bgu_ref[0, 0]
        g = jnp.minimum(gu[:, :D_FF], SWIGLU_LIMIT)
        lin = jnp.clip(gu[:, D_FF:], -SWIGLU_LIMIT, SWIGLU_LIMIT)
        a = g * jax.nn.sigmoid(SWIGLU_ALPHA * g) * (lin + 1.0)
        y = jnp.dot(a.astype(BF16), wd_bf[...], preferred_element_type=F32) + bd_ref[0, 0]
        _to_row_tiles(ybuf.at[slot], y, MOE_BLOCK)
        out_copy(b, slot).start()
        return carry

    lax.fori_loop(0, nblk_ref[e], body, 0)

    @pl.when(e == N_EXPERTS - 1)
    def _():
        @pl.when(n_used >= 2)
        def _():
            out_copy(n_used - 2, lax.rem(n_used, 2)).wait()

        out_copy(n_used - 1, lax.rem(n_used - 1, 2)).wait()
        ybuf[0] = jnp.zeros(ybuf.shape[1:], F32)
        lax.fori_loop(n_used, n_blocks, lambda b, c: (out_copy(b, 0).start(), c)[1], 0)
        lax.fori_loop(n_used, n_blocks, lambda b, c: (out_copy(b, 0).wait(), c)[1], 0)


def _ffn(layer, first_blk, n_blk, n_used, order, xs, w_gu, b_gu, w_down, b_down, n_blocks):
    block_rows = MOE_BLOCK * ROW_TILES
    depth = w_gu.shape[0]

    def expert(e, *_):
        return (layer, e, 0, 0)

    return pl.pallas_call(
        functools.partial(_ffn_kernel, n_blocks=n_blocks),
        out_shape=jax.ShapeDtypeStruct((n_blocks * block_rows, LANES), F32),
        grid_spec=pltpu.PrefetchScalarGridSpec(
            num_scalar_prefetch=4,
            grid=(N_EXPERTS,),
            in_specs=[
                pl.BlockSpec(memory_space=pl.ANY),
                pl.BlockSpec((1, 1, D_MODEL, 2 * D_FF), expert),
                pl.BlockSpec((1, 1, 1, 2 * D_FF), expert),
                pl.BlockSpec((1, 1, D_FF, D_MODEL), expert),
                pl.BlockSpec((1, 1, 1, D_MODEL), expert),
            ],
            out_specs=pl.BlockSpec(memory_space=pl.ANY),
            scratch_shapes=[
                pltpu.VMEM((D_MODEL, 2 * D_FF), BF16),
                pltpu.VMEM((D_FF, D_MODEL), BF16),
                pltpu.VMEM((2, block_rows, LANES), F32),
                pltpu.VMEM((2, block_rows, LANES), F32),
                pltpu.SemaphoreType.DMA((2,)),
                pltpu.SemaphoreType.DMA((2,)),
            ],
        ),
        compiler_params=pltpu.CompilerParams(dimension_semantics=("arbitrary",),
                                             vmem_limit_bytes=_vmem_limit(56)),
        name="ffn",
    )(first_blk, n_blk, n_used, order, xs, w_gu, b_gu.reshape(depth, N_EXPERTS, 1, 2 * D_FF), w_down,
      b_down.reshape(depth, N_EXPERTS, 1, D_MODEL))


def _combine_kernel(dest_ref, next_ref, x1_ref, mod_ref, gate_ref, ln2g_ref, ln2b_ref, ys_ref, o_ref,
                    buf0, buf1, sem0, sem1, *, tile, alpha, n_steps):
    t = tile
    step = pl.program_id(0) * pl.num_programs(1) + pl.program_id(1)

    def fetch(d_ref, g, tt, buf, sem):
        tok = g * DEST_GROUP + tt
        for k in range(TOP_K):
            pltpu.make_async_copy(_row(ys_ref, d_ref[g, k, tt]), _row(buf, k * t + tok),
                                  sem).start(priority=k % 2)

    def wait(buf, sem):
        pltpu.make_async_copy(ys_ref.at[pl.ds(0, TOP_K * t * ROW_TILES), :], buf, sem).wait()

    @pl.when(step == 0)
    def _():
        for g in range(t // DEST_GROUP):
            lax.fori_loop(0, DEST_GROUP, lambda tt, c, g=g: (fetch(dest_ref, g, tt, buf0, sem0), c)[1], 0)

    def phase(buf, sem, nxt_buf, nxt_sem):
        wait(buf, sem)
        for g in range(t // DEST_GROUP):
            for tt in range(DEST_GROUP):
                fetch(next_ref, g, tt, nxt_buf, nxt_sem)
        gates = gate_ref[0]
        gates_t = jnp.transpose(jnp.concatenate([gates, jnp.zeros_like(gates)], axis=0))
        ff = jnp.zeros((t, D_MODEL), F32)
        for k in range(TOP_K):
            ff = ff + gates_t[:, k:k + 1] * _from_row_tiles(buf, k * t, t)
        g2 = mod_ref[0][5:6, :]
        o_ref[0] = _layernorm(alpha * x1_ref[0] + (1.0 + g2) * ff, ln2g_ref[...], ln2b_ref[...])

    @pl.when(step % 2 == 0)
    def _():
        phase(buf0, sem0, buf1, sem1)

    @pl.when(step % 2 == 1)
    def _():
        phase(buf1, sem1, buf0, sem0)

    @pl.when(step == n_steps - 1)
    def _():
        if (n_steps - 1) % 2 == 0:
            wait(buf1, sem1)
        else:
            wait(buf0, sem0)


def _combine(dest, group_offset, x1, mod, gates, ln2_g, ln2_b, ys, *, tile, alpha):
    b, s, d = x1.shape
    n_t = s // tile
    n_steps = b * n_t
    groups = tile // DEST_GROUP
    goff = group_offset // groups
    per_gate_row = gates.shape[2] // tile
    kern = functools.partial(_combine_kernel, tile=tile, alpha=alpha, n_steps=n_steps)
    buf = pltpu.VMEM((TOP_K * tile * ROW_TILES, LANES), F32)
    return pl.pallas_call(
        kern,
        out_shape=jax.ShapeDtypeStruct((b, s, d), F32),
        grid=(b, n_t),
        in_specs=[
            pl.BlockSpec((groups, TOP_K, DEST_GROUP), lambda bb, ii: (goff + bb * n_t + ii, 0, 0),
                         memory_space=pltpu.SMEM),
            pl.BlockSpec((groups, TOP_K, DEST_GROUP),
                         lambda bb, ii: (goff + jnp.minimum(bb * n_t + ii + 1, n_steps - 1), 0, 0),
                         memory_space=pltpu.SMEM),
            pl.BlockSpec((1, tile, d), lambda bb, ii: (bb, ii, 0)),
            pl.BlockSpec((1, 6, d), lambda bb, ii: (bb, 0, 0)),
            pl.BlockSpec((1, TOP_K, tile), lambda bb, ii: ((bb * n_t + ii) // per_gate_row, 0,
                                                           (bb * n_t + ii) % per_gate_row)),
            pl.BlockSpec((1, d), lambda bb, ii: (0, 0)),
            pl.BlockSpec((1, d), lambda bb, ii: (0, 0)),
            pl.BlockSpec(memory_space=pl.ANY),
        ],
        out_specs=pl.BlockSpec((1, tile, d), lambda bb, ii: (bb, ii, 0)),
        scratch_shapes=[buf, buf, pltpu.SemaphoreType.DMA, pltpu.SemaphoreType.DMA],
        compiler_params=pltpu.CompilerParams(dimension_semantics=("arbitrary", "arbitrary"),
                                             vmem_limit_bytes=_vmem_limit(32)),
        name="combine",
    )(dest, dest, x1, mod, gates, ln2_g, ln2_b, ys)


def kernel(x_prompt, x_sample, c_prompt, c_sample, cache_k, cache_v, state_conv, rel_table, ln0_g, ln0_b, w_ada, b_ada, w_in, b_in, sinks, conv_w, w_oa, w_ob, w_o, ln1_g, ln1_b, w_router, b_router, w_gu, b_gu, w_down, b_down, ln2_g, ln2_b):
    depth = w_ada.shape[0]
    bp, sp, d = x_prompt.shape
    bs, ss, _ = x_sample.shape
    alpha = (2 * depth) ** 0.25
    n_tok = bp * sp + bs * ss
    assert sp % PROMPT_TILE == 0 and ss % DEST_GROUP == 0 and ss <= WINDOW
    assert -(-n_tok // MOE_BLOCK) <= TABLE_LANES
    n_blocks = -(-(n_tok * TOP_K) // MOE_BLOCK) + N_EXPERTS

    mod_all = _ada(jnp.concatenate([c_prompt, c_sample], axis=0), w_ada, b_ada)
    mod_all = mod_all.reshape(depth, bp + bs, 6, d)
    bias = _bias_table(rel_table)
    row = lambda a: a.reshape(1, -1)
    zeros_kv = jnp.zeros((bp, WINDOW, KV_WIDTH), F32)
    zeros_u = jnp.zeros((bp, SUBLANES, CONV_WIDTH), F32)
    state0 = jnp.zeros((N_EXPERTS, LANES), F32)
    table0 = jnp.zeros((N_EXPERTS, TABLE_LANES), F32)
    slot = jnp.arange(n_blocks, dtype=jnp.int32)

    y_p, y_s = x_prompt, x_sample
    outs = {name: [] for name in ("kp", "vp", "up", "ks", "vs", "us")}
    for l in range(depth):
        shared = (w_in[l].astype(BF16), row(b_in[l]))
        tail = (sinks[l], conv_w[l], w_oa[l].astype(BF16), w_ob[l].astype(BF16), w_o[l].astype(BF16),
                row(ln1_g[l]), row(ln1_b[l]), w_router[l].T.astype(BF16), b_router[l].reshape(N_EXPERTS, 1))
        mod_p, mod_s = mod_all[l, :bp], mod_all[l, bp:]
        x1_p, gate_p, dest_p, k_p, v_p, u_p, state, table, xs = _mix(
            y_p, mod_p, row(ln0_g), row(ln0_b), *shared, bias, *tail, zeros_kv, zeros_kv, zeros_u,
            state0, table0, None, tile=PROMPT_TILE, apply_ln0=(l == 0), mask_first=True, alpha=alpha,
            n_blocks=n_blocks, finalize=True)
        u0 = jnp.pad(state_conv[l], ((0, 0), (SUBLANES - (CONV_K - 1), 0), (0, 0)))
        x1_s, gate_s, dest_s, k_s, v_s, u_s, state, table, xs = _mix(
            y_s, mod_s, row(ln0_g), row(ln0_b), *shared, bias, *tail,
            cache_k[l].reshape(bs, WINDOW, KV_WIDTH), cache_v[l].reshape(bs, WINDOW, KV_WIDTH), u0,
            state, table, xs, tile=ss, apply_ln0=(l == 0), mask_first=False, alpha=alpha,
            n_blocks=n_blocks, finalize=False)

        counts = state[:, 0].astype(jnp.int32)
        n_used = state[0:1, 2].astype(jnp.int32)
        n_blk = (counts + MOE_BLOCK - 1) // MOE_BLOCK
        blk_end = jnp.cumsum(n_blk)
        first_blk = blk_end - n_blk
        owner = jnp.minimum(jnp.sum((blk_end[None, :] <= slot[:, None]).astype(jnp.int32), axis=1), N_EXPERTS - 1)
        nth = jnp.clip(slot - first_blk[owner], 0, TABLE_LANES - 1)
        order = jnp.where(slot < n_used, table[owner, nth].astype(jnp.int32), slot)

        ys = _ffn(l, first_blk, n_blk, n_used, order, xs, w_gu, b_gu, w_down, b_down, n_blocks)
        y_p = _combine(dest_p, 0, x1_p, mod_p, gate_p, row(ln2_g[l]), row(ln2_b[l]), ys,
                       tile=COMBINE_TILE, alpha=alpha)
        y_s = _combine(dest_s, 0, x1_s, mod_s, gate_s, row(ln2_g[l]), row(ln2_b[l]), ys,
                       tile=ss, alpha=alpha)

        outs["kp"].append(k_p.reshape(bp, WINDOW, N_KV_HEADS, HEAD_DIM))
        outs["vp"].append(v_p.reshape(bp, WINDOW, N_KV_HEADS, HEAD_DIM))
        outs["up"].append(u_p[:, SUBLANES - (CONV_K - 1):, :])
        outs["ks"].append(k_s.reshape(bs, WINDOW, N_KV_HEADS, HEAD_DIM))
        outs["vs"].append(v_s.reshape(bs, WINDOW, N_KV_HEADS, HEAD_DIM))
        outs["us"].append(u_s[:, SUBLANES - (CONV_K - 1):, :])
    return (y_p, y_s, jnp.stack(outs["kp"]), jnp.stack(outs["vp"]), jnp.stack(outs["up"]),
            jnp.stack(outs["ks"]), jnp.stack(outs["vs"]), jnp.stack(outs["us"]))
```

```python
import functools
import math

import jax
import jax.numpy as jnp
import numpy as np
from jax import lax
from jax.experimental import pallas as pl
from jax.experimental.pallas import tpu as pltpu

D_MODEL = 1024
CHUNK = 64
N_HEADS = 8
N_KV_HEADS = 2
HEAD_DIM = 64
GROUP = N_HEADS // N_KV_HEADS
ATT_WIDTH = N_HEADS * HEAD_DIM
KV_WIDTH = N_KV_HEADS * HEAD_DIM
WINDOW = 128
CONV_WIDTH = 512
CONV_K = 3
NUM_BUCKETS = 32
MAX_DISTANCE = 128
N_EXPERTS = 32
TOP_K = 4
D_FF = 1024
SWIGLU_LIMIT = 7.0
SWIGLU_ALPHA = 1.702
MOE_BLOCK = 512
LN_EPS = 1e-5
NEG_INF = -1e30
IN_SIZES = (ATT_WIDTH, KV_WIDTH, KV_WIDTH, CONV_WIDTH, CONV_WIDTH, CONV_WIDTH, D_MODEL, D_MODEL)
IN_WIDTH = sum(IN_SIZES)
IN_OFFS = tuple(int(s) for s in np.cumsum((0,) + IN_SIZES))

SUBLANES = 8
LANES = 128
ROW_TILES = D_MODEL // LANES
assert ROW_TILES == SUBLANES

PROMPT_TILE = 512
COMBINE_TILE = 256
DEST_GROUP = 64
LAG = 2
RING = 2 * LAG
SPARE_BLOCKS = LAG * PROMPT_TILE * TOP_K // MOE_BLOCK
TABLE_LANES = 256

F32 = jnp.float32
BF16 = jnp.bfloat16
HIGHEST = lax.Precision.HIGHEST
NT_DIMS = (((1,), (1,)), ((), ()))


def _vmem_limit(mib):
    return mib * 1024 * 1024


def _layernorm(x, g, b):
    mu = jnp.mean(x, axis=-1, keepdims=True)
    xc = x - mu
    var = jnp.mean(xc * xc, axis=-1, keepdims=True)
    return xc * lax.rsqrt(var + LN_EPS) * g + b


def _to_row_tiles(ref, x, rows):
    for s in range(ROW_TILES):
        ref[pl.ds(s, rows, stride=ROW_TILES), :] = x[:, s * LANES:(s + 1) * LANES]


def _from_row_tiles(ref, base, rows):
    return jnp.concatenate(
        [ref[pl.ds(base * ROW_TILES + s, rows, stride=ROW_TILES), :] for s in range(ROW_TILES)], axis=-1)


def _ada_kernel(c_ref, w_ref, b_ref, o_ref):
    c = c_ref[...]
    s = c * jax.nn.sigmoid(c)
    o_ref[0] = jnp.dot(s, w_ref[0], precision=HIGHEST, preferred_element_type=F32) + b_ref[0]


def _ada(c_all, w_ada, b_ada):
    depth = w_ada.shape[0]
    nb = c_all.shape[0]
    n_col = 6 * D_MODEL // D_MODEL
    return pl.pallas_call(
        _ada_kernel,
        out_shape=jax.ShapeDtypeStruct((depth, nb, 6 * D_MODEL), F32),
        grid=(depth, n_col),
        in_specs=[
            pl.BlockSpec((nb, D_MODEL), lambda l, j: (0, 0)),
            pl.BlockSpec((1, D_MODEL, D_MODEL), lambda l, j: (l, 0, j)),
            pl.BlockSpec((1, 1, D_MODEL), lambda l, j: (l, 0, j)),
        ],
        out_specs=pl.BlockSpec((1, nb, D_MODEL), lambda l, j: (l, 0, j)),
        compiler_params=pltpu.CompilerParams(dimension_semantics=("arbitrary", "arbitrary"),
                                             vmem_limit_bytes=_vmem_limit(32)),
        name="ada",
    )(c_all, w_ada, b_ada.reshape(depth, 1, 6 * D_MODEL))


def _rel_bucket(rel):
    half = NUM_BUCKETS // 2
    max_exact = half // 2
    n = jnp.abs(rel)
    n_f = jnp.maximum(n, 1).astype(jnp.float32)
    large = max_exact + (jnp.log(n_f / max_exact) / math.log(MAX_DISTANCE / max_exact)
                         * (half - max_exact)).astype(jnp.int32)
    large = jnp.minimum(large, half - 1)
    return jnp.where(rel > 0, half, 0) + jnp.where(n < max_exact, n, large)


BAND = WINDOW + CHUNK
PAIR_ROWS = 2 * CHUNK
PAIR_COLS = 2 * BAND


def _band_codes():
    r = jnp.arange(PAIR_ROWS)[:, None]
    j = jnp.arange(PAIR_COLS)[None, :]
    bucket = _rel_bucket(j % BAND - WINDOW - r % CHUNK)
    head = 2 * (r // CHUNK) + j // BAND
    return (bucket + NUM_BUCKETS * head).astype(jnp.int32)


def _bias_kernel(table_ref, code_ref, o_ref):
    g = pl.program_id(0)
    code = code_ref[...]
    acc = jnp.zeros(code.shape, F32)
    for hq in range(GROUP):
        for i in range(NUM_BUCKETS):
            acc = jnp.where(code == hq * NUM_BUCKETS + i, table_ref[i, g * GROUP + hq], acc)
    o_ref[0] = acc


def _bias_table(rel_table):
    return pl.pallas_call(
        _bias_kernel,
        out_shape=jax.ShapeDtypeStruct((N_KV_HEADS, PAIR_ROWS, PAIR_COLS), F32),
        grid=(N_KV_HEADS,),
        in_specs=[
            pl.BlockSpec(memory_space=pltpu.SMEM),
            pl.BlockSpec((PAIR_ROWS, PAIR_COLS), lambda g: (0, 0)),
        ],
        out_specs=pl.BlockSpec((1, PAIR_ROWS, PAIR_COLS), lambda g: (g, 0, 0)),
        compiler_params=pltpu.CompilerParams(dimension_semantics=("arbitrary",)),
        name="rel_bias",
    )(rel_table, _band_codes())


def _row(ref, r):
    return ref.at[pl.ds(pl.multiple_of(r * ROW_TILES, ROW_TILES), ROW_TILES), :]


def _mix_kernel(*refs, tile, apply_ln0, mask_first, alpha, n_steps, n_blocks, has_xs_in, finalize):
    (x_ref, mod_ref, ln0g_ref, ln0b_ref, win_ref, bin_ref, bias_ref, sink_ref, convw_ref,
     woa_ref, wob_ref, wo_ref, ln1g_ref, ln1b_ref, wr_ref, br_ref, k0_ref, v0_ref, u0_ref,
     upto_ref, st0_ref, tab0_ref) = refs[:22]
    refs = refs[22 + (1 if has_xs_in else 0):]
    (x1_ref, gate_ref, dest_ref, newk_ref, newv_ref, newu_ref, st_ref, tab_ref, xs_ref,
     kc_ref, vc_ref, ubuf_ref, run_ref, cur_ref, nfree_ref, h2buf, dest_v, dest_s, fin_v, fin_s,
     ssem, dsem, zsem) = refs
    i = pl.program_id(1)
    t = tile
    lin = pl.program_id(0) * pl.num_programs(1) + i
    now = lax.rem(lin, RING)
    src = lax.rem(lin + RING - LAG, RING)
    slot_rows = t * ROW_TILES

    def dest_copy(slot):
        return pltpu.make_async_copy(dest_v.at[slot], dest_s.at[slot], dsem.at[slot])

    def scatter_wait(slot):
        for _ in range(TOP_K):
            pltpu.make_async_copy(h2buf.at[slot], xs_ref.at[pl.ds(0, slot_rows), :], ssem.at[slot]).wait()

    def scatter_row(slot, tok):
        for k in range(TOP_K):
            pltpu.make_async_copy(_row(h2buf.at[slot], tok), _row(xs_ref, dest_s[slot, k, tok]),
                                  ssem.at[slot]).start(priority=k % 2)

    @pl.when(lin == 0)
    def _():
        run_ref[...] = st0_ref[:, 0:1]
        cur_ref[...] = st0_ref[:, 1:2]
        nfree_ref[...] = st0_ref[0:1, 2:3]
        tab_ref[...] = tab0_ref[...]
        for back in range(1, LAG + 1):
            h2buf[RING - back] = jnp.zeros(h2buf.shape[1:], F32)
            for k in range(TOP_K):
                def fill(tok, c, k=k, back=back):
                    dest_s[RING - back, k, tok] = n_blocks * MOE_BLOCK + ((back - 1) * TOP_K + k) * t + tok
                    return c
                lax.fori_loop(0, t, fill, 0)

    @pl.when(lin >= LAG)
    def _():
        dest_copy(src).wait()
        scatter_wait(now)

    @pl.when(i == 0)
    def _():
        kc_ref[...] = k0_ref[0]
        vc_ref[...] = v0_ref[0]
        ubuf_ref[0:SUBLANES, :] = u0_ref[0]

    for tok in range(t):
        scatter_row(src, tok)

    x = x_ref[0]
    if apply_ln0:
        x = _layernorm(x, ln0g_ref[...], ln0b_ref[...])
    mod = mod_ref[0]
    sh1, sc1, g1, sh2, sc2, g2 = [mod[j:j + 1, :] for j in range(6)]
    h = (x * (1.0 + sc1) + sh1).astype(BF16)

    def proj(j0, j1):
        lo, hi = IN_OFFS[j0], IN_OFFS[j1]
        return jnp.dot(h, win_ref[:, lo:hi], preferred_element_type=F32) + bin_ref[:, lo:hi]

    q = proj(0, 1)
    kv = proj(1, 3)
    kfull = jnp.concatenate([kc_ref[...], kv[:, :KV_WIDTH]], axis=0)
    vfull = jnp.concatenate([vc_ref[...], kv[:, KV_WIDTH:]], axis=0)
    kc_ref[...] = kfull[t:, :]
    vc_ref[...] = vfull[t:, :]
    newk_ref[0] = kfull[t:, :]
    newv_ref[0] = vfull[t:, :]
    low = lax.broadcasted_iota(jnp.int32, kfull.shape, 1) < HEAD_DIM
    k_sw = pltpu.roll(kfull, HEAD_DIM, axis=1)
    v_sw = pltpu.roll(vfull, HEAD_DIM, axis=1)
    k_even = [jnp.where(low, kfull, 0.0).astype(BF16), jnp.where(low, k_sw, 0.0).astype(BF16)]
    k_odd = [jnp.where(low, 0.0, k_sw).astype(BF16), jnp.where(low, 0.0, kfull).astype(BF16)]
    v_even = [jnp.where(low, vfull, 0.0).astype(BF16), jnp.where(low, v_sw, 0.0).astype(BF16)]
    v_odd = [jnp.where(low, 0.0, v_sw).astype(BF16), jnp.where(low, 0.0, vfull).astype(BF16)]
    col = lax.broadcasted_iota(jnp.int32, (PAIR_ROWS, PAIR_COLS), 1)
    even = col < BAND
    band_col = jnp.where(even, col, col - BAND)
    first_pair = lax.broadcasted_iota(jnp.int32, (PAIR_ROWS, 1), 0) < CHUNK
    out_low = lax.broadcasted_iota(jnp.int32, (PAIR_ROWS, LANES), 1) < HEAD_DIM
    sink_even = [jnp.where(first_pair, sink_ref[g * GROUP], sink_ref[g * GROUP + 2]) for g in range(N_KV_HEADS)]
    sink_odd = [jnp.where(first_pair, sink_ref[g * GROUP + 1], sink_ref[g * GROUP + 3])
                for g in range(N_KV_HEADS)]
    units = [(c, g) for c in range(t // CHUNK) for g in range(N_KV_HEADS)]
    scores = []
    for c, g in units:
        r0 = c * CHUNK
        ql = jnp.concatenate([q[r0:r0 + CHUNK, (2 * g) * LANES:(2 * g + 1) * LANES],
                              q[r0:r0 + CHUNK, (2 * g + 1) * LANES:(2 * g + 2) * LANES]], axis=0)
        kp = jnp.concatenate([k_even[g][r0:r0 + BAND], k_odd[g][r0:r0 + BAND]], axis=0)
        s = lax.dot_general(ql.astype(BF16), kp, NT_DIMS, preferred_element_type=F32)
        s = s * (HEAD_DIM ** -0.5) + bias_ref[g]
        if mask_first and c < WINDOW // CHUNK:
            s = jnp.where(band_col < jnp.where(i == 0, WINDOW - r0, 0), NEG_INF, s)
        scores.append(s)
    maxes = []
    for (c, g), s in zip(units, scores):
        m_e = jnp.maximum(jnp.max(jnp.where(even, s, -jnp.inf), axis=-1, keepdims=True), sink_even[g])
        m_o = jnp.maximum(jnp.max(jnp.where(even, -jnp.inf, s), axis=-1, keepdims=True), sink_odd[g])
        maxes.append((m_e, m_o))
    exps, scales = [], []
    for (c, g), s, (m_e, m_o) in zip(units, scores, maxes):
        e = jnp.exp(s - jnp.where(even, m_e, m_o))
        d_e = jnp.sum(jnp.where(even, e, 0.0), axis=-1, keepdims=True) + jnp.exp(sink_even[g] - m_e)
        d_o = jnp.sum(jnp.where(even, 0.0, e), axis=-1, keepdims=True) + jnp.exp(sink_odd[g] - m_o)
        exps.append(e.astype(BF16))
        scales.append(jnp.where(out_low, 1.0 / d_e, 1.0 / d_o))
    outs = {}
    for (c, g), e, scale in zip(units, exps, scales):
        r0 = c * CHUNK
        vp = jnp.concatenate([v_even[g][r0:r0 + BAND], v_odd[g][r0:r0 + BAND]], axis=0)
        outs[c, g] = jnp.dot(e, vp, preferred_element_type=F32) * scale
    ya = jnp.concatenate(
        [jnp.concatenate([outs[c, g][half * CHUNK:(half + 1) * CHUNK] for g in range(N_KV_HEADS)
                          for half in range(2)], axis=-1) for c in range(t // CHUNK)], axis=0)

    cb = proj(3, 4)
    u = proj(4, 5) * proj(5, 6)
    ubuf_ref[SUBLANES:t + SUBLANES, :] = u
    cw = convw_ref[...]
    yc = (cw[0:1, :] * ubuf_ref[SUBLANES - 2:t + SUBLANES - 2, :]
          + cw[1:2, :] * ubuf_ref[SUBLANES - 1:t + SUBLANES - 1, :] + cw[2:3, :] * u)
    yb = cb * yc
    tail = ubuf_ref[t:t + SUBLANES, :]
    newu_ref[0] = tail
    ubuf_ref[0:SUBLANES, :] = tail

    a_out = jnp.dot(ya.astype(BF16), woa_ref[...], preferred_element_type=F32)
    b_out = jnp.dot(yb.astype(BF16), wob_ref[...], preferred_element_type=F32)
    mixin = jax.nn.sigmoid(proj(6, 7)) * a_out + jax.nn.sigmoid(proj(7, 8)) * b_out
    mix = jnp.dot(mixin.astype(BF16), wo_ref[...], preferred_element_type=F32)
    x1 = _layernorm(alpha * x + (1.0 + g1) * mix, ln1g_ref[...], ln1b_ref[...])
    x1_ref[0] = x1

    h2 = x1 * (1.0 + sc2) + sh2
    _to_row_tiles(h2buf.at[now], h2, t)
    logits = lax.dot_general(wr_ref[...], h2.astype(BF16), NT_DIMS, preferred_element_type=F32) + br_ref[...]
    eid = lax.broadcasted_iota(jnp.int32, (N_EXPERTS, t), 0)
    vals, ids = [], []
    for _ in range(TOP_K):
        mx = jnp.max(logits, axis=0, keepdims=True)
        sel = jnp.min(jnp.where(logits == mx, eid, N_EXPERTS), axis=0, keepdims=True)
        vals.append(mx)
        ids.append(sel)
        logits = jnp.where(eid == sel, -jnp.inf, logits)
    ex = [jnp.exp(v - vals[0]) for v in vals]
    tot = ex[0] + ex[1] + ex[2] + ex[3]
    gate_ref[0] = jnp.concatenate([e_ / tot for e_ in ex], axis=0)

    per_block = 1.0 / MOE_BLOCK
    hits = [(eid == ids[k]).astype(F32) for k in range(TOP_K)]
    hit = hits[0] + hits[1] + hits[2] + hits[3]
    count = jnp.sum(hit, axis=1, keepdims=True)
    incl = jnp.dot(hit.astype(BF16), upto_ref[...], preferred_element_type=F32)
    run = run_ref[...]
    cur = cur_ref[...]
    run_blk = jnp.floor(run * per_block)
    is_open = (run - run_blk * MOE_BLOCK > 0).astype(F32)
    end = run + count
    blocks_before = jnp.ceil(run * per_block)
    n_new = jnp.ceil(end * per_block) - blocks_before
    er = lax.broadcasted_iota(jnp.int32, (N_EXPERTS, N_EXPERTS), 0)
    ec = lax.broadcasted_iota(jnp.int32, (N_EXPERTS, N_EXPERTS), 1)
    earlier = jnp.dot((ec < er).astype(BF16), jnp.broadcast_to(n_new, (N_EXPERTS, LANES)).astype(BF16),
                      preferred_element_type=F32)[:, 0:1]
    base = nfree_ref[...] + earlier
    r = run + incl - hit
    r_blk = jnp.floor(r * per_block)
    ordinal = r_blk - run_blk
    block = jnp.where(jnp.logical_and(is_open > 0, ordinal == 0), cur, base + ordinal - is_open)
    row = block * MOE_BLOCK + (r - r_blk * MOE_BLOCK)
    dests = [jnp.sum(hits[k] * row, axis=0, keepdims=True).astype(jnp.int32) for k in range(TOP_K)]
    dest_v[now] = jnp.concatenate(dests, axis=0)
    for k in range(TOP_K):
        for gidx in range(t // DEST_GROUP):
            dest_ref[gidx, k:k + 1, :] = dests[k][:, gidx * DEST_GROUP:(gidx + 1) * DEST_GROUP]
    lane = lax.broadcasted_iota(jnp.int32, tab_ref.shape, 1).astype(F32)
    table = tab_ref[...]
    for j in range(-(-t // MOE_BLOCK)):
        table = jnp.where(jnp.logical_and(lane == blocks_before + j, n_new > j), base + j, table)
    tab_ref[...] = table
    cur = jnp.where(n_new > 0, base + n_new - 1, cur)
    nfree = nfree_ref[...] + jnp.sum(n_new, axis=0, keepdims=True)
    run_ref[...] = end
    cur_ref[...] = cur
    nfree_ref[...] = nfree
    st_lane = lax.broadcasted_iota(jnp.int32, st_ref.shape, 1)
    st_ref[...] = jnp.where(st_lane == 0, end, jnp.where(st_lane == 1, cur, nfree))
    dest_copy(now).start()

    @pl.when(lin == n_steps - 1)
    def _():
        for back in range(LAG):
            late = lax.rem(lin + RING - back, RING)
            dest_copy(late).wait()
            lax.fori_loop(0, t, lambda tok, c, late=late: (scatter_row(late, tok), c)[1], 0)
        for slot in range(RING):
            scatter_wait(slot)
        if finalize:
            rem = end - jnp.floor(end * per_block) * MOE_BLOCK
            tail_lo = cur * MOE_BLOCK + rem
            tail_hi = jnp.where(rem > 0, (cur + 1.0) * MOE_BLOCK, tail_lo)
            fin_v[...] = jnp.where(st_lane == 0, tail_lo, jnp.where(st_lane == 1, tail_hi, nfree)).astype(jnp.int32)
            fin = pltpu.make_async_copy(fin_v, fin_s, dsem.at[0])
            fin.start()
            fin.wait()
            h2buf[0] = jnp.zeros(h2buf.shape[1:], F32)

            def zero_rows(r_, n_rows):
                return pltpu.make_async_copy(
                    h2buf.at[0].at[pl.ds(0, n_rows * ROW_TILES), :],
                    xs_ref.at[pl.ds(pl.multiple_of(r_ * ROW_TILES, ROW_TILES), n_rows * ROW_TILES), :], zsem)

            piece = min(slot_rows, MOE_BLOCK * ROW_TILES)

            def zero_part(c_):
                return pltpu.make_async_copy(
                    h2buf.at[0].at[pl.ds(0, piece), :],
                    xs_ref.at[pl.ds(pl.multiple_of(c_ * piece, piece), piece), :], zsem)

            def per_expert(e, carry):
                lo_, hi_ = fin_s[e, 0], fin_s[e, 1]
                n_big = lax.div(hi_ - lo_, DEST_GROUP)
                mid = lo_ + n_big * DEST_GROUP
                for act in ("start", "wait"):
                    lax.fori_loop(0, n_big, lambda j, c, act=act: (
                        getattr(zero_rows(lo_ + j * DEST_GROUP, DEST_GROUP), act)(), c)[1], 0)
                    lax.fori_loop(mid, hi_, lambda r_, c, act=act: (getattr(zero_rows(r_, 1), act)(), c)[1], 0)
                return carry

            lax.fori_loop(0, N_EXPERTS, per_expert, 0)
            parts = MOE_BLOCK * ROW_TILES // piece
            lo, hi = fin_s[0, 2] * parts, n_blocks * parts
            lax.fori_loop(lo, hi, lambda c_, c: (zero_part(c_).start(), c)[1], 0)
            lax.fori_loop(lo, hi, lambda c_, c: (zero_part(c_).wait(), c)[1], 0)


def _mix(x, mod, ln0_g, ln0_b, w_in, b_in, bias, sink, conv_w, w_oa, w_ob, w_o, ln1_g, ln1_b,
         w_rt, b_r, k0, v0, u0, state, table, xs_in, *, tile, apply_ln0, mask_first, alpha, n_blocks, finalize):
    b, s, d = x.shape
    n_t = s // tile
    assert b * n_t >= RING
    const = lambda shape: pl.BlockSpec(shape, lambda bb, ii: (0,) * len(shape), pipeline_mode=pl.Buffered(1))
    per_b = lambda shape: pl.BlockSpec((1,) + shape, lambda bb, ii: (bb,) + (0,) * len(shape))
    whole = lambda shape: pl.BlockSpec(shape, lambda bb, ii: (0,) * len(shape))
    kern = functools.partial(_mix_kernel, tile=tile, apply_ln0=apply_ln0, mask_first=mask_first, alpha=alpha,
                             n_steps=b * n_t, n_blocks=n_blocks, has_xs_in=xs_in is not None, finalize=finalize)
    upto = (jnp.arange(tile)[:, None] <= jnp.arange(tile)[None, :]).astype(BF16)
    xs_rows = (n_blocks + SPARE_BLOCKS) * MOE_BLOCK * ROW_TILES
    groups = tile // DEST_GROUP
    operands = [x, mod, ln0_g, ln0_b, w_in, b_in, bias, sink, conv_w, w_oa, w_ob, w_o, ln1_g, ln1_b, w_rt, b_r,
                k0, v0, u0, upto, state, table]
    in_specs = [
        pl.BlockSpec((1, tile, d), lambda bb, ii: (bb, ii, 0)),
        per_b((6, d)),
        const((1, d)), const((1, d)),
        const((d, IN_WIDTH)), const((1, IN_WIDTH)),
        const((N_KV_HEADS, PAIR_ROWS, PAIR_COLS)),
        pl.BlockSpec(memory_space=pltpu.SMEM),
        const((CONV_K, CONV_WIDTH)),
        const((ATT_WIDTH, d)), const((CONV_WIDTH, d)), const((d, d)),
        const((1, d)), const((1, d)),
        const((N_EXPERTS, d)), const((N_EXPERTS, 1)),
        per_b((WINDOW, KV_WIDTH)), per_b((WINDOW, KV_WIDTH)), per_b((SUBLANES, CONV_WIDTH)),
        const((tile, tile)), const((N_EXPERTS, LANES)), const((N_EXPERTS, TABLE_LANES)),
    ]
    aliases = {}
    if xs_in is not None:
        operands.append(xs_in)
        in_specs.append(pl.BlockSpec(memory_space=pl.ANY))
        aliases = {len(operands) - 1: 8}
    return pl.pallas_call(
        kern,
        out_shape=(
            jax.ShapeDtypeStruct((b, s, d), F32),
            jax.ShapeDtypeStruct((b * n_t, TOP_K, tile), F32),
            jax.ShapeDtypeStruct((b * s // DEST_GROUP, TOP_K, DEST_GROUP), jnp.int32),
            jax.ShapeDtypeStruct((b, WINDOW, KV_WIDTH), F32),
            jax.ShapeDtypeStruct((b, WINDOW, KV_WIDTH), F32),
            jax.ShapeDtypeStruct((b, SUBLANES, CONV_WIDTH), F32),
            jax.ShapeDtypeStruct((N_EXPERTS, LANES), F32),
            jax.ShapeDtypeStruct((N_EXPERTS, TABLE_LANES), F32),
            jax.ShapeDtypeStruct((xs_rows, LANES), F32),
        ),
        grid=(b, n_t),
        in_specs=in_specs,
        out_specs=(
            pl.BlockSpec((1, tile, d), lambda bb, ii: (bb, ii, 0)),
            pl.BlockSpec((1, TOP_K, tile), lambda bb, ii: (bb * n_t + ii, 0, 0)),
            pl.BlockSpec((groups, TOP_K, DEST_GROUP), lambda bb, ii: (bb * n_t + ii, 0, 0)),
            per_b((WINDOW, KV_WIDTH)), per_b((WINDOW, KV_WIDTH)), per_b((SUBLANES, CONV_WIDTH)),
            whole((N_EXPERTS, LANES)), whole((N_EXPERTS, TABLE_LANES)),
            pl.BlockSpec(memory_space=pl.ANY),
        ),
        scratch_shapes=[
            pltpu.VMEM((WINDOW, KV_WIDTH), F32),
            pltpu.VMEM((WINDOW, KV_WIDTH), F32),
            pltpu.VMEM((tile + SUBLANES, CONV_WIDTH), F32),
            pltpu.VMEM((N_EXPERTS, 1), F32),
            pltpu.VMEM((N_EXPERTS, 1), F32),
            pltpu.VMEM((1, 1), F32),
            pltpu.VMEM((RING, tile * ROW_TILES, LANES), F32),
            pltpu.VMEM((RING, TOP_K, tile), jnp.int32),
            pltpu.SMEM((RING, TOP_K, tile), jnp.int32),
            pltpu.VMEM((N_EXPERTS, LANES), jnp.int32),
            pltpu.SMEM((N_EXPERTS, LANES), jnp.int32),
            pltpu.SemaphoreType.DMA((RING,)),
            pltpu.SemaphoreType.DMA((RING,)),
            pltpu.SemaphoreType.DMA,
        ],
        input_output_aliases=aliases,
        compiler_params=pltpu.CompilerParams(dimension_semantics=("arbitrary", "arbitrary"),
                                             vmem_limit_bytes=_vmem_limit(56)),
        name="mix",
    )(*operands)


def _ffn_kernel(first_ref, nblk_ref, nused_ref, order_ref, xs_ref, wgu_ref, bgu_ref, wd_ref, bd_ref, ys_ref,
                wgu_bf, wd_bf, xbuf, ybuf, in_sem, out_sem, *, n_blocks):
    e = pl.program_id(0)
    first = first_ref[e]
    n_used = nused_ref[0]
    block_rows = MOE_BLOCK * ROW_TILES

    def blk(ref, b):
        return ref.at[pl.ds(pl.multiple_of(b * block_rows, block_rows), block_rows), :]

    def in_copy(b, slot):
        return pltpu.make_async_copy(blk(xs_ref, order_ref[b]), xbuf.at[slot], in_sem.at[slot])

    def out_copy(b, slot):
        return pltpu.make_async_copy(ybuf.at[slot], blk(ys_ref, order_ref[b]), out_sem.at[slot])

    @pl.when(e == 0)
    def _():
        in_copy(0, 0).start()

    @pl.when(nblk_ref[e] > 0)
    def _():
        wgu_bf[...] = wgu_ref[0, 0].astype(BF16)
        wd_bf[...] = wd_ref[0, 0].astype(BF16)

    def body(j, carry):
        b = first + j
        slot = lax.rem(b, 2)
        in_copy(b, slot).wait()

        @pl.when(b + 1 < n_used)
        def _():
            in_copy(b + 1, 1 - slot).start()

        @pl.when(b >= 2)
        def _():
            out_copy(b - 2, slot).wait()

        x = _from_row_tiles(xbuf.at[slot], 0, MOE_BLOCK).astype(BF16)
        gu = jnp.dot(x, wgu_bf[...], preferred_element_type=F32) + bgu_ref[0, 0]
        g = jnp.minimum(gu[:, :D_FF], SWIGLU_LIMIT)
        lin = jnp.clip(gu[:, D_FF:], -SWIGLU_LIMIT, SWIGLU_LIMIT)
        a = g * jax.nn.sigmoid(SWIGLU_ALPHA * g) * (lin + 1.0)
        y = jnp.dot(a.astype(BF16), wd_bf[...], preferred_element_type=F32) + bd_ref[0, 0]
        _to_row_tiles(ybuf.at[slot], y, MOE_BLOCK)
        out_copy(b, slot).start()
        return carry

    lax.fori_loop(0, nblk_ref[e], body, 0)

    @pl.when(e == N_EXPERTS - 1)
    def _():
        @pl.when(n_used >= 2)
        def _():
            out_copy(n_used - 2, lax.rem(n_used, 2)).wait()

        out_copy(n_used - 1, lax.rem(n_used - 1, 2)).wait()
        ybuf[0] = jnp.zeros(ybuf.shape[1:], F32)
        lax.fori_loop(n_used, n_blocks, lambda b, c: (out_copy(b, 0).start(), c)[1], 0)
        lax.fori_loop(n_used, n_blocks, lambda b, c: (out_copy(b, 0).wait(), c)[1], 0)


def _ffn(layer, first_blk, n_blk, n_used, order, xs, w_gu, b_gu, w_down, b_down, n_blocks):
    block_rows = MOE_BLOCK * ROW_TILES
    depth = w_gu.shape[0]

    def expert(e, *_):
        return (layer, e, 0, 0)

    return pl.pallas_call(
        functools.partial(_ffn_kernel, n_blocks=n_blocks),
        out_shape=jax.ShapeDtypeStruct((n_blocks * block_rows, LANES), F32),
        grid_spec=pltpu.PrefetchScalarGridSpec(
            num_scalar_prefetch=4,
            grid=(N_EXPERTS,),
            in_specs=[
                pl.BlockSpec(memory_space=pl.ANY),
                pl.BlockSpec((1, 1, D_MODEL, 2 * D_FF), expert),
                pl.BlockSpec((1, 1, 1, 2 * D_FF), expert),
                pl.BlockSpec((1, 1, D_FF, D_MODEL), expert),
                pl.BlockSpec((1, 1, 1, D_MODEL), expert),
            ],
            out_specs=pl.BlockSpec(memory_space=pl.ANY),
            scratch_shapes=[
                pltpu.VMEM((D_MODEL, 2 * D_FF), BF16),
                pltpu.VMEM((D_FF, D_MODEL), BF16),
                pltpu.VMEM((2, block_rows, LANES), F32),
                pltpu.VMEM((2, block_rows, LANES), F32),
                pltpu.SemaphoreType.DMA((2,)),
                pltpu.SemaphoreType.DMA((2,)),
            ],
        ),
        compiler_params=pltpu.CompilerParams(dimension_semantics=("arbitrary",),
                                             vmem_limit_bytes=_vmem_limit(56)),
        name="ffn",
    )(first_blk, n_blk, n_used, order, xs, w_gu, b_gu.reshape(depth, N_EXPERTS, 1, 2 * D_FF), w_down,
      b_down.reshape(depth, N_EXPERTS, 1, D_MODEL))


def _combine_kernel(dest_ref, next_ref, x1_ref, mod_ref, gate_ref, ln2g_ref, ln2b_ref, ys_ref, o_ref,
                    buf0, buf1, sem0, sem1, *, tile, alpha, n_steps):
    t = tile
    step = pl.program_id(0) * pl.num_programs(1) + pl.program_id(1)

    def fetch(d_ref, g, tt, buf, sem):
        tok = g * DEST_GROUP + tt
        for k in range(TOP_K):
            pltpu.make_async_copy(_row(ys_ref, d_ref[g, k, tt]), _row(buf, k * t + tok),
                                  sem).start(priority=k % 2)

    def wait(buf, sem):
        pltpu.make_async_copy(ys_ref.at[pl.ds(0, TOP_K * t * ROW_TILES), :], buf, sem).wait()

    @pl.when(step == 0)
    def _():
        for g in range(t // DEST_GROUP):
            lax.fori_loop(0, DEST_GROUP, lambda tt, c, g=g: (fetch(dest_ref, g, tt, buf0, sem0), c)[1], 0)

    def phase(buf, sem, nxt_buf, nxt_sem):
        wait(buf, sem)
        for g in range(t // DEST_GROUP):
            for tt in range(DEST_GROUP):
                fetch(next_ref, g, tt, nxt_buf, nxt_sem)
        gates = gate_ref[0]
        gates_t = jnp.transpose(jnp.concatenate([gates, jnp.zeros_like(gates)], axis=0))
        ff = jnp.zeros((t, D_MODEL), F32)
        for k in range(TOP_K):
            ff = ff + gates_t[:, k:k + 1] * _from_row_tiles(buf, k * t, t)
        g2 = mod_ref[0][5:6, :]
        o_ref[0] = _layernorm(alpha * x1_ref[0] + (1.0 + g2) * ff, ln2g_ref[...], ln2b_ref[...])

    @pl.when(step % 2 == 0)
    def _():
        phase(buf0, sem0, buf1, sem1)

    @pl.when(step % 2 == 1)
    def _():
        phase(buf1, sem1, buf0, sem0)

    @pl.when(step == n_steps - 1)
    def _():
        if (n_steps - 1) % 2 == 0:
            wait(buf1, sem1)
        else:
            wait(buf0, sem0)


def _combine(dest, group_offset, x1, mod, gates, ln2_g, ln2_b, ys, *, tile, alpha):
    b, s, d = x1.shape
    n_t = s // tile
    n_steps = b * n_t
    groups = tile // DEST_GROUP
    goff = group_offset // groups
    per_gate_row = gates.shape[2] // tile
    kern = functools.partial(_combine_kernel, tile=tile, alpha=alpha, n_steps=n_steps)
    buf = pltpu.VMEM((TOP_K * tile * ROW_TILES, LANES), F32)
    return pl.pallas_call(
        kern,
        out_shape=jax.ShapeDtypeStruct((b, s, d), F32),
        grid=(b, n_t),
        in_specs=[
            pl.BlockSpec((groups, TOP_K, DEST_GROUP), lambda bb, ii: (goff + bb * n_t + ii, 0, 0),
                         memory_space=pltpu.SMEM),
            pl.BlockSpec((groups, TOP_K, DEST_GROUP),
                         lambda bb, ii: (goff + jnp.minimum(bb * n_t + ii + 1, n_steps - 1), 0, 0),
                         memory_space=pltpu.SMEM),
            pl.BlockSpec((1, tile, d), lambda bb, ii: (bb, ii, 0)),
            pl.BlockSpec((1, 6, d), lambda bb, ii: (bb, 0, 0)),
            pl.BlockSpec((1, TOP_K, tile), lambda bb, ii: ((bb * n_t + ii) // per_gate_row, 0,
                                                           (bb * n_t + ii) % per_gate_row)),
            pl.BlockSpec((1, d), lambda bb, ii: (0, 0)),
            pl.BlockSpec((1, d), lambda bb, ii: (0, 0)),
            pl.BlockSpec(memory_space=pl.ANY),
        ],
        out_specs=pl.BlockSpec((1, tile, d), lambda bb, ii: (bb, ii, 0)),
        scratch_shapes=[buf, buf, pltpu.SemaphoreType.DMA, pltpu.SemaphoreType.DMA],
        compiler_params=pltpu.CompilerParams(dimension_semantics=("arbitrary", "arbitrary"),
                                             vmem_limit_bytes=_vmem_limit(32)),
        name="combine",
    )(dest, dest, x1, mod, gates, ln2_g, ln2_b, ys)


def kernel(x_prompt, x_sample, c_prompt, c_sample, cache_k, cache_v, state_conv, rel_table, ln0_g, ln0_b, w_ada, b_ada, w_in, b_in, sinks, conv_w, w_oa, w_ob, w_o, ln1_g, ln1_b, w_router, b_router, w_gu, b_gu, w_down, b_down, ln2_g, ln2_b):
    depth = w_ada.shape[0]
    bp, sp, d = x_prompt.shape
    bs, ss, _ = x_sample.shape
    alpha = (2 * depth) ** 0.25
    n_tok = bp * sp + bs * ss
    assert sp % PROMPT_TILE == 0 and ss % DEST_GROUP == 0 and ss <= WINDOW
    assert -(-n_tok // MOE_BLOCK) <= TABLE_LANES
    n_blocks = -(-(n_tok * TOP_K) // MOE_BLOCK) + N_EXPERTS

    mod_all = _ada(jnp.concatenate([c_prompt, c_sample], axis=0), w_ada, b_ada)
    mod_all = mod_all.reshape(depth, bp + bs, 6, d)
    bias = _bias_table(rel_table)
    row = lambda a: a.reshape(1, -1)
    zeros_kv = jnp.zeros((bp, WINDOW, KV_WIDTH), F32)
    zeros_u = jnp.zeros((bp, SUBLANES, CONV_WIDTH), F32)
    state0 = jnp.zeros((N_EXPERTS, LANES), F32)
    table0 = jnp.zeros((N_EXPERTS, TABLE_LANES), F32)
    slot = jnp.arange(n_blocks, dtype=jnp.int32)

    y_p, y_s = x_prompt, x_sample
    outs = {name: [] for name in ("kp", "vp", "up", "ks", "vs", "us")}
    for l in range(depth):
        shared = (w_in[l].astype(BF16), row(b_in[l]))
        tail = (sinks[l], conv_w[l], w_oa[l].astype(BF16), w_ob[l].astype(BF16), w_o[l].astype(BF16),
                row(ln1_g[l]), row(ln1_b[l]), w_router[l].T.astype(BF16), b_router[l].reshape(N_EXPERTS, 1))
        mod_p, mod_s = mod_all[l, :bp], mod_all[l, bp:]
        x1_p, gate_p, dest_p, k_p, v_p, u_p, state, table, xs = _mix(
            y_p, mod_p, row(ln0_g), row(ln0_b), *shared, bias, *tail, zeros_kv, zeros_kv, zeros_u,
            state0, table0, None, tile=PROMPT_TILE, apply_ln0=(l == 0), mask_first=True, alpha=alpha,
            n_blocks=n_blocks, finalize=True)
        u0 = jnp.pad(state_conv[l], ((0, 0), (SUBLANES - (CONV_K - 1), 0), (0, 0)))
        x1_s, gate_s, dest_s, k_s, v_s, u_s, state, table, xs = _mix(
            y_s, mod_s, row(ln0_g), row(ln0_b), *shared, bias, *tail,
            cache_k[l].reshape(bs, WINDOW, KV_WIDTH), cache_v[l].reshape(bs, WINDOW, KV_WIDTH), u0,
            state, table, xs, tile=ss, apply_ln0=(l == 0), mask_first=False, alpha=alpha,
            n_blocks=n_blocks, finalize=False)

        counts = state[:, 0].astype(jnp.int32)
        n_used = state[0:1, 2].astype(jnp.int32)
        n_blk = (counts + MOE_BLOCK - 1) // MOE_BLOCK
        blk_end = jnp.cumsum(n_blk)
        first_blk = blk_end - n_blk
        owner = jnp.minimum(jnp.sum((blk_end[None, :] <= slot[:, None]).astype(jnp.int32), axis=1), N_EXPERTS - 1)
        nth = jnp.clip(slot - first_blk[owner], 0, TABLE_LANES - 1)
        order = jnp.where(slot < n_used, table[owner, nth].astype(jnp.int32), slot)

        ys = _ffn(l, first_blk, n_blk, n_used, order, xs, w_gu, b_gu, w_down, b_down, n_blocks)
        y_p = _combine(dest_p, 0, x1_p, mod_p, gate_p, row(ln2_g[l]), row(ln2_b[l]), ys,
                       tile=COMBINE_TILE, alpha=alpha)
        y_s = _combine(dest_s, 0, x1_s, mod_s, gate_s, row(ln2_g[l]), row(ln2_b[l]), ys,
                       tile=ss, alpha=alpha)

        outs["kp"].append(k_p.reshape(bp, WINDOW, N_KV_HEADS, HEAD_DIM))
        outs["vp"].append(v_p.reshape(bp, WINDOW, N_KV_HEADS, HEAD_DIM))
        outs["up"].append(u_p[:, SUBLANES - (CONV_K - 1):, :])
        outs["ks"].append(k_s.reshape(bs, WINDOW, N_KV_HEADS, HEAD_DIM))
        outs["vs"].append(v_s.reshape(bs, WINDOW, N_KV_HEADS, HEAD_DIM))
        outs["us"].append(u_s[:, SUBLANES - (CONV_K - 1):, :])
    return (y_p, y_s, jnp.stack(outs["kp"]), jnp.stack(outs["vp"]), jnp.stack(outs["up"]),
            jnp.stack(outs["ks"]), jnp.stack(outs["vs"]), jnp.stack(outs["us"]))
```

```python
import functools
import math

import jax
import jax.numpy as jnp
import numpy as np
from jax import lax
from jax.experimental import pallas as pl
from jax.experimental.pallas import tpu as pltpu

D_MODEL = 1024
CHUNK = 64
N_HEADS = 8
N_KV_HEADS = 2
HEAD_DIM = 64
GROUP = N_HEADS // N_KV_HEADS
ATT_WIDTH = N_HEADS * HEAD_DIM
KV_WIDTH = N_KV_HEADS * HEAD_DIM
WINDOW = 128
CONV_WIDTH = 512
CONV_K = 3
NUM_BUCKETS = 32
MAX_DISTANCE = 128
N_EXPERTS = 32
TOP_K = 4
D_FF = 1024
SWIGLU_LIMIT = 7.0
SWIGLU_ALPHA = 1.702
MOE_BLOCK = 512
LN_EPS = 1e-5
NEG_INF = -1e30
IN_SIZES = (ATT_WIDTH, KV_WIDTH, KV_WIDTH, CONV_WIDTH, CONV_WIDTH, CONV_WIDTH, D_MODEL, D_MODEL)
IN_WIDTH = sum(IN_SIZES)
IN_OFFS = tuple(int(s) for s in np.cumsum((0,) + IN_SIZES))

SUBLANES = 8
LANES = 128
ROW_TILES = D_MODEL // LANES
assert ROW_TILES == SUBLANES

PROMPT_TILE = 512
COMBINE_TILE = 512
DEST_GROUP = 64
LAG = 2
RING = 2 * LAG
SPARE_BLOCKS = LAG * PROMPT_TILE * TOP_K // MOE_BLOCK
TABLE_LANES = 256

F32 = jnp.float32
BF16 = jnp.bfloat16
HIGHEST = lax.Precision.HIGHEST
NT_DIMS = (((1,), (1,)), ((), ()))


def _vmem_limit(mib):
    return mib * 1024 * 1024


def _layernorm(x, g, b):
    mu = jnp.mean(x, axis=-1, keepdims=True)
    xc = x - mu
    var = jnp.mean(xc * xc, axis=-1, keepdims=True)
    return xc * lax.rsqrt(var + LN_EPS) * g + b


def _to_row_tiles(ref, x, rows):
    for s in range(ROW_TILES):
        ref[pl.ds(s, rows, stride=ROW_TILES), :] = x[:, s * LANES:(s + 1) * LANES]


def _from_row_tiles(ref, base, rows):
    return jnp.concatenate(
        [ref[pl.ds(base * ROW_TILES + s, rows, stride=ROW_TILES), :] for s in range(ROW_TILES)], axis=-1)


def _ada_kernel(c_ref, w_ref, b_ref, o_ref):
    c = c_ref[...]
    s = c * jax.nn.sigmoid(c)
    o_ref[0] = jnp.dot(s, w_ref[0], precision=HIGHEST, preferred_element_type=F32) + b_ref[0]


def _ada(c_all, w_ada, b_ada):
    depth = w_ada.shape[0]
    nb = c_all.shape[0]
    n_col = 6 * D_MODEL // D_MODEL
    return pl.pallas_call(
        _ada_kernel,
        out_shape=jax.ShapeDtypeStruct((depth, nb, 6 * D_MODEL), F32),
        grid=(depth, n_col),
        in_specs=[
            pl.BlockSpec((nb, D_MODEL), lambda l, j: (0, 0)),
            pl.BlockSpec((1, D_MODEL, D_MODEL), lambda l, j: (l, 0, j)),
            pl.BlockSpec((1, 1, D_MODEL), lambda l, j: (l, 0, j)),
        ],
        out_specs=pl.BlockSpec((1, nb, D_MODEL), lambda l, j: (l, 0, j)),
        compiler_params=pltpu.CompilerParams(dimension_semantics=("arbitrary", "arbitrary"),
                                             vmem_limit_bytes=_vmem_limit(32)),
        name="ada",
    )(c_all, w_ada, b_ada.reshape(depth, 1, 6 * D_MODEL))


def _rel_bucket(rel):
    half = NUM_BUCKETS // 2
    max_exact = half // 2
    n = jnp.abs(rel)
    n_f = jnp.maximum(n, 1).astype(jnp.float32)
    large = max_exact + (jnp.log(n_f / max_exact) / math.log(MAX_DISTANCE / max_exact)
                         * (half - max_exact)).astype(jnp.int32)
    large = jnp.minimum(large, half - 1)
    return jnp.where(rel > 0, half, 0) + jnp.where(n < max_exact, n, large)


BAND = WINDOW + CHUNK
PAIR_ROWS = 2 * CHUNK
PAIR_COLS = 2 * BAND


def _band_codes():
    r = jnp.arange(PAIR_ROWS)[:, None]
    j = jnp.arange(PAIR_COLS)[None, :]
    bucket = _rel_bucket(j % BAND - WINDOW - r % CHUNK)
    head = 2 * (r // CHUNK) + j // BAND
    return (bucket + NUM_BUCKETS * head).astype(jnp.int32)


def _bias_kernel(table_ref, code_ref, o_ref):
    g = pl.program_id(0)
    code = code_ref[...]
    acc = jnp.zeros(code.shape, F32)
    for hq in range(GROUP):
        for i in range(NUM_BUCKETS):
            acc = jnp.where(code == hq * NUM_BUCKETS + i, table_ref[i, g * GROUP + hq], acc)
    o_ref[0] = acc


def _bias_table(rel_table):
    return pl.pallas_call(
        _bias_kernel,
        out_shape=jax.ShapeDtypeStruct((N_KV_HEADS, PAIR_ROWS, PAIR_COLS), F32),
        grid=(N_KV_HEADS,),
        in_specs=[
            pl.BlockSpec(memory_space=pltpu.SMEM),
            pl.BlockSpec((PAIR_ROWS, PAIR_COLS), lambda g: (0, 0)),
        ],
        out_specs=pl.BlockSpec((1, PAIR_ROWS, PAIR_COLS), lambda g: (g, 0, 0)),
        compiler_params=pltpu.CompilerParams(dimension_semantics=("arbitrary",)),
        name="rel_bias",
    )(rel_table, _band_codes())


def _row(ref, r):
    return ref.at[pl.ds(pl.multiple_of(r * ROW_TILES, ROW_TILES), ROW_TILES), :]


def _mix_kernel(*refs, tile, apply_ln0, mask_first, alpha, n_steps, n_blocks, has_xs_in, finalize):
    (x_ref, mod_ref, ln0g_ref, ln0b_ref, win_ref, bin_ref, bias_ref, sink_ref, convw_ref,
     woa_ref, wob_ref, wo_ref, ln1g_ref, ln1b_ref, wr_ref, br_ref, k0_ref, v0_ref, u0_ref,
     upto_ref, st0_ref, tab0_ref) = refs[:22]
    refs = refs[22 + (1 if has_xs_in else 0):]
    (x1_ref, gate_ref, dest_ref, newk_ref, newv_ref, newu_ref, st_ref, tab_ref, xs_ref,
     kc_ref, vc_ref, ubuf_ref, run_ref, cur_ref, nfree_ref, h2buf, dest_v, dest_s, fin_v, fin_s,
     ssem, dsem, zsem) = refs
    i = pl.program_id(1)
    t = tile
    lin = pl.program_id(0) * pl.num_programs(1) + i
    now = lax.rem(lin, RING)
    src = lax.rem(lin + RING - LAG, RING)
    slot_rows = t * ROW_TILES

    def dest_copy(slot):
        return pltpu.make_async_copy(dest_v.at[slot], dest_s.at[slot], dsem.at[slot])

    def scatter_wait(slot):
        for _ in range(TOP_K):
            pltpu.make_async_copy(h2buf.at[slot], xs_ref.at[pl.ds(0, slot_rows), :], ssem.at[slot]).wait()

    def scatter_row(slot, tok):
        for k in range(TOP_K):
            pltpu.make_async_copy(_row(h2buf.at[slot], tok), _row(xs_ref, dest_s[slot, k, tok]),
                                  ssem.at[slot]).start(priority=k % 2)

    @pl.when(lin == 0)
    def _():
        run_ref[...] = st0_ref[:, 0:1]
        cur_ref[...] = st0_ref[:, 1:2]
        nfree_ref[...] = st0_ref[0:1, 2:3]
        tab_ref[...] = tab0_ref[...]
        for back in range(1, LAG + 1):
            h2buf[RING - back] = jnp.zeros(h2buf.shape[1:], F32)
            for k in range(TOP_K):
                def fill(tok, c, k=k, back=back):
                    dest_s[RING - back, k, tok] = n_blocks * MOE_BLOCK + ((back - 1) * TOP_K + k) * t + tok
                    return c
                lax.fori_loop(0, t, fill, 0)

    @pl.when(lin >= LAG)
    def _():
        dest_copy(src).wait()
        scatter_wait(now)

    @pl.when(i == 0)
    def _():
        kc_ref[...] = k0_ref[0]
        vc_ref[...] = v0_ref[0]
        ubuf_ref[0:SUBLANES, :] = u0_ref[0]

    for tok in range(t):
        scatter_row(src, tok)

    x = x_ref[0]
    if apply_ln0:
        x = _layernorm(x, ln0g_ref[...], ln0b_ref[...])
    mod = mod_ref[0]
    sh1, sc1, g1, sh2, sc2, g2 = [mod[j:j + 1, :] for j in range(6)]
    h = (x * (1.0 + sc1) + sh1).astype(BF16)

    def proj(j0, j1):
        lo, hi = IN_OFFS[j0], IN_OFFS[j1]
        return jnp.dot(h, win_ref[:, lo:hi], preferred_element_type=F32) + bin_ref[:, lo:hi]

    q = proj(0, 1)
    kv = proj(1, 3)
    kfull = jnp.concatenate([kc_ref[...], kv[:, :KV_WIDTH]], axis=0)
    vfull = jnp.concatenate([vc_ref[...], kv[:, KV_WIDTH:]], axis=0)
    kc_ref[...] = kfull[t:, :]
    vc_ref[...] = vfull[t:, :]
    newk_ref[0] = kfull[t:, :]
    newv_ref[0] = vfull[t:, :]
    low = lax.broadcasted_iota(jnp.int32, kfull.shape, 1) < HEAD_DIM
    k_sw = pltpu.roll(kfull, HEAD_DIM, axis=1)
    v_sw = pltpu.roll(vfull, HEAD_DIM, axis=1)
    k_even = [jnp.where(low, kfull, 0.0).astype(BF16), jnp.where(low, k_sw, 0.0).astype(BF16)]
    k_odd = [jnp.where(low, 0.0, k_sw).astype(BF16), jnp.where(low, 0.0, kfull).astype(BF16)]
    v_even = [jnp.where(low, vfull, 0.0).astype(BF16), jnp.where(low, v_sw, 0.0).astype(BF16)]
    v_odd = [jnp.where(low, 0.0, v_sw).astype(BF16), jnp.where(low, 0.0, vfull).astype(BF16)]
    col = lax.broadcasted_iota(jnp.int32, (PAIR_ROWS, PAIR_COLS), 1)
    even = col < BAND
    band_col = jnp.where(even, col, col - BAND)
    first_pair = lax.broadcasted_iota(jnp.int32, (PAIR_ROWS, 1), 0) < CHUNK
    out_low = lax.broadcasted_iota(jnp.int32, (PAIR_ROWS, LANES), 1) < HEAD_DIM
    sink_even = [jnp.where(first_pair, sink_ref[g * GROUP], sink_ref[g * GROUP + 2]) for g in range(N_KV_HEADS)]
    sink_odd = [jnp.where(first_pair, sink_ref[g * GROUP + 1], sink_ref[g * GROUP + 3])
                for g in range(N_KV_HEADS)]
    units = [(c, g) for c in range(t // CHUNK) for g in range(N_KV_HEADS)]
    scores = []
    for c, g in units:
        r0 = c * CHUNK
        ql = jnp.concatenate([q[r0:r0 + CHUNK, (2 * g) * LANES:(2 * g + 1) * LANES],
                              q[r0:r0 + CHUNK, (2 * g + 1) * LANES:(2 * g + 2) * LANES]], axis=0)
        kp = jnp.concatenate([k_even[g][r0:r0 + BAND], k_odd[g][r0:r0 + BAND]], axis=0)
        s = lax.dot_general(ql.astype(BF16), kp, NT_DIMS, preferred_element_type=F32)
        s = s * (HEAD_DIM ** -0.5) + bias_ref[g]
        if mask_first and c < WINDOW // CHUNK:
            s = jnp.where(band_col < jnp.where(i == 0, WINDOW - r0, 0), NEG_INF, s)
        scores.append(s)
    maxes = []
    for (c, g), s in zip(units, scores):
        m_e = jnp.maximum(jnp.max(jnp.where(even, s, -jnp.inf), axis=-1, keepdims=True), sink_even[g])
        m_o = jnp.maximum(jnp.max(jnp.where(even, -jnp.inf, s), axis=-1, keepdims=True), sink_odd[g])
        maxes.append((m_e, m_o))
    exps, scales = [], []
    for (c, g), s, (m_e, m_o) in zip(units, scores, maxes):
        e = jnp.exp(s - jnp.where(even, m_e, m_o))
        d_e = jnp.sum(jnp.where(even, e, 0.0), axis=-1, keepdims=True) + jnp.exp(sink_even[g] - m_e)
        d_o = jnp.sum(jnp.where(even, 0.0, e), axis=-1, keepdims=True) + jnp.exp(sink_odd[g] - m_o)
        exps.append(e.astype(BF16))
        scales.append(jnp.where(out_low, 1.0 / d_e, 1.0 / d_o))
    outs = {}
    for (c, g), e, scale in zip(units, exps, scales):
        r0 = c * CHUNK
        vp = jnp.concatenate([v_even[g][r0:r0 + BAND], v_odd[g][r0:r0 + BAND]], axis=0)
        outs[c, g] = jnp.dot(e, vp, preferred_element_type=F32) * scale
    ya = jnp.concatenate(
        [jnp.concatenate([outs[c, g][half * CHUNK:(half + 1) * CHUNK] for g in range(N_KV_HEADS)
                          for half in range(2)], axis=-1) for c in range(t // CHUNK)], axis=0)

    cb = proj(3, 4)
    u = proj(4, 5) * proj(5, 6)
    ubuf_ref[SUBLANES:t + SUBLANES, :] = u
    cw = convw_ref[...]
    yc = (cw[0:1, :] * ubuf_ref[SUBLANES - 2:t + SUBLANES - 2, :]
          + cw[1:2, :] * ubuf_ref[SUBLANES - 1:t + SUBLANES - 1, :] + cw[2:3, :] * u)
    yb = cb * yc
    tail = ubuf_ref[t:t + SUBLANES, :]
    newu_ref[0] = tail
    ubuf_ref[0:SUBLANES, :] = tail

    a_out = jnp.dot(ya.astype(BF16), woa_ref[...], preferred_element_type=F32)
    b_out = jnp.dot(yb.astype(BF16), wob_ref[...], preferred_element_type=F32)
    mixin = jax.nn.sigmoid(proj(6, 7)) * a_out + jax.nn.sigmoid(proj(7, 8)) * b_out
    mix = jnp.dot(mixin.astype(BF16), wo_ref[...], preferred_element_type=F32)
    x1 = _layernorm(alpha * x + (1.0 + g1) * mix, ln1g_ref[...], ln1b_ref[...])
    x1_ref[0] = x1

    h2 = x1 * (1.0 + sc2) + sh2
    _to_row_tiles(h2buf.at[now], h2, t)
    logits = lax.dot_general(wr_ref[...], h2.astype(BF16), NT_DIMS, preferred_element_type=F32) + br_ref[...]
    eid = lax.broadcasted_iota(jnp.int32, (N_EXPERTS, t), 0)
    vals, ids = [], []
    for _ in range(TOP_K):
        mx = jnp.max(logits, axis=0, keepdims=True)
        sel = jnp.min(jnp.where(logits == mx, eid, N_EXPERTS), axis=0, keepdims=True)
        vals.append(mx)
        ids.append(sel)
        logits = jnp.where(eid == sel, -jnp.inf, logits)
    ex = [jnp.exp(v - vals[0]) for v in vals]
    tot = ex[0] + ex[1] + ex[2] + ex[3]
    gate_ref[0] = jnp.concatenate([e_ / tot for e_ in ex], axis=0)

    per_block = 1.0 / MOE_BLOCK
    hits = [(eid == ids[k]).astype(F32) for k in range(TOP_K)]
    hit = hits[0] + hits[1] + hits[2] + hits[3]
    count = jnp.sum(hit, axis=1, keepdims=True)
    incl = jnp.dot(hit.astype(BF16), upto_ref[...], preferred_element_type=F32)
    run = run_ref[...]
    cur = cur_ref[...]
    run_blk = jnp.floor(run * per_block)
    is_open = (run - run_blk * MOE_BLOCK > 0).astype(F32)
    end = run + count
    blocks_before = jnp.ceil(run * per_block)
    n_new = jnp.ceil(end * per_block) - blocks_before
    er = lax.broadcasted_iota(jnp.int32, (N_EXPERTS, N_EXPERTS), 0)
    ec = lax.broadcasted_iota(jnp.int32, (N_EXPERTS, N_EXPERTS), 1)
    earlier = jnp.dot((ec < er).astype(BF16), jnp.broadcast_to(n_new, (N_EXPERTS, LANES)).astype(BF16),
                      preferred_element_type=F32)[:, 0:1]
    base = nfree_ref[...] + earlier
    r = run + incl - hit
    r_blk = jnp.floor(r * per_block)
    ordinal = r_blk - run_blk
    block = jnp.where(jnp.logical_and(is_open > 0, ordinal == 0), cur, base + ordinal - is_open)
    row = block * MOE_BLOCK + (r - r_blk * MOE_BLOCK)
    dests = [jnp.sum(hits[k] * row, axis=0, keepdims=True).astype(jnp.int32) for k in range(TOP_K)]
    dest_v[now] = jnp.concatenate(dests, axis=0)
    for k in range(TOP_K):
        for gidx in range(t // DEST_GROUP):
            dest_ref[gidx, k:k + 1, :] = dests[k][:, gidx * DEST_GROUP:(gidx + 1) * DEST_GROUP]
    lane = lax.broadcasted_iota(jnp.int32, tab_ref.shape, 1).astype(F32)
    table = tab_ref[...]
    for j in range(-(-t // MOE_BLOCK)):
        table = jnp.where(jnp.logical_and(lane == blocks_before + j, n_new > j), base + j, table)
    tab_ref[...] = table
    cur = jnp.where(n_new > 0, base + n_new - 1, cur)
    nfree = nfree_ref[...] + jnp.sum(n_new, axis=0, keepdims=True)
    run_ref[...] = end
    cur_ref[...] = cur
    nfree_ref[...] = nfree
    st_lane = lax.broadcasted_iota(jnp.int32, st_ref.shape, 1)
    st_ref[...] = jnp.where(st_lane == 0, end, jnp.where(st_lane == 1, cur, nfree))
    dest_copy(now).start()

    @pl.when(lin == n_steps - 1)
    def _():
        for back in range(LAG):
            late = lax.rem(lin + RING - back, RING)
            dest_copy(late).wait()
            lax.fori_loop(0, t, lambda tok, c, late=late: (scatter_row(late, tok), c)[1], 0)
        for slot in range(RING):
            scatter_wait(slot)
        if finalize:
            rem = end - jnp.floor(end * per_block) * MOE_BLOCK
            tail_lo = cur * MOE_BLOCK + rem
            tail_hi = jnp.where(rem > 0, (cur + 1.0) * MOE_BLOCK, tail_lo)
            fin_v[...] = jnp.where(st_lane == 0, tail_lo, jnp.where(st_lane == 1, tail_hi, nfree)).astype(jnp.int32)
            fin = pltpu.make_async_copy(fin_v, fin_s, dsem.at[0])
            fin.start()
            fin.wait()
            h2buf[0] = jnp.zeros(h2buf.shape[1:], F32)

            def zero_rows(r_, n_rows):
                return pltpu.make_async_copy(
                    h2buf.at[0].at[pl.ds(0, n_rows * ROW_TILES), :],
                    xs_ref.at[pl.ds(pl.multiple_of(r_ * ROW_TILES, ROW_TILES), n_rows * ROW_TILES), :], zsem)

            piece = min(slot_rows, MOE_BLOCK * ROW_TILES)

            def zero_part(c_):
                return pltpu.make_async_copy(
                    h2buf.at[0].at[pl.ds(0, piece), :],
                    xs_ref.at[pl.ds(pl.multiple_of(c_ * piece, piece), piece), :], zsem)

            def per_expert(e, carry):
                lo_, hi_ = fin_s[e, 0], fin_s[e, 1]
                n_big = lax.div(hi_ - lo_, DEST_GROUP)
                mid = lo_ + n_big * DEST_GROUP
                for act in ("start", "wait"):
                    lax.fori_loop(0, n_big, lambda j, c, act=act: (
                        getattr(zero_rows(lo_ + j * DEST_GROUP, DEST_GROUP), act)(), c)[1], 0)
                    lax.fori_loop(mid, hi_, lambda r_, c, act=act: (getattr(zero_rows(r_, 1), act)(), c)[1], 0)
                return carry

            lax.fori_loop(0, N_EXPERTS, per_expert, 0)
            parts = MOE_BLOCK * ROW_TILES // piece
            lo, hi = fin_s[0, 2] * parts, n_blocks * parts
            lax.fori_loop(lo, hi, lambda c_, c: (zero_part(c_).start(), c)[1], 0)
            lax.fori_loop(lo, hi, lambda c_, c: (zero_part(c_).wait(), c)[1], 0)


def _mix(x, mod, ln0_g, ln0_b, w_in, b_in, bias, sink, conv_w, w_oa, w_ob, w_o, ln1_g, ln1_b,
         w_rt, b_r, k0, v0, u0, state, table, xs_in, *, tile, apply_ln0, mask_first, alpha, n_blocks, finalize):
    b, s, d = x.shape
    n_t = s // tile
    assert b * n_t >= RING
    const = lambda shape: pl.BlockSpec(shape, lambda bb, ii: (0,) * len(shape), pipeline_mode=pl.Buffered(1))
    per_b = lambda shape: pl.BlockSpec((1,) + shape, lambda bb, ii: (bb,) + (0,) * len(shape))
    whole = lambda shape: pl.BlockSpec(shape, lambda bb, ii: (0,) * len(shape))
    kern = functools.partial(_mix_kernel, tile=tile, apply_ln0=apply_ln0, mask_first=mask_first, alpha=alpha,
                             n_steps=b * n_t, n_blocks=n_blocks, has_xs_in=xs_in is not None, finalize=finalize)
    upto = (jnp.arange(tile)[:, None] <= jnp.arange(tile)[None, :]).astype(BF16)
    xs_rows = (n_blocks + SPARE_BLOCKS) * MOE_BLOCK * ROW_TILES
    groups = tile // DEST_GROUP
    operands = [x, mod, ln0_g, ln0_b, w_in, b_in, bias, sink, conv_w, w_oa, w_ob, w_o, ln1_g, ln1_b, w_rt, b_r,
                k0, v0, u0, upto, state, table]
    in_specs = [
        pl.BlockSpec((1, tile, d), lambda bb, ii: (bb, ii, 0)),
        per_b((6, d)),
        const((1, d)), const((1, d)),
        const((d, IN_WIDTH)), const((1, IN_WIDTH)),
        const((N_KV_HEADS, PAIR_ROWS, PAIR_COLS)),
        pl.BlockSpec(memory_space=pltpu.SMEM),
        const((CONV_K, CONV_WIDTH)),
        const((ATT_WIDTH, d)), const((CONV_WIDTH, d)), const((d, d)),
        const((1, d)), const((1, d)),
        const((N_EXPERTS, d)), const((N_EXPERTS, 1)),
        per_b((WINDOW, KV_WIDTH)), per_b((WINDOW, KV_WIDTH)), per_b((SUBLANES, CONV_WIDTH)),
        const((tile, tile)), const((N_EXPERTS, LANES)), const((N_EXPERTS, TABLE_LANES)),
    ]
    aliases = {}
    if xs_in is not None:
        operands.append(xs_in)
        in_specs.append(pl.BlockSpec(memory_space=pl.ANY))
        aliases = {len(operands) - 1: 8}
    return pl.pallas_call(
        kern,
        out_shape=(
            jax.ShapeDtypeStruct((b, s, d), F32),
            jax.ShapeDtypeStruct((b * n_t, TOP_K, tile), F32),
            jax.ShapeDtypeStruct((b * s // DEST_GROUP, TOP_K, DEST_GROUP), jnp.int32),
            jax.ShapeDtypeStruct((b, WINDOW, KV_WIDTH), F32),
            jax.ShapeDtypeStruct((b, WINDOW, KV_WIDTH), F32),
            jax.ShapeDtypeStruct((b, SUBLANES, CONV_WIDTH), F32),
            jax.ShapeDtypeStruct((N_EXPERTS, LANES), F32),
            jax.ShapeDtypeStruct((N_EXPERTS, TABLE_LANES), F32),
            jax.ShapeDtypeStruct((xs_rows, LANES), F32),
        ),
        grid=(b, n_t),
        in_specs=in_specs,
        out_specs=(
            pl.BlockSpec((1, tile, d), lambda bb, ii: (bb, ii, 0)),
            pl.BlockSpec((1, TOP_K, tile), lambda bb, ii: (bb * n_t + ii, 0, 0)),
            pl.BlockSpec((groups, TOP_K, DEST_GROUP), lambda bb, ii: (bb * n_t + ii, 0, 0)),
            per_b((WINDOW, KV_WIDTH)), per_b((WINDOW, KV_WIDTH)), per_b((SUBLANES, CONV_WIDTH)),
            whole((N_EXPERTS, LANES)), whole((N_EXPERTS, TABLE_LANES)),
            pl.BlockSpec(memory_space=pl.ANY),
        ),
        scratch_shapes=[
            pltpu.VMEM((WINDOW, KV_WIDTH), F32),
            pltpu.VMEM((WINDOW, KV_WIDTH), F32),
            pltpu.VMEM((tile + SUBLANES, CONV_WIDTH), F32),
            pltpu.VMEM((N_EXPERTS, 1), F32),
            pltpu.VMEM((N_EXPERTS, 1), F32),
            pltpu.VMEM((1, 1), F32),
            pltpu.VMEM((RING, tile * ROW_TILES, LANES), F32),
            pltpu.VMEM((RING, TOP_K, tile), jnp.int32),
            pltpu.SMEM((RING, TOP_K, tile), jnp.int32),
            pltpu.VMEM((N_EXPERTS, LANES), jnp.int32),
            pltpu.SMEM((N_EXPERTS, LANES), jnp.int32),
            pltpu.SemaphoreType.DMA((RING,)),
            pltpu.SemaphoreType.DMA((RING,)),
            pltpu.SemaphoreType.DMA,
        ],
        input_output_aliases=aliases,
        compiler_params=pltpu.CompilerParams(dimension_semantics=("arbitrary", "arbitrary"),
                                             vmem_limit_bytes=_vmem_limit(56)),
        name="mix",
    )(*operands)


def _ffn_kernel(first_ref, nblk_ref, nused_ref, order_ref, xs_ref, wgu_ref, bgu_ref, wd_ref, bd_ref, ys_ref,
                wgu_bf, wd_bf, xbuf, ybuf, in_sem, out_sem, *, n_blocks):
    e = pl.program_id(0)
    first = first_ref[e]
    n_used = nused_ref[0]
    block_rows = MOE_BLOCK * ROW_TILES

    def blk(ref, b):
        return ref.at[pl.ds(pl.multiple_of(b * block_rows, block_rows), block_rows), :]

    def in_copy(b, slot):
        return pltpu.make_async_copy(blk(xs_ref, order_ref[b]), xbuf.at[slot], in_sem.at[slot])

    def out_copy(b, slot):
        return pltpu.make_async_copy(ybuf.at[slot], blk(ys_ref, order_ref[b]), out_sem.at[slot])

    @pl.when(e == 0)
    def _():
        in_copy(0, 0).start()

    @pl.when(nblk_ref[e] > 0)
    def _():
        wgu_bf[...] = wgu_ref[0, 0].astype(BF16)
        wd_bf[...] = wd_ref[0, 0].astype(BF16)

    def body(j, carry):
        b = first + j
        slot = lax.rem(b, 2)
        in_copy(b, slot).wait()

        @pl.when(b + 1 < n_used)
        def _():
            in_copy(b + 1, 1 - slot).start()

        @pl.when(b >= 2)
        def _():
            out_copy(b - 2, slot).wait()

        x = _from_row_tiles(xbuf.at[slot], 0, MOE_BLOCK).astype(BF16)
        gu = jnp.dot(x, wgu_bf[...], preferred_element_type=F32) + bgu_ref[0, 0]
        g = jnp.minimum(gu[:, :D_FF], SWIGLU_LIMIT)
        lin = jnp.clip(gu[:, D_FF:], -SWIGLU_LIMIT, SWIGLU_LIMIT)
        a = g * jax.nn.sigmoid(SWIGLU_ALPHA * g) * (lin + 1.0)
        y = jnp.dot(a.astype(BF16), wd_bf[...], preferred_element_type=F32) + bd_ref[0, 0]
        _to_row_tiles(ybuf.at[slot], y, MOE_BLOCK)
        out_copy(b, slot).start()
        return carry

    lax.fori_loop(0, nblk_ref[e], body, 0)

    @pl.when(e == N_EXPERTS - 1)
    def _():
        @pl.when(n_used >= 2)
        def _():
            out_copy(n_used - 2, lax.rem(n_used, 2)).wait()

        out_copy(n_used - 1, lax.rem(n_used - 1, 2)).wait()
        ybuf[0] = jnp.zeros(ybuf.shape[1:], F32)
        lax.fori_loop(n_used, n_blocks, lambda b, c: (out_copy(b, 0).start(), c)[1], 0)
        lax.fori_loop(n_used, n_blocks, lambda b, c: (out_copy(b, 0).wait(), c)[1], 0)


def _ffn(layer, first_blk, n_blk, n_used, order, xs, w_gu, b_gu, w_down, b_down, n_blocks):
    block_rows = MOE_BLOCK * ROW_TILES
    depth = w_gu.shape[0]

    def expert(e, *_):
        return (layer, e, 0, 0)

    return pl.pallas_call(
        functools.partial(_ffn_kernel, n_blocks=n_blocks),
        out_shape=jax.ShapeDtypeStruct((n_blocks * block_rows, LANES), F32),
        grid_spec=pltpu.PrefetchScalarGridSpec(
            num_scalar_prefetch=4,
            grid=(N_EXPERTS,),
            in_specs=[
                pl.BlockSpec(memory_space=pl.ANY),
                pl.BlockSpec((1, 1, D_MODEL, 2 * D_FF), expert),
                pl.BlockSpec((1, 1, 1, 2 * D_FF), expert),
                pl.BlockSpec((1, 1, D_FF, D_MODEL), expert),
                pl.BlockSpec((1, 1, 1, D_MODEL), expert),
            ],
            out_specs=pl.BlockSpec(memory_space=pl.ANY),
            scratch_shapes=[
                pltpu.VMEM((D_MODEL, 2 * D_FF), BF16),
                pltpu.VMEM((D_FF, D_MODEL), BF16),
                pltpu.VMEM((2, block_rows, LANES), F32),
                pltpu.VMEM((2, block_rows, LANES), F32),
                pltpu.SemaphoreType.DMA((2,)),
                pltpu.SemaphoreType.DMA((2,)),
            ],
        ),
        compiler_params=pltpu.CompilerParams(dimension_semantics=("arbitrary",),
                                             vmem_limit_bytes=_vmem_limit(56)),
        name="ffn",
    )(first_blk, n_blk, n_used, order, xs, w_gu, b_gu.reshape(depth, N_EXPERTS, 1, 2 * D_FF), w_down,
      b_down.reshape(depth, N_EXPERTS, 1, D_MODEL))


def _combine_kernel(dest_ref, next_ref, x1_ref, mod_ref, gate_ref, ln2g_ref, ln2b_ref, ys_ref, o_ref,
                    buf0, buf1, sem0, sem1, *, tile, alpha, n_steps):
    t = tile
    step = pl.program_id(0) * pl.num_programs(1) + pl.program_id(1)

    def fetch(d_ref, g, tt, buf, sem):
        tok = g * DEST_GROUP + tt
        for k in range(TOP_K):
            pltpu.make_async_copy(_row(ys_ref, d_ref[g, k, tt]), _row(buf, k * t + tok),
                                  sem).start(priority=k % 2)

    def wait(buf, sem):
        pltpu.make_async_copy(ys_ref.at[pl.ds(0, TOP_K * t * ROW_TILES), :], buf, sem).wait()

    @pl.when(step == 0)
    def _():
        for g in range(t // DEST_GROUP):
            lax.fori_loop(0, DEST_GROUP, lambda tt, c, g=g: (fetch(dest_ref, g, tt, buf0, sem0), c)[1], 0)

    def phase(buf, sem, nxt_buf, nxt_sem):
        wait(buf, sem)
        for g in range(t // DEST_GROUP):
            for tt in range(DEST_GROUP):
                fetch(next_ref, g, tt, nxt_buf, nxt_sem)
        gates = gate_ref[0]
        gates_t = jnp.transpose(jnp.concatenate([gates, jnp.zeros_like(gates)], axis=0))
        ff = jnp.zeros((t, D_MODEL), F32)
        for k in range(TOP_K):
            ff = ff + gates_t[:, k:k + 1] * _from_row_tiles(buf, k * t, t)
        g2 = mod_ref[0][5:6, :]
        o_ref[0] = _layernorm(alpha * x1_ref[0] + (1.0 + g2) * ff, ln2g_ref[...], ln2b_ref[...])

    @pl.when(step % 2 == 0)
    def _():
        phase(buf0, sem0, buf1, sem1)

    @pl.when(step % 2 == 1)
    def _():
        phase(buf1, sem1, buf0, sem0)

    @pl.when(step == n_steps - 1)
    def _():
        if (n_steps - 1) % 2 == 0:
            wait(buf1, sem1)
        else:
            wait(buf0, sem0)


def _combine(dest, group_offset, x1, mod, gates, ln2_g, ln2_b, ys, *, tile, alpha):
    b, s, d = x1.shape
    n_t = s // tile
    n_steps = b * n_t
    groups = tile // DEST_GROUP
    goff = group_offset // groups
    per_gate_row = gates.shape[2] // tile
    kern = functools.partial(_combine_kernel, tile=tile, alpha=alpha, n_steps=n_steps)
    buf = pltpu.VMEM((TOP_K * tile * ROW_TILES, LANES), F32)
    return pl.pallas_call(
        kern,
        out_shape=jax.ShapeDtypeStruct((b, s, d), F32),
        grid=(b, n_t),
        in_specs=[
            pl.BlockSpec((groups, TOP_K, DEST_GROUP), lambda bb, ii: (goff + bb * n_t + ii, 0, 0),
                         memory_space=pltpu.SMEM),
            pl.BlockSpec((groups, TOP_K, DEST_GROUP),
                         lambda bb, ii: (goff + jnp.minimum(bb * n_t + ii + 1, n_steps - 1), 0, 0),
                         memory_space=pltpu.SMEM),
            pl.BlockSpec((1, tile, d), lambda bb, ii: (bb, ii, 0)),
            pl.BlockSpec((1, 6, d), lambda bb, ii: (bb, 0, 0)),
            pl.BlockSpec((1, TOP_K, tile), lambda bb, ii: ((bb * n_t + ii) // per_gate_row, 0,
                                                           (bb * n_t + ii) % per_gate_row)),
            pl.BlockSpec((1, d), lambda bb, ii: (0, 0)),
            pl.BlockSpec((1, d), lambda bb, ii: (0, 0)),
            pl.BlockSpec(memory_space=pl.ANY),
        ],
        out_specs=pl.BlockSpec((1, tile, d), lambda bb, ii: (bb, ii, 0)),
        scratch_shapes=[buf, buf, pltpu.SemaphoreType.DMA, pltpu.SemaphoreType.DMA],
        compiler_params=pltpu.CompilerParams(dimension_semantics=("arbitrary", "arbitrary"),
                                             vmem_limit_bytes=_vmem_limit(48)),
        name="combine",
    )(dest, dest, x1, mod, gates, ln2_g, ln2_b, ys)


def kernel(x_prompt, x_sample, c_prompt, c_sample, cache_k, cache_v, state_conv, rel_table, ln0_g, ln0_b, w_ada, b_ada, w_in, b_in, sinks, conv_w, w_oa, w_ob, w_o, ln1_g, ln1_b, w_router, b_router, w_gu, b_gu, w_down, b_down, ln2_g, ln2_b):
    depth = w_ada.shape[0]
    bp, sp, d = x_prompt.shape
    bs, ss, _ = x_sample.shape
    alpha = (2 * depth) ** 0.25
    n_tok = bp * sp + bs * ss
    assert sp % PROMPT_TILE == 0 and ss % DEST_GROUP == 0 and ss <= WINDOW
    assert -(-n_tok // MOE_BLOCK) <= TABLE_LANES
    n_blocks = -(-(n_tok * TOP_K) // MOE_BLOCK) + N_EXPERTS

    mod_all = _ada(jnp.concatenate([c_prompt, c_sample], axis=0), w_ada, b_ada)
    mod_all = mod_all.reshape(depth, bp + bs, 6, d)
    bias = _bias_table(rel_table)
    row = lambda a: a.reshape(1, -1)
    zeros_kv = jnp.zeros((bp, WINDOW, KV_WIDTH), F32)
    zeros_u = jnp.zeros((bp, SUBLANES, CONV_WIDTH), F32)
    state0 = jnp.zeros((N_EXPERTS, LANES), F32)
    table0 = jnp.zeros((N_EXPERTS, TABLE_LANES), F32)
    slot = jnp.arange(n_blocks, dtype=jnp.int32)

    y_p, y_s = x_prompt, x_sample
    outs = {name: [] for name in ("kp", "vp", "up", "ks", "vs", "us")}
    for l in range(depth):
        shared = (w_in[l].astype(BF16), row(b_in[l]))
        tail = (sinks[l], conv_w[l], w_oa[l].astype(BF16), w_ob[l].astype(BF16), w_o[l].astype(BF16),
                row(ln1_g[l]), row(ln1_b[l]), w_router[l].T.astype(BF16), b_router[l].reshape(N_EXPERTS, 1))
        mod_p, mod_s = mod_all[l, :bp], mod_all[l, bp:]
        x1_p, gate_p, dest_p, k_p, v_p, u_p, state, table, xs = _mix(
            y_p, mod_p, row(ln0_g), row(ln0_b), *shared, bias, *tail, zeros_kv, zeros_kv, zeros_u,
            state0, table0, None, tile=PROMPT_TILE, apply_ln0=(l == 0), mask_first=True, alpha=alpha,
            n_blocks=n_blocks, finalize=True)
        u0 = jnp.pad(state_conv[l], ((0, 0), (SUBLANES - (CONV_K - 1), 0), (0, 0)))
        x1_s, gate_s, dest_s, k_s, v_s, u_s, state, table, xs = _mix(
            y_s, mod_s, row(ln0_g), row(ln0_b), *shared, bias, *tail,
            cache_k[l].reshape(bs, WINDOW, KV_WIDTH), cache_v[l].reshape(bs, WINDOW, KV_WIDTH), u0,
            state, table, xs, tile=ss, apply_ln0=(l == 0), mask_first=False, alpha=alpha,
            n_blocks=n_blocks, finalize=False)

        counts = state[:, 0].astype(jnp.int32)
        n_used = state[0:1, 2].astype(jnp.int32)
        n_blk = (counts + MOE_BLOCK - 1) // MOE_BLOCK
        blk_end = jnp.cumsum(n_blk)
        first_blk = blk_end - n_blk
        owner = jnp.minimum(jnp.sum((blk_end[None, :] <= slot[:, None]).astype(jnp.int32), axis=1), N_EXPERTS - 1)
        nth = jnp.clip(slot - first_blk[owner], 0, TABLE_LANES - 1)
        order = jnp.where(slot < n_used, table[owner, nth].astype(jnp.int32), slot)

        ys = _ffn(l, first_blk, n_blk, n_used, order, xs, w_gu, b_gu, w_down, b_down, n_blocks)
        y_p = _combine(dest_p, 0, x1_p, mod_p, gate_p, row(ln2_g[l]), row(ln2_b[l]), ys,
                       tile=COMBINE_TILE, alpha=alpha)
        y_s = _combine(dest_s, 0, x1_s, mod_s, gate_s, row(ln2_g[l]), row(ln2_b[l]), ys,
                       tile=ss, alpha=alpha)

        outs["kp"].append(k_p.reshape(bp, WINDOW, N_KV_HEADS, HEAD_DIM))
        outs["vp"].append(v_p.reshape(bp, WINDOW, N_KV_HEADS, HEAD_DIM))
        outs["up"].append(u_p[:, SUBLANES - (CONV_K - 1):, :])
        outs["ks"].append(k_s.reshape(bs, WINDOW, N_KV_HEADS, HEAD_DIM))
        outs["vs"].append(v_s.reshape(bs, WINDOW, N_KV_HEADS, HEAD_DIM))
        outs["us"].append(u_s[:, SUBLANES - (CONV_K - 1):, :])
    return (y_p, y_s, jnp.stack(outs["kp"]), jnp.stack(outs["vp"]), jnp.stack(outs["up"]),
            jnp.stack(outs["ks"]), jnp.stack(outs["vs"]), jnp.stack(outs["us"]))
```

```python
import functools
import math

import jax
import jax.numpy as jnp
import numpy as np
from jax import lax
from jax.experimental import pallas as pl
from jax.experimental.pallas import tpu as pltpu

D_MODEL = 1024
CHUNK = 64
N_HEADS = 8
N_KV_HEADS = 2
HEAD_DIM = 64
GROUP = N_HEADS // N_KV_HEADS
ATT_WIDTH = N_HEADS * HEAD_DIM
KV_WIDTH = N_KV_HEADS * HEAD_DIM
WINDOW = 128
CONV_WIDTH = 512
CONV_K = 3
NUM_BUCKETS = 32
MAX_DISTANCE = 128
N_EXPERTS = 32
TOP_K = 4
D_FF = 1024
SWIGLU_LIMIT = 7.0
SWIGLU_ALPHA = 1.702
MOE_BLOCK = 512
LN_EPS = 1e-5
NEG_INF = -1e30
IN_SIZES = (ATT_WIDTH, KV_WIDTH, KV_WIDTH, CONV_WIDTH, CONV_WIDTH, CONV_WIDTH, D_MODEL, D_MODEL)
IN_WIDTH = sum(IN_SIZES)
IN_OFFS = tuple(int(s) for s in np.cumsum((0,) + IN_SIZES))

SUBLANES = 8
LANES = 128
ROW_TILES = D_MODEL // LANES
assert ROW_TILES == SUBLANES

PROMPT_TILE = 512
COMBINE_TILE = 512
DEST_GROUP = 64
LAG = 2
RING = 2 * LAG
SPARE_BLOCKS = LAG * PROMPT_TILE * TOP_K // MOE_BLOCK
TABLE_LANES = 256

F32 = jnp.float32
BF16 = jnp.bfloat16
HIGHEST = lax.Precision.HIGHEST
NT_DIMS = (((1,), (1,)), ((), ()))


def _vmem_limit(mib):
    return mib * 1024 * 1024


def _layernorm(x, g, b):
    mu = jnp.mean(x, axis=-1, keepdims=True)
    xc = x - mu
    var = jnp.mean(xc * xc, axis=-1, keepdims=True)
    return xc * lax.rsqrt(var + LN_EPS) * g + b


def _to_row_tiles(ref, x, rows):
    for s in range(ROW_TILES):
        ref[pl.ds(s, rows, stride=ROW_TILES), :] = x[:, s * LANES:(s + 1) * LANES]


def _from_row_tiles(ref, base, rows):
    return jnp.concatenate(
        [ref[pl.ds(base * ROW_TILES + s, rows, stride=ROW_TILES), :] for s in range(ROW_TILES)], axis=-1)


def _ada_kernel(c_ref, w_ref, b_ref, o_ref):
    c = c_ref[...]
    s = c * jax.nn.sigmoid(c)
    o_ref[0] = jnp.dot(s, w_ref[0], precision=HIGHEST, preferred_element_type=F32) + b_ref[0]


def _ada(c_all, w_ada, b_ada):
    depth = w_ada.shape[0]
    nb = c_all.shape[0]
    n_col = 6 * D_MODEL // D_MODEL
    return pl.pallas_call(
        _ada_kernel,
        out_shape=jax.ShapeDtypeStruct((depth, nb, 6 * D_MODEL), F32),
        grid=(depth, n_col),
        in_specs=[
            pl.BlockSpec((nb, D_MODEL), lambda l, j: (0, 0)),
            pl.BlockSpec((1, D_MODEL, D_MODEL), lambda l, j: (l, 0, j)),
            pl.BlockSpec((1, 1, D_MODEL), lambda l, j: (l, 0, j)),
        ],
        out_specs=pl.BlockSpec((1, nb, D_MODEL), lambda l, j: (l, 0, j)),
        compiler_params=pltpu.CompilerParams(dimension_semantics=("arbitrary", "arbitrary"),
                                             vmem_limit_bytes=_vmem_limit(32)),
        name="ada",
    )(c_all, w_ada, b_ada.reshape(depth, 1, 6 * D_MODEL))


def _rel_bucket(rel):
    half = NUM_BUCKETS // 2
    max_exact = half // 2
    n = jnp.abs(rel)
    n_f = jnp.maximum(n, 1).astype(jnp.float32)
    large = max_exact + (jnp.log(n_f / max_exact) / math.log(MAX_DISTANCE / max_exact)
                         * (half - max_exact)).astype(jnp.int32)
    large = jnp.minimum(large, half - 1)
    return jnp.where(rel > 0, half, 0) + jnp.where(n < max_exact, n, large)


BAND = WINDOW + CHUNK
PAIR_ROWS = 2 * CHUNK
PAIR_COLS = 2 * BAND


def _band_codes():
    r = jnp.arange(PAIR_ROWS)[:, None]
    j = jnp.arange(PAIR_COLS)[None, :]
    bucket = _rel_bucket(j % BAND - WINDOW - r % CHUNK)
    head = 2 * (r // CHUNK) + j // BAND
    return (bucket + NUM_BUCKETS * head).astype(jnp.int32)


def _bias_kernel(table_ref, code_ref, o_ref):
    g = pl.program_id(0)
    code = code_ref[...]
    acc = jnp.zeros(code.shape, F32)
    for hq in range(GROUP):
        for i in range(NUM_BUCKETS):
            acc = jnp.where(code == hq * NUM_BUCKETS + i, table_ref[i, g * GROUP + hq], acc)
    o_ref[0] = acc


def _bias_table(rel_table):
    return pl.pallas_call(
        _bias_kernel,
        out_shape=jax.ShapeDtypeStruct((N_KV_HEADS, PAIR_ROWS, PAIR_COLS), F32),
        grid=(N_KV_HEADS,),
        in_specs=[
            pl.BlockSpec(memory_space=pltpu.SMEM),
            pl.BlockSpec((PAIR_ROWS, PAIR_COLS), lambda g: (0, 0)),
        ],
        out_specs=pl.BlockSpec((1, PAIR_ROWS, PAIR_COLS), lambda g: (g, 0, 0)),
        compiler_params=pltpu.CompilerParams(dimension_semantics=("arbitrary",)),
        name="rel_bias",
    )(rel_table, _band_codes())


def _row(ref, r):
    return ref.at[pl.ds(pl.multiple_of(r * ROW_TILES, ROW_TILES), ROW_TILES), :]


def _mix_kernel(*refs, tile, apply_ln0, mask_first, alpha, n_steps, n_blocks, has_xs_in, finalize):
    (x_ref, mod_ref, ln0g_ref, ln0b_ref, win_ref, bin_ref, bias_ref, sink_ref, convw_ref,
     woa_ref, wob_ref, wo_ref, ln1g_ref, ln1b_ref, wr_ref, br_ref, k0_ref, v0_ref, u0_ref,
     upto_ref, st0_ref, tab0_ref) = refs[:22]
    refs = refs[22 + (1 if has_xs_in else 0):]
    (x1_ref, gate_ref, dest_ref, newk_ref, newv_ref, newu_ref, st_ref, tab_ref, xs_ref,
     kc_ref, vc_ref, ubuf_ref, run_ref, cur_ref, nfree_ref, h2buf, dest_v, dest_s, fin_v, fin_s,
     ssem, dsem, zsem) = refs
    i = pl.program_id(1)
    t = tile
    lin = pl.program_id(0) * pl.num_programs(1) + i
    now = lax.rem(lin, RING)
    src = lax.rem(lin + RING - LAG, RING)
    slot_rows = t * ROW_TILES

    def dest_copy(slot):
        return pltpu.make_async_copy(dest_v.at[slot], dest_s.at[slot], dsem.at[slot])

    def scatter_wait(slot):
        for _ in range(TOP_K):
            pltpu.make_async_copy(h2buf.at[slot], xs_ref.at[pl.ds(0, slot_rows), :], ssem.at[slot]).wait()

    def scatter_row(slot, tok):
        for k in range(TOP_K):
            pltpu.make_async_copy(_row(h2buf.at[slot], tok), _row(xs_ref, dest_s[slot, k, tok]),
                                  ssem.at[slot]).start(priority=k % 2)

    @pl.when(lin == 0)
    def _():
        run_ref[...] = st0_ref[:, 0:1]
        cur_ref[...] = st0_ref[:, 1:2]
        nfree_ref[...] = st0_ref[0:1, 2:3]
        tab_ref[...] = tab0_ref[...]
        for back in range(1, LAG + 1):
            h2buf[RING - back] = jnp.zeros(h2buf.shape[1:], F32)
            for k in range(TOP_K):
                def fill(tok, c, k=k, back=back):
                    dest_s[RING - back, k, tok] = n_blocks * MOE_BLOCK + ((back - 1) * TOP_K + k) * t + tok
                    return c
                lax.fori_loop(0, t, fill, 0)

    @pl.when(lin >= LAG)
    def _():
        dest_copy(src).wait()
        scatter_wait(now)

    @pl.when(i == 0)
    def _():
        kc_ref[...] = k0_ref[0]
        vc_ref[...] = v0_ref[0]
        ubuf_ref[0:SUBLANES, :] = u0_ref[0]

    for tok in range(t):
        scatter_row(src, tok)

    x = x_ref[0]
    if apply_ln0:
        x = _layernorm(x, ln0g_ref[...], ln0b_ref[...])
    mod = mod_ref[0]
    sh1, sc1, g1, sh2, sc2, g2 = [mod[j:j + 1, :] for j in range(6)]
    h = (x * (1.0 + sc1) + sh1).astype(BF16)

    def proj(j0, j1):
        lo, hi = IN_OFFS[j0], IN_OFFS[j1]
        return jnp.dot(h, win_ref[:, lo:hi], preferred_element_type=F32) + bin_ref[:, lo:hi]

    q = proj(0, 1)
    kv = proj(1, 3)
    kfull = jnp.concatenate([kc_ref[...], kv[:, :KV_WIDTH]], axis=0)
    vfull = jnp.concatenate([vc_ref[...], kv[:, KV_WIDTH:]], axis=0)
    kc_ref[...] = kfull[t:, :]
    vc_ref[...] = vfull[t:, :]
    newk_ref[0] = kfull[t:, :]
    newv_ref[0] = vfull[t:, :]
    low = lax.broadcasted_iota(jnp.int32, kfull.shape, 1) < HEAD_DIM
    k_sw = pltpu.roll(kfull, HEAD_DIM, axis=1)
    v_sw = pltpu.roll(vfull, HEAD_DIM, axis=1)
    k_even = [jnp.where(low, kfull, 0.0).astype(BF16), jnp.where(low, k_sw, 0.0).astype(BF16)]
    k_odd = [jnp.where(low, 0.0, k_sw).astype(BF16), jnp.where(low, 0.0, kfull).astype(BF16)]
    v_even = [jnp.where(low, vfull, 0.0).astype(BF16), jnp.where(low, v_sw, 0.0).astype(BF16)]
    v_odd = [jnp.where(low, 0.0, v_sw).astype(BF16), jnp.where(low, 0.0, vfull).astype(BF16)]
    col = lax.broadcasted_iota(jnp.int32, (PAIR_ROWS, PAIR_COLS), 1)
    even = col < BAND
    band_col = jnp.where(even, col, col - BAND)
    first_pair = lax.broadcasted_iota(jnp.int32, (PAIR_ROWS, 1), 0) < CHUNK
    out_low = lax.broadcasted_iota(jnp.int32, (PAIR_ROWS, LANES), 1) < HEAD_DIM
    sink_even = [jnp.where(first_pair, sink_ref[g * GROUP], sink_ref[g * GROUP + 2]) for g in range(N_KV_HEADS)]
    sink_odd = [jnp.where(first_pair, sink_ref[g * GROUP + 1], sink_ref[g * GROUP + 3])
                for g in range(N_KV_HEADS)]
    units = [(c, g) for c in range(t // CHUNK) for g in range(N_KV_HEADS)]
    scores = []
    for c, g in units:
        r0 = c * CHUNK
        ql = jnp.concatenate([q[r0:r0 + CHUNK, (2 * g) * LANES:(2 * g + 1) * LANES],
                              q[r0:r0 + CHUNK, (2 * g + 1) * LANES:(2 * g + 2) * LANES]], axis=0)
        kp = jnp.concatenate([k_even[g][r0:r0 + BAND], k_odd[g][r0:r0 + BAND]], axis=0)
        s = lax.dot_general(ql.astype(BF16), kp, NT_DIMS, preferred_element_type=F32)
        s = s * (HEAD_DIM ** -0.5) + bias_ref[g]
        if mask_first and c < WINDOW // CHUNK:
            s = jnp.where(band_col < jnp.where(i == 0, WINDOW - r0, 0), NEG_INF, s)
        scores.append(s)
    maxes = []
    for (c, g), s in zip(units, scores):
        m_e = jnp.maximum(jnp.max(jnp.where(even, s, -jnp.inf), axis=-1, keepdims=True), sink_even[g])
        m_o = jnp.maximum(jnp.max(jnp.where(even, -jnp.inf, s), axis=-1, keepdims=True), sink_odd[g])
        maxes.append((m_e, m_o))
    exps, scales = [], []
    for (c, g), s, (m_e, m_o) in zip(units, scores, maxes):
        e = jnp.exp(s - jnp.where(even, m_e, m_o))
        d_e = jnp.sum(jnp.where(even, e, 0.0), axis=-1, keepdims=True) + jnp.exp(sink_even[g] - m_e)
        d_o = jnp.sum(jnp.where(even, 0.0, e), axis=-1, keepdims=True) + jnp.exp(sink_odd[g] - m_o)
        exps.append(e.astype(BF16))
        scales.append(jnp.where(out_low, 1.0 / d_e, 1.0 / d_o))
    outs = {}
    for (c, g), e, scale in zip(units, exps, scales):
        r0 = c * CHUNK
        vp = jnp.concatenate([v_even[g][r0:r0 + BAND], v_odd[g][r0:r0 + BAND]], axis=0)
        outs[c, g] = jnp.dot(e, vp, preferred_element_type=F32) * scale
    ya = jnp.concatenate(
        [jnp.concatenate([outs[c, g][half * CHUNK:(half + 1) * CHUNK] for g in range(N_KV_HEADS)
                          for half in range(2)], axis=-1) for c in range(t // CHUNK)], axis=0)

    cb = proj(3, 4)
    u = proj(4, 5) * proj(5, 6)
    ubuf_ref[SUBLANES:t + SUBLANES, :] = u
    cw = convw_ref[...]
    yc = (cw[0:1, :] * ubuf_ref[SUBLANES - 2:t + SUBLANES - 2, :]
          + cw[1:2, :] * ubuf_ref[SUBLANES - 1:t + SUBLANES - 1, :] + cw[2:3, :] * u)
    yb = cb * yc
    tail = ubuf_ref[t:t + SUBLANES, :]
    newu_ref[0] = tail
    ubuf_ref[0:SUBLANES, :] = tail

    a_out = jnp.dot(ya.astype(BF16), woa_ref[...], preferred_element_type=F32)
    b_out = jnp.dot(yb.astype(BF16), wob_ref[...], preferred_element_type=F32)
    mixin = jax.nn.sigmoid(proj(6, 7)) * a_out + jax.nn.sigmoid(proj(7, 8)) * b_out
    mix = jnp.dot(mixin.astype(BF16), wo_ref[...], preferred_element_type=F32)
    x1 = _layernorm(alpha * x + (1.0 + g1) * mix, ln1g_ref[...], ln1b_ref[...])
    x1_ref[0] = x1

    h2 = x1 * (1.0 + sc2) + sh2
    _to_row_tiles(h2buf.at[now], h2, t)
    logits = lax.dot_general(wr_ref[...], h2.astype(BF16), NT_DIMS, preferred_element_type=F32) + br_ref[...]
    eid = lax.broadcasted_iota(jnp.int32, (N_EXPERTS, t), 0)
    vals, ids = [], []
    for _ in range(TOP_K):
        mx = jnp.max(logits, axis=0, keepdims=True)
        sel = jnp.min(jnp.where(logits == mx, eid, N_EXPERTS), axis=0, keepdims=True)
        vals.append(mx)
        ids.append(sel)
        logits = jnp.where(eid == sel, -jnp.inf, logits)
    ex = [jnp.exp(v - vals[0]) for v in vals]
    tot = ex[0] + ex[1] + ex[2] + ex[3]
    gate_ref[0] = jnp.concatenate([e_ / tot for e_ in ex], axis=0)

    per_block = 1.0 / MOE_BLOCK
    hits = [(eid == ids[k]).astype(F32) for k in range(TOP_K)]
    hit = hits[0] + hits[1] + hits[2] + hits[3]
    count = jnp.sum(hit, axis=1, keepdims=True)
    incl = jnp.dot(hit.astype(BF16), upto_ref[...], preferred_element_type=F32)
    run = run_ref[...]
    cur = cur_ref[...]
    run_blk = jnp.floor(run * per_block)
    is_open = (run - run_blk * MOE_BLOCK > 0).astype(F32)
    end = run + count
    blocks_before = jnp.ceil(run * per_block)
    n_new = jnp.ceil(end * per_block) - blocks_before
    er = lax.broadcasted_iota(jnp.int32, (N_EXPERTS, N_EXPERTS), 0)
    ec = lax.broadcasted_iota(jnp.int32, (N_EXPERTS, N_EXPERTS), 1)
    earlier = jnp.dot((ec < er).astype(BF16), jnp.broadcast_to(n_new, (N_EXPERTS, LANES)).astype(BF16),
                      preferred_element_type=F32)[:, 0:1]
    base = nfree_ref[...] + earlier
    r = run + incl - hit
    r_blk = jnp.floor(r * per_block)
    ordinal = r_blk - run_blk
    block = jnp.where(jnp.logical_and(is_open > 0, ordinal == 0), cur, base + ordinal - is_open)
    row = block * MOE_BLOCK + (r - r_blk * MOE_BLOCK)
    dests = [jnp.sum(hits[k] * row, axis=0, keepdims=True).astype(jnp.int32) for k in range(TOP_K)]
    dest_v[now] = jnp.concatenate(dests, axis=0)
    for k in range(TOP_K):
        for gidx in range(t // DEST_GROUP):
            dest_ref[gidx, k:k + 1, :] = dests[k][:, gidx * DEST_GROUP:(gidx + 1) * DEST_GROUP]
    lane = lax.broadcasted_iota(jnp.int32, tab_ref.shape, 1).astype(F32)
    table = tab_ref[...]
    for j in range(-(-t // MOE_BLOCK)):
        table = jnp.where(jnp.logical_and(lane == blocks_before + j, n_new > j), base + j, table)
    tab_ref[...] = table
    cur = jnp.where(n_new > 0, base + n_new - 1, cur)
    nfree = nfree_ref[...] + jnp.sum(n_new, axis=0, keepdims=True)
    run_ref[...] = end
    cur_ref[...] = cur
    nfree_ref[...] = nfree
    st_lane = lax.broadcasted_iota(jnp.int32, st_ref.shape, 1)
    st_ref[...] = jnp.where(st_lane == 0, end, jnp.where(st_lane == 1, cur, nfree))
    dest_copy(now).start()

    @pl.when(lin == n_steps - 1)
    def _():
        for back in range(LAG):
            late = lax.rem(lin + RING - back, RING)
            dest_copy(late).wait()
            lax.fori_loop(0, t, lambda tok, c, late=late: (scatter_row(late, tok), c)[1], 0)
        for slot in range(RING):
            scatter_wait(slot)
        if finalize:
            rem = end - jnp.floor(end * per_block) * MOE_BLOCK
            tail_lo = cur * MOE_BLOCK + rem
            tail_hi = jnp.where(rem > 0, (cur + 1.0) * MOE_BLOCK, tail_lo)
            fin_v[...] = jnp.where(st_lane == 0, tail_lo, jnp.where(st_lane == 1, tail_hi, nfree)).astype(jnp.int32)
            fin = pltpu.make_async_copy(fin_v, fin_s, dsem.at[0])
            fin.start()
            fin.wait()
            h2buf[0] = jnp.zeros(h2buf.shape[1:], F32)

            def zero_rows(r_, n_rows):
                return pltpu.make_async_copy(
                    h2buf.at[0].at[pl.ds(0, n_rows * ROW_TILES), :],
                    xs_ref.at[pl.ds(pl.multiple_of(r_ * ROW_TILES, ROW_TILES), n_rows * ROW_TILES), :], zsem)

            piece = min(slot_rows, MOE_BLOCK * ROW_TILES)

            def zero_part(c_):
                return pltpu.make_async_copy(
                    h2buf.at[0].at[pl.ds(0, piece), :],
                    xs_ref.at[pl.ds(pl.multiple_of(c_ * piece, piece), piece), :], zsem)

            def per_expert(e, carry):
                lo_, hi_ = fin_s[e, 0], fin_s[e, 1]
                n_big = lax.div(hi_ - lo_, DEST_GROUP)
                mid = lo_ + n_big * DEST_GROUP
                for act in ("start", "wait"):
                    lax.fori_loop(0, n_big, lambda j, c, act=act: (
                        getattr(zero_rows(lo_ + j * DEST_GROUP, DEST_GROUP), act)(), c)[1], 0)
                    lax.fori_loop(mid, hi_, lambda r_, c, act=act: (getattr(zero_rows(r_, 1), act)(), c)[1], 0)
                return carry

            lax.fori_loop(0, N_EXPERTS, per_expert, 0)
            parts = MOE_BLOCK * ROW_TILES // piece
            lo, hi = fin_s[0, 2] * parts, n_blocks * parts
            lax.fori_loop(lo, hi, lambda c_, c: (zero_part(c_).start(), c)[1], 0)
            lax.fori_loop(lo, hi, lambda c_, c: (zero_part(c_).wait(), c)[1], 0)


def _mix(x, mod, ln0_g, ln0_b, w_in, b_in, bias, sink, conv_w, w_oa, w_ob, w_o, ln1_g, ln1_b,
         w_rt, b_r, k0, v0, u0, state, table, xs_in, *, tile, apply_ln0, mask_first, alpha, n_blocks, finalize):
    b, s, d = x.shape
    n_t = s // tile
    assert b * n_t >= RING
    const = lambda shape: pl.BlockSpec(shape, lambda bb, ii: (0,) * len(shape), pipeline_mode=pl.Buffered(1))
    per_b = lambda shape: pl.BlockSpec((1,) + shape, lambda bb, ii: (bb,) + (0,) * len(shape))
    whole = lambda shape: pl.BlockSpec(shape, lambda bb, ii: (0,) * len(shape))
    kern = functools.partial(_mix_kernel, tile=tile, apply_ln0=apply_ln0, mask_first=mask_first, alpha=alpha,
                             n_steps=b * n_t, n_blocks=n_blocks, has_xs_in=xs_in is not None, finalize=finalize)
    upto = (jnp.arange(tile)[:, None] <= jnp.arange(tile)[None, :]).astype(BF16)
    xs_rows = (n_blocks + SPARE_BLOCKS) * MOE_BLOCK * ROW_TILES
    groups = tile // DEST_GROUP
    operands = [x, mod, ln0_g, ln0_b, w_in, b_in, bias, sink, conv_w, w_oa, w_ob, w_o, ln1_g, ln1_b, w_rt, b_r,
                k0, v0, u0, upto, state, table]
    in_specs = [
        pl.BlockSpec((1, tile, d), lambda bb, ii: (bb, ii, 0)),
        per_b((6, d)),
        const((1, d)), const((1, d)),
        const((d, IN_WIDTH)), const((1, IN_WIDTH)),
        const((N_KV_HEADS, PAIR_ROWS, PAIR_COLS)),
        pl.BlockSpec(memory_space=pltpu.SMEM),
        const((CONV_K, CONV_WIDTH)),
        const((ATT_WIDTH, d)), const((CONV_WIDTH, d)), const((d, d)),
        const((1, d)), const((1, d)),
        const((N_EXPERTS, d)), const((N_EXPERTS, 1)),
        per_b((WINDOW, KV_WIDTH)), per_b((WINDOW, KV_WIDTH)), per_b((SUBLANES, CONV_WIDTH)),
        const((tile, tile)), const((N_EXPERTS, LANES)), const((N_EXPERTS, TABLE_LANES)),
    ]
    aliases = {}
    if xs_in is not None:
        operands.append(xs_in)
        in_specs.append(pl.BlockSpec(memory_space=pl.ANY))
        aliases = {len(operands) - 1: 8}
    return pl.pallas_call(
        kern,
        out_shape=(
            jax.ShapeDtypeStruct((b, s, d), F32),
            jax.ShapeDtypeStruct((b * n_t, TOP_K, tile), F32),
            jax.ShapeDtypeStruct((b * s // DEST_GROUP, TOP_K, DEST_GROUP), jnp.int32),
            jax.ShapeDtypeStruct((b, WINDOW, KV_WIDTH), F32),
            jax.ShapeDtypeStruct((b, WINDOW, KV_WIDTH), F32),
            jax.ShapeDtypeStruct((b, SUBLANES, CONV_WIDTH), F32),
            jax.ShapeDtypeStruct((N_EXPERTS, LANES), F32),
            jax.ShapeDtypeStruct((N_EXPERTS, TABLE_LANES), F32),
            jax.ShapeDtypeStruct((xs_rows, LANES), F32),
        ),
        grid=(b, n_t),
        in_specs=in_specs,
        out_specs=(
            pl.BlockSpec((1, tile, d), lambda bb, ii: (bb, ii, 0)),
            pl.BlockSpec((1, TOP_K, tile), lambda bb, ii: (bb * n_t + ii, 0, 0)),
            pl.BlockSpec((groups, TOP_K, DEST_GROUP), lambda bb, ii: (bb * n_t + ii, 0, 0)),
            per_b((WINDOW, KV_WIDTH)), per_b((WINDOW, KV_WIDTH)), per_b((SUBLANES, CONV_WIDTH)),
            whole((N_EXPERTS, LANES)), whole((N_EXPERTS, TABLE_LANES)),
            pl.BlockSpec(memory_space=pl.ANY),
        ),
        scratch_shapes=[
            pltpu.VMEM((WINDOW, KV_WIDTH), F32),
            pltpu.VMEM((WINDOW, KV_WIDTH), F32),
            pltpu.VMEM((tile + SUBLANES, CONV_WIDTH), F32),
            pltpu.VMEM((N_EXPERTS, 1), F32),
            pltpu.VMEM((N_EXPERTS, 1), F32),
            pltpu.VMEM((1, 1), F32),
            pltpu.VMEM((RING, tile * ROW_TILES, LANES), F32),
            pltpu.VMEM((RING, TOP_K, tile), jnp.int32),
            pltpu.SMEM((RING, TOP_K, tile), jnp.int32),
            pltpu.VMEM((N_EXPERTS, LANES), jnp.int32),
            pltpu.SMEM((N_EXPERTS, LANES), jnp.int32),
            pltpu.SemaphoreType.DMA((RING,)),
            pltpu.SemaphoreType.DMA((RING,)),
            pltpu.SemaphoreType.DMA,
        ],
        input_output_aliases=aliases,
        compiler_params=pltpu.CompilerParams(dimension_semantics=("arbitrary", "arbitrary"),
                                             vmem_limit_bytes=_vmem_limit(56)),
        name="mix",
    )(*operands)


def _ffn_kernel(first_ref, nblk_ref, nused_ref, order_ref, short_ref, xs_ref, wgu_ref, bgu_ref, wd_ref, bd_ref,
                ys_ref, wgu_bf, wd_bf, xbuf, ybuf, in_sem, out_sem, *, n_blocks):
    e = pl.program_id(0)
    first = first_ref[e]
    n_used = nused_ref[0]
    block_rows = MOE_BLOCK * ROW_TILES

    def blk(ref, b):
        return ref.at[pl.ds(pl.multiple_of(b * block_rows, block_rows), block_rows), :]

    def in_copy(b, slot):
        return pltpu.make_async_copy(blk(xs_ref, order_ref[b]), xbuf.at[slot], in_sem.at[slot])

    def out_copy(b, slot):
        return pltpu.make_async_copy(ybuf.at[slot], blk(ys_ref, order_ref[b]), out_sem.at[slot])

    @pl.when(e == 0)
    def _():
        in_copy(0, 0).start()
        ybuf[...] = jnp.zeros(ybuf.shape, F32)

    @pl.when(nblk_ref[e] > 0)
    def _():
        wgu_bf[...] = wgu_ref[0, 0].astype(BF16)
        wd_bf[...] = wd_ref[0, 0].astype(BF16)

    def compute(slot, rows):
        x = _from_row_tiles(xbuf.at[slot], 0, rows).astype(BF16)
        gu = jnp.dot(x, wgu_bf[...], preferred_element_type=F32) + bgu_ref[0, 0]
        g = jnp.minimum(gu[:, :D_FF], SWIGLU_LIMIT)
        lin = jnp.clip(gu[:, D_FF:], -SWIGLU_LIMIT, SWIGLU_LIMIT)
        a = g * jax.nn.sigmoid(SWIGLU_ALPHA * g) * (lin + 1.0)
        y = jnp.dot(a.astype(BF16), wd_bf[...], preferred_element_type=F32) + bd_ref[0, 0]
        _to_row_tiles(ybuf.at[slot], y, rows)

    def body(j, carry):
        b = first + j
        slot = lax.rem(b, 2)
        in_copy(b, slot).wait()

        @pl.when(b + 1 < n_used)
        def _():
            in_copy(b + 1, 1 - slot).start()

        @pl.when(b >= 2)
        def _():
            out_copy(b - 2, slot).wait()

        short = jnp.logical_and(j == nblk_ref[e] - 1, short_ref[e] == 1)

        @pl.when(short)
        def _():
            compute(slot, MOE_BLOCK // 2)

        @pl.when(jnp.logical_not(short))
        def _():
            compute(slot, MOE_BLOCK)

        out_copy(b, slot).start()
        return carry

    lax.fori_loop(0, nblk_ref[e], body, 0)

    @pl.when(e == N_EXPERTS - 1)
    def _():
        @pl.when(n_used >= 2)
        def _():
            out_copy(n_used - 2, lax.rem(n_used, 2)).wait()

        out_copy(n_used - 1, lax.rem(n_used - 1, 2)).wait()
        ybuf[0] = jnp.zeros(ybuf.shape[1:], F32)
        lax.fori_loop(n_used, n_blocks, lambda b, c: (out_copy(b, 0).start(), c)[1], 0)
        lax.fori_loop(n_used, n_blocks, lambda b, c: (out_copy(b, 0).wait(), c)[1], 0)


def _ffn(layer, first_blk, n_blk, n_used, order, short, xs, w_gu, b_gu, w_down, b_down, n_blocks):
    block_rows = MOE_BLOCK * ROW_TILES
    depth = w_gu.shape[0]

    def expert(e, *_):
        return (layer, e, 0, 0)

    return pl.pallas_call(
        functools.partial(_ffn_kernel, n_blocks=n_blocks),
        out_shape=jax.ShapeDtypeStruct((n_blocks * block_rows, LANES), F32),
        grid_spec=pltpu.PrefetchScalarGridSpec(
            num_scalar_prefetch=5,
            grid=(N_EXPERTS,),
            in_specs=[
                pl.BlockSpec(memory_space=pl.ANY),
                pl.BlockSpec((1, 1, D_MODEL, 2 * D_FF), expert),
                pl.BlockSpec((1, 1, 1, 2 * D_FF), expert),
                pl.BlockSpec((1, 1, D_FF, D_MODEL), expert),
                pl.BlockSpec((1, 1, 1, D_MODEL), expert),
            ],
            out_specs=pl.BlockSpec(memory_space=pl.ANY),
            scratch_shapes=[
                pltpu.VMEM((D_MODEL, 2 * D_FF), BF16),
                pltpu.VMEM((D_FF, D_MODEL), BF16),
                pltpu.VMEM((2, block_rows, LANES), F32),
                pltpu.VMEM((2, block_rows, LANES), F32),
                pltpu.SemaphoreType.DMA((2,)),
                pltpu.SemaphoreType.DMA((2,)),
            ],
        ),
        compiler_params=pltpu.CompilerParams(dimension_semantics=("arbitrary",),
                                             vmem_limit_bytes=_vmem_limit(56)),
        name="ffn",
    )(first_blk, n_blk, n_used, order, short, xs, w_gu, b_gu.reshape(depth, N_EXPERTS, 1, 2 * D_FF), w_down,
      b_down.reshape(depth, N_EXPERTS, 1, D_MODEL))


def _combine_kernel(dest_ref, next_ref, x1_ref, mod_ref, gate_ref, ln2g_ref, ln2b_ref, ys_ref, o_ref,
                    buf0, buf1, sem0, sem1, *, tile, alpha, n_steps):
    t = tile
    step = pl.program_id(0) * pl.num_programs(1) + pl.program_id(1)

    def fetch(d_ref, g, tt, buf, sem):
        tok = g * DEST_GROUP + tt
        for k in range(TOP_K):
            pltpu.make_async_copy(_row(ys_ref, d_ref[g, k, tt]), _row(buf, k * t + tok),
                                  sem).start(priority=k % 2)

    def wait(buf, sem):
        pltpu.make_async_copy(ys_ref.at[pl.ds(0, TOP_K * t * ROW_TILES), :], buf, sem).wait()

    @pl.when(step == 0)
    def _():
        for g in range(t // DEST_GROUP):
            lax.fori_loop(0, DEST_GROUP, lambda tt, c, g=g: (fetch(dest_ref, g, tt, buf0, sem0), c)[1], 0)

    def phase(buf, sem, nxt_buf, nxt_sem):
        wait(buf, sem)
        for g in range(t // DEST_GROUP):
            for tt in range(DEST_GROUP):
                fetch(next_ref, g, tt, nxt_buf, nxt_sem)
        gates = gate_ref[0]
        gates_t = jnp.transpose(jnp.concatenate([gates, jnp.zeros_like(gates)], axis=0))
        ff = jnp.zeros((t, D_MODEL), F32)
        for k in range(TOP_K):
            ff = ff + gates_t[:, k:k + 1] * _from_row_tiles(buf, k * t, t)
        g2 = mod_ref[0][5:6, :]
        o_ref[0] = _layernorm(alpha * x1_ref[0] + (1.0 + g2) * ff, ln2g_ref[...], ln2b_ref[...])

    @pl.when(step % 2 == 0)
    def _():
        phase(buf0, sem0, buf1, sem1)

    @pl.when(step % 2 == 1)
    def _():
        phase(buf1, sem1, buf0, sem0)

    @pl.when(step == n_steps - 1)
    def _():
        if (n_steps - 1) % 2 == 0:
            wait(buf1, sem1)
        else:
            wait(buf0, sem0)


def _combine(dest, group_offset, x1, mod, gates, ln2_g, ln2_b, ys, *, tile, alpha):
    b, s, d = x1.shape
    n_t = s // tile
    n_steps = b * n_t
    groups = tile // DEST_GROUP
    goff = group_offset // groups
    per_gate_row = gates.shape[2] // tile
    kern = functools.partial(_combine_kernel, tile=tile, alpha=alpha, n_steps=n_steps)
    buf = pltpu.VMEM((TOP_K * tile * ROW_TILES, LANES), F32)
    return pl.pallas_call(
        kern,
        out_shape=jax.ShapeDtypeStruct((b, s, d), F32),
        grid=(b, n_t),
        in_specs=[
            pl.BlockSpec((groups, TOP_K, DEST_GROUP), lambda bb, ii: (goff + bb * n_t + ii, 0, 0),
                         memory_space=pltpu.SMEM),
            pl.BlockSpec((groups, TOP_K, DEST_GROUP),
                         lambda bb, ii: (goff + jnp.minimum(bb * n_t + ii + 1, n_steps - 1), 0, 0),
                         memory_space=pltpu.SMEM),
            pl.BlockSpec((1, tile, d), lambda bb, ii: (bb, ii, 0)),
            pl.BlockSpec((1, 6, d), lambda bb, ii: (bb, 0, 0)),
            pl.BlockSpec((1, TOP_K, tile), lambda bb, ii: ((bb * n_t + ii) // per_gate_row, 0,
                                                           (bb * n_t + ii) % per_gate_row)),
            pl.BlockSpec((1, d), lambda bb, ii: (0, 0)),
            pl.BlockSpec((1, d), lambda bb, ii: (0, 0)),
            pl.BlockSpec(memory_space=pl.ANY),
        ],
        out_specs=pl.BlockSpec((1, tile, d), lambda bb, ii: (bb, ii, 0)),
        scratch_shapes=[buf, buf, pltpu.SemaphoreType.DMA, pltpu.SemaphoreType.DMA],
        compiler_params=pltpu.CompilerParams(dimension_semantics=("arbitrary", "arbitrary"),
                                             vmem_limit_bytes=_vmem_limit(48)),
        name="combine",
    )(dest, dest, x1, mod, gates, ln2_g, ln2_b, ys)


def kernel(x_prompt, x_sample, c_prompt, c_sample, cache_k, cache_v, state_conv, rel_table, ln0_g, ln0_b, w_ada, b_ada, w_in, b_in, sinks, conv_w, w_oa, w_ob, w_o, ln1_g, ln1_b, w_router, b_router, w_gu, b_gu, w_down, b_down, ln2_g, ln2_b):
    depth = w_ada.shape[0]
    bp, sp, d = x_prompt.shape
    bs, ss, _ = x_sample.shape
    alpha = (2 * depth) ** 0.25
    n_tok = bp * sp + bs * ss
    assert sp % PROMPT_TILE == 0 and ss % DEST_GROUP == 0 and ss <= WINDOW
    assert -(-n_tok // MOE_BLOCK) <= TABLE_LANES
    n_blocks = -(-(n_tok * TOP_K) // MOE_BLOCK) + N_EXPERTS

    mod_all = _ada(jnp.concatenate([c_prompt, c_sample], axis=0), w_ada, b_ada)
    mod_all = mod_all.reshape(depth, bp + bs, 6, d)
    bias = _bias_table(rel_table)
    row = lambda a: a.reshape(1, -1)
    zeros_kv = jnp.zeros((bp, WINDOW, KV_WIDTH), F32)
    zeros_u = jnp.zeros((bp, SUBLANES, CONV_WIDTH), F32)
    state0 = jnp.zeros((N_EXPERTS, LANES), F32)
    table0 = jnp.zeros((N_EXPERTS, TABLE_LANES), F32)
    slot = jnp.arange(n_blocks, dtype=jnp.int32)

    y_p, y_s = x_prompt, x_sample
    outs = {name: [] for name in ("kp", "vp", "up", "ks", "vs", "us")}
    for l in range(depth):
        shared = (w_in[l].astype(BF16), row(b_in[l]))
        tail = (sinks[l], conv_w[l], w_oa[l].astype(BF16), w_ob[l].astype(BF16), w_o[l].astype(BF16),
                row(ln1_g[l]), row(ln1_b[l]), w_router[l].T.astype(BF16), b_router[l].reshape(N_EXPERTS, 1))
        mod_p, mod_s = mod_all[l, :bp], mod_all[l, bp:]
        x1_p, gate_p, dest_p, k_p, v_p, u_p, state, table, xs = _mix(
            y_p, mod_p, row(ln0_g), row(ln0_b), *shared, bias, *tail, zeros_kv, zeros_kv, zeros_u,
            state0, table0, None, tile=PROMPT_TILE, apply_ln0=(l == 0), mask_first=True, alpha=alpha,
            n_blocks=n_blocks, finalize=True)
        u0 = jnp.pad(state_conv[l], ((0, 0), (SUBLANES - (CONV_K - 1), 0), (0, 0)))
        x1_s, gate_s, dest_s, k_s, v_s, u_s, state, table, xs = _mix(
            y_s, mod_s, row(ln0_g), row(ln0_b), *shared, bias, *tail,
            cache_k[l].reshape(bs, WINDOW, KV_WIDTH), cache_v[l].reshape(bs, WINDOW, KV_WIDTH), u0,
            state, table, xs, tile=ss, apply_ln0=(l == 0), mask_first=False, alpha=alpha,
            n_blocks=n_blocks, finalize=False)

        counts = state[:, 0].astype(jnp.int32)
        n_used = state[0:1, 2].astype(jnp.int32)
        n_blk = (counts + MOE_BLOCK - 1) // MOE_BLOCK
        blk_end = jnp.cumsum(n_blk)
        first_blk = blk_end - n_blk
        owner = jnp.minimum(jnp.sum((blk_end[None, :] <= slot[:, None]).astype(jnp.int32), axis=1), N_EXPERTS - 1)
        nth = jnp.clip(slot - first_blk[owner], 0, TABLE_LANES - 1)
        order = jnp.where(slot < n_used, table[owner, nth].astype(jnp.int32), slot)

        last_rows = counts - (n_blk - 1) * MOE_BLOCK
        short = jnp.logical_and(n_blk > 0, last_rows <= MOE_BLOCK // 2).astype(jnp.int32)

        ys = _ffn(l, first_blk, n_blk, n_used, order, short, xs, w_gu, b_gu, w_down, b_down, n_blocks)
        y_p = _combine(dest_p, 0, x1_p, mod_p, gate_p, row(ln2_g[l]), row(ln2_b[l]), ys,
                       tile=COMBINE_TILE, alpha=alpha)
        y_s = _combine(dest_s, 0, x1_s, mod_s, gate_s, row(ln2_g[l]), row(ln2_b[l]), ys,
                       tile=ss, alpha=alpha)

        outs["kp"].append(k_p.reshape(bp, WINDOW, N_KV_HEADS, HEAD_DIM))
        outs["vp"].append(v_p.reshape(bp, WINDOW, N_KV_HEADS, HEAD_DIM))
        outs["up"].append(u_p[:, SUBLANES - (CONV_K - 1):, :])
        outs["ks"].append(k_s.reshape(bs, WINDOW, N_KV_HEADS, HEAD_DIM))
        outs["vs"].append(v_s.reshape(bs, WINDOW, N_KV_HEADS, HEAD_DIM))
        outs["us"].append(u_s[:, SUBLANES - (CONV_K - 1):, :])
    return (y_p, y_s, jnp.stack(outs["kp"]), jnp.stack(outs["vp"]), jnp.stack(outs["up"]),
            jnp.stack(outs["ks"]), jnp.stack(outs["vs"]), jnp.stack(outs["us"]))
```

```python
import functools
import math

import jax
import jax.numpy as jnp
import numpy as np
from jax import lax
from jax.experimental import pallas as pl
from jax.experimental.pallas import tpu as pltpu

D_MODEL = 1024
CHUNK = 64
N_HEADS = 8
N_KV_HEADS = 2
HEAD_DIM = 64
GROUP = N_HEADS // N_KV_HEADS
ATT_WIDTH = N_HEADS * HEAD_DIM
KV_WIDTH = N_KV_HEADS * HEAD_DIM
WINDOW = 128
CONV_WIDTH = 512
CONV_K = 3
NUM_BUCKETS = 32
MAX_DISTANCE = 128
N_EXPERTS = 32
TOP_K = 4
D_FF = 1024
SWIGLU_LIMIT = 7.0
SWIGLU_ALPHA = 1.702
MOE_BLOCK = 512
LN_EPS = 1e-5
NEG_INF = -1e30
IN_SIZES = (ATT_WIDTH, KV_WIDTH, KV_WIDTH, CONV_WIDTH, CONV_WIDTH, CONV_WIDTH, D_MODEL, D_MODEL)
IN_WIDTH = sum(IN_SIZES)
IN_OFFS = tuple(int(s) for s in np.cumsum((0,) + IN_SIZES))

SUBLANES = 8
LANES = 128
ROW_TILES = D_MODEL // LANES
assert ROW_TILES == SUBLANES

PROMPT_TILE = 512
COMBINE_TILE = 512
DEST_GROUP = 64
LAG = 2
RING = 2 * LAG
SPARE_BLOCKS = LAG * PROMPT_TILE * TOP_K // MOE_BLOCK
TABLE_LANES = 256

F32 = jnp.float32
BF16 = jnp.bfloat16
HIGHEST = lax.Precision.HIGHEST
NT_DIMS = (((1,), (1,)), ((), ()))


def _vmem_limit(mib):
    return mib * 1024 * 1024


def _layernorm(x, g, b):
    mu = jnp.mean(x, axis=-1, keepdims=True)
    xc = x - mu
    var = jnp.mean(xc * xc, axis=-1, keepdims=True)
    return xc * lax.rsqrt(var + LN_EPS) * g + b


def _to_row_tiles(ref, x, rows, base=0):
    for s in range(ROW_TILES):
        ref[pl.ds(base * ROW_TILES + s, rows, stride=ROW_TILES), :] = x[:, s * LANES:(s + 1) * LANES]


def _from_row_tiles(ref, base, rows):
    return jnp.concatenate(
        [ref[pl.ds(base * ROW_TILES + s, rows, stride=ROW_TILES), :] for s in range(ROW_TILES)], axis=-1)


def _ada_kernel(c_ref, w_ref, b_ref, o_ref):
    c = c_ref[...]
    s = c * jax.nn.sigmoid(c)
    o_ref[0] = jnp.dot(s, w_ref[0], precision=HIGHEST, preferred_element_type=F32) + b_ref[0]


def _ada(c_all, w_ada, b_ada):
    depth = w_ada.shape[0]
    nb = c_all.shape[0]
    n_col = 6 * D_MODEL // D_MODEL
    return pl.pallas_call(
        _ada_kernel,
        out_shape=jax.ShapeDtypeStruct((depth, nb, 6 * D_MODEL), F32),
        grid=(depth, n_col),
        in_specs=[
            pl.BlockSpec((nb, D_MODEL), lambda l, j: (0, 0)),
            pl.BlockSpec((1, D_MODEL, D_MODEL), lambda l, j: (l, 0, j)),
            pl.BlockSpec((1, 1, D_MODEL), lambda l, j: (l, 0, j)),
        ],
        out_specs=pl.BlockSpec((1, nb, D_MODEL), lambda l, j: (l, 0, j)),
        compiler_params=pltpu.CompilerParams(dimension_semantics=("arbitrary", "arbitrary"),
                                             vmem_limit_bytes=_vmem_limit(32)),
        name="ada",
    )(c_all, w_ada, b_ada.reshape(depth, 1, 6 * D_MODEL))


def _rel_bucket(rel):
    half = NUM_BUCKETS // 2
    max_exact = half // 2
    n = jnp.abs(rel)
    n_f = jnp.maximum(n, 1).astype(jnp.float32)
    large = max_exact + (jnp.log(n_f / max_exact) / math.log(MAX_DISTANCE / max_exact)
                         * (half - max_exact)).astype(jnp.int32)
    large = jnp.minimum(large, half - 1)
    return jnp.where(rel > 0, half, 0) + jnp.where(n < max_exact, n, large)


BAND = WINDOW + CHUNK
PAIR_ROWS = 2 * CHUNK
PAIR_COLS = 2 * BAND


def _band_codes():
    r = jnp.arange(PAIR_ROWS)[:, None]
    j = jnp.arange(PAIR_COLS)[None, :]
    bucket = _rel_bucket(j % BAND - WINDOW - r % CHUNK)
    head = 2 * (r // CHUNK) + j // BAND
    return (bucket + NUM_BUCKETS * head).astype(jnp.int32)


def _bias_kernel(table_ref, code_ref, o_ref):
    g = pl.program_id(0)
    code = code_ref[...]
    acc = jnp.zeros(code.shape, F32)
    for hq in range(GROUP):
        for i in range(NUM_BUCKETS):
            acc = jnp.where(code == hq * NUM_BUCKETS + i, table_ref[i, g * GROUP + hq], acc)
    o_ref[0] = acc


def _bias_table(rel_table):
    return pl.pallas_call(
        _bias_kernel,
        out_shape=jax.ShapeDtypeStruct((N_KV_HEADS, PAIR_ROWS, PAIR_COLS), F32),
        grid=(N_KV_HEADS,),
        in_specs=[
            pl.BlockSpec(memory_space=pltpu.SMEM),
            pl.BlockSpec((PAIR_ROWS, PAIR_COLS), lambda g: (0, 0)),
        ],
        out_specs=pl.BlockSpec((1, PAIR_ROWS, PAIR_COLS), lambda g: (g, 0, 0)),
        compiler_params=pltpu.CompilerParams(dimension_semantics=("arbitrary",)),
        name="rel_bias",
    )(rel_table, _band_codes())


def _row(ref, r):
    return ref.at[pl.ds(pl.multiple_of(r * ROW_TILES, ROW_TILES), ROW_TILES), :]


def _mix_kernel(*refs, tile, apply_ln0, mask_first, alpha, n_steps, n_blocks, has_xs_in, finalize):
    (x_ref, mod_ref, ln0g_ref, ln0b_ref, win_ref, bin_ref, bias_ref, sink_ref, convw_ref,
     woa_ref, wob_ref, wo_ref, ln1g_ref, ln1b_ref, wr_ref, br_ref, k0_ref, v0_ref, u0_ref,
     upto_ref, st0_ref, tab0_ref) = refs[:22]
    refs = refs[22 + (1 if has_xs_in else 0):]
    (x1_ref, gate_ref, dest_ref, newk_ref, newv_ref, newu_ref, st_ref, tab_ref, xs_ref,
     kc_ref, vc_ref, ubuf_ref, run_ref, cur_ref, nfree_ref, h2buf, dest_v, dest_s, fin_v, fin_s,
     ssem, dsem, zsem) = refs
    i = pl.program_id(1)
    t = tile
    lin = pl.program_id(0) * pl.num_programs(1) + i
    now = lax.rem(lin, RING)
    src = lax.rem(lin + RING - LAG, RING)
    slot_rows = t * ROW_TILES

    def dest_copy(slot):
        return pltpu.make_async_copy(dest_v.at[slot], dest_s.at[slot], dsem.at[slot])

    def scatter_wait(slot):
        for _ in range(TOP_K):
            pltpu.make_async_copy(h2buf.at[slot], xs_ref.at[pl.ds(0, slot_rows), :], ssem.at[slot]).wait()

    def scatter_row(slot, tok):
        for k in range(TOP_K):
            pltpu.make_async_copy(_row(h2buf.at[slot], tok), _row(xs_ref, dest_s[slot, k, tok]),
                                  ssem.at[slot]).start(priority=k % 2)

    @pl.when(lin == 0)
    def _():
        run_ref[...] = st0_ref[:, 0:1]
        cur_ref[...] = st0_ref[:, 1:2]
        nfree_ref[...] = st0_ref[0:1, 2:3]
        tab_ref[...] = tab0_ref[...]
        for back in range(1, LAG + 1):
            h2buf[RING - back] = jnp.zeros(h2buf.shape[1:], F32)
            for k in range(TOP_K):
                def fill(tok, c, k=k, back=back):
                    dest_s[RING - back, k, tok] = n_blocks * MOE_BLOCK + ((back - 1) * TOP_K + k) * t + tok
                    return c
                lax.fori_loop(0, t, fill, 0)

    @pl.when(lin >= LAG)
    def _():
        dest_copy(src).wait()
        scatter_wait(now)

    @pl.when(i == 0)
    def _():
        kc_ref[...] = k0_ref[0]
        vc_ref[...] = v0_ref[0]
        ubuf_ref[0:SUBLANES, :] = u0_ref[0]

    for tok in range(t):
        scatter_row(src, tok)

    x = x_ref[0]
    if apply_ln0:
        x = _layernorm(x, ln0g_ref[...], ln0b_ref[...])
    mod = mod_ref[0]
    sh1, sc1, g1, sh2, sc2, g2 = [mod[j:j + 1, :] for j in range(6)]
    h = (x * (1.0 + sc1) + sh1).astype(BF16)

    def proj(j0, j1):
        lo, hi = IN_OFFS[j0], IN_OFFS[j1]
        return jnp.dot(h, win_ref[:, lo:hi], preferred_element_type=F32) + bin_ref[:, lo:hi]

    q = proj(0, 1)
    kv = proj(1, 3)
    kfull = jnp.concatenate([kc_ref[...], kv[:, :KV_WIDTH]], axis=0)
    vfull = jnp.concatenate([vc_ref[...], kv[:, KV_WIDTH:]], axis=0)
    kc_ref[...] = kfull[t:, :]
    vc_ref[...] = vfull[t:, :]
    newk_ref[0] = kfull[t:, :]
    newv_ref[0] = vfull[t:, :]
    low = lax.broadcasted_iota(jnp.int32, kfull.shape, 1) < HEAD_DIM
    k_sw = pltpu.roll(kfull, HEAD_DIM, axis=1)
    v_sw = pltpu.roll(vfull, HEAD_DIM, axis=1)
    k_even = [jnp.where(low, kfull, 0.0).astype(BF16), jnp.where(low, k_sw, 0.0).astype(BF16)]
    k_odd = [jnp.where(low, 0.0, k_sw).astype(BF16), jnp.where(low, 0.0, kfull).astype(BF16)]
    v_even = [jnp.where(low, vfull, 0.0).astype(BF16), jnp.where(low, v_sw, 0.0).astype(BF16)]
    v_odd = [jnp.where(low, 0.0, v_sw).astype(BF16), jnp.where(low, 0.0, vfull).astype(BF16)]
    col = lax.broadcasted_iota(jnp.int32, (PAIR_ROWS, PAIR_COLS), 1)
    even = col < BAND
    band_col = jnp.where(even, col, col - BAND)
    first_pair = lax.broadcasted_iota(jnp.int32, (PAIR_ROWS, 1), 0) < CHUNK
    out_low = lax.broadcasted_iota(jnp.int32, (PAIR_ROWS, LANES), 1) < HEAD_DIM
    sink_even = [jnp.where(first_pair, sink_ref[g * GROUP], sink_ref[g * GROUP + 2]) for g in range(N_KV_HEADS)]
    sink_odd = [jnp.where(first_pair, sink_ref[g * GROUP + 1], sink_ref[g * GROUP + 3])
                for g in range(N_KV_HEADS)]
    units = [(c, g) for c in range(t // CHUNK) for g in range(N_KV_HEADS)]
    scores = []
    for c, g in units:
        r0 = c * CHUNK
        ql = jnp.concatenate([q[r0:r0 + CHUNK, (2 * g) * LANES:(2 * g + 1) * LANES],
                              q[r0:r0 + CHUNK, (2 * g + 1) * LANES:(2 * g + 2) * LANES]], axis=0)
        kp = jnp.concatenate([k_even[g][r0:r0 + BAND], k_odd[g][r0:r0 + BAND]], axis=0)
        s = lax.dot_general(ql.astype(BF16), kp, NT_DIMS, preferred_element_type=F32)
        s = s * (HEAD_DIM ** -0.5) + bias_ref[g]
        if mask_first and c < WINDOW // CHUNK:
            s = jnp.where(band_col < jnp.where(i == 0, WINDOW - r0, 0), NEG_INF, s)
        scores.append(s)
    maxes = []
    for (c, g), s in zip(units, scores):
        m_e = jnp.maximum(jnp.max(jnp.where(even, s, -jnp.inf), axis=-1, keepdims=True), sink_even[g])
        m_o = jnp.maximum(jnp.max(jnp.where(even, -jnp.inf, s), axis=-1, keepdims=True), sink_odd[g])
        maxes.append((m_e, m_o))
    exps, scales = [], []
    for (c, g), s, (m_e, m_o) in zip(units, scores, maxes):
        e = jnp.exp(s - jnp.where(even, m_e, m_o))
        d_e = jnp.sum(jnp.where(even, e, 0.0), axis=-1, keepdims=True) + jnp.exp(sink_even[g] - m_e)
        d_o = jnp.sum(jnp.where(even, 0.0, e), axis=-1, keepdims=True) + jnp.exp(sink_odd[g] - m_o)
        exps.append(e.astype(BF16))
        scales.append(jnp.where(out_low, 1.0 / d_e, 1.0 / d_o))
    outs = {}
    for (c, g), e, scale in zip(units, exps, scales):
        r0 = c * CHUNK
        vp = jnp.concatenate([v_even[g][r0:r0 + BAND], v_odd[g][r0:r0 + BAND]], axis=0)
        outs[c, g] = jnp.dot(e, vp, preferred_element_type=F32) * scale
    ya = jnp.concatenate(
        [jnp.concatenate([outs[c, g][half * CHUNK:(half + 1) * CHUNK] for g in range(N_KV_HEADS)
                          for half in range(2)], axis=-1) for c in range(t // CHUNK)], axis=0)

    cb = proj(3, 4)
    u = proj(4, 5) * proj(5, 6)
    ubuf_ref[SUBLANES:t + SUBLANES, :] = u
    cw = convw_ref[...]
    yc = (cw[0:1, :] * ubuf_ref[SUBLANES - 2:t + SUBLANES - 2, :]
          + cw[1:2, :] * ubuf_ref[SUBLANES - 1:t + SUBLANES - 1, :] + cw[2:3, :] * u)
    yb = cb * yc
    tail = ubuf_ref[t:t + SUBLANES, :]
    newu_ref[0] = tail
    ubuf_ref[0:SUBLANES, :] = tail

    gate_a = jax.nn.sigmoid(proj(6, 7))
    gate_b = jax.nn.sigmoid(proj(7, 8))
    ya_b, yb_b = ya.astype(BF16), yb.astype(BF16)
    n_half = 2 if t % (2 * LANES) == 0 else 1
    hr = t // n_half
    mixes = []
    for r in range(n_half):
        rows = slice(r * hr, (r + 1) * hr)
        a_out = jnp.dot(ya_b[rows], woa_ref[...], preferred_element_type=F32)
        b_out = jnp.dot(yb_b[rows], wob_ref[...], preferred_element_type=F32)
        mixin = gate_a[rows] * a_out + gate_b[rows] * b_out
        mixes.append(jnp.dot(mixin.astype(BF16), wo_ref[...], preferred_element_type=F32))
    logit_parts = []
    for r in range(n_half):
        rows = slice(r * hr, (r + 1) * hr)
        x1 = _layernorm(alpha * x[rows] + (1.0 + g1) * mixes[r], ln1g_ref[...], ln1b_ref[...])
        x1_ref[0, rows, :] = x1
        h2 = x1 * (1.0 + sc2) + sh2
        _to_row_tiles(h2buf.at[now], h2, hr, base=r * hr)
        logit_parts.append(lax.dot_general(wr_ref[...], h2.astype(BF16), NT_DIMS, preferred_element_type=F32))
    logits = jnp.concatenate(logit_parts, axis=1) + br_ref[...]
    eid = lax.broadcasted_iota(jnp.int32, (N_EXPERTS, t), 0)
    vals, ids = [], []
    for _ in range(TOP_K):
        mx = jnp.max(logits, axis=0, keepdims=True)
        sel = jnp.min(jnp.where(logits == mx, eid, N_EXPERTS), axis=0, keepdims=True)
        vals.append(mx)
        ids.append(sel)
        logits = jnp.where(eid == sel, -jnp.inf, logits)
    ex = [jnp.exp(v - vals[0]) for v in vals]
    tot = ex[0] + ex[1] + ex[2] + ex[3]
    gate_ref[0] = jnp.concatenate([e_ / tot for e_ in ex], axis=0)

    per_block = 1.0 / MOE_BLOCK
    hits = [(eid == ids[k]).astype(F32) for k in range(TOP_K)]
    hit = hits[0] + hits[1] + hits[2] + hits[3]
    count = jnp.sum(hit, axis=1, keepdims=True)
    incl = jnp.dot(hit.astype(BF16), upto_ref[...], preferred_element_type=F32)
    run = run_ref[...]
    cur = cur_ref[...]
    run_blk = jnp.floor(run * per_block)
    is_open = (run - run_blk * MOE_BLOCK > 0).astype(F32)
    end = run + count
    blocks_before = jnp.ceil(run * per_block)
    n_new = jnp.ceil(end * per_block) - blocks_before
    er = lax.broadcasted_iota(jnp.int32, (N_EXPERTS, N_EXPERTS), 0)
    ec = lax.broadcasted_iota(jnp.int32, (N_EXPERTS, N_EXPERTS), 1)
    earlier = jnp.dot((ec < er).astype(BF16), jnp.broadcast_to(n_new, (N_EXPERTS, LANES)).astype(BF16),
                      preferred_element_type=F32)[:, 0:1]
    base = nfree_ref[...] + earlier
    r = run + incl - hit
    r_blk = jnp.floor(r * per_block)
    ordinal = r_blk - run_blk
    block = jnp.where(jnp.logical_and(is_open > 0, ordinal == 0), cur, base + ordinal - is_open)
    row = block * MOE_BLOCK + (r - r_blk * MOE_BLOCK)
    dests = [jnp.sum(hits[k] * row, axis=0, keepdims=True).astype(jnp.int32) for k in range(TOP_K)]
    dest_v[now] = jnp.concatenate(dests, axis=0)
    for k in range(TOP_K):
        for gidx in range(t // DEST_GROUP):
            dest_ref[gidx, k:k + 1, :] = dests[k][:, gidx * DEST_GROUP:(gidx + 1) * DEST_GROUP]
    lane = lax.broadcasted_iota(jnp.int32, tab_ref.shape, 1).astype(F32)
    table = tab_ref[...]
    for j in range(-(-t // MOE_BLOCK)):
        table = jnp.where(jnp.logical_and(lane == blocks_before + j, n_new > j), base + j, table)
    tab_ref[...] = table
    cur = jnp.where(n_new > 0, base + n_new - 1, cur)
    nfree = nfree_ref[...] + jnp.sum(n_new, axis=0, keepdims=True)
    run_ref[...] = end
    cur_ref[...] = cur
    nfree_ref[...] = nfree
    st_lane = lax.broadcasted_iota(jnp.int32, st_ref.shape, 1)
    st_ref[...] = jnp.where(st_lane == 0, end, jnp.where(st_lane == 1, cur, nfree))
    dest_copy(now).start()

    @pl.when(lin == n_steps - 1)
    def _():
        for back in range(LAG):
            late = lax.rem(lin + RING - back, RING)
            dest_copy(late).wait()
            lax.fori_loop(0, t, lambda tok, c, late=late: (scatter_row(late, tok), c)[1], 0)
        for slot in range(RING):
            scatter_wait(slot)
        if finalize:
            rem = end - jnp.floor(end * per_block) * MOE_BLOCK
            tail_lo = cur * MOE_BLOCK + rem
            tail_hi = jnp.where(rem > 0, (cur + 1.0) * MOE_BLOCK, tail_lo)
            fin_v[...] = jnp.where(st_lane == 0, tail_lo, jnp.where(st_lane == 1, tail_hi, nfree)).astype(jnp.int32)
            fin = pltpu.make_async_copy(fin_v, fin_s, dsem.at[0])
            fin.start()
            fin.wait()
            h2buf[0] = jnp.zeros(h2buf.shape[1:], F32)

            def zero_rows(r_, n_rows):
                return pltpu.make_async_copy(
                    h2buf.at[0].at[pl.ds(0, n_rows * ROW_TILES), :],
                    xs_ref.at[pl.ds(pl.multiple_of(r_ * ROW_TILES, ROW_TILES), n_rows * ROW_TILES), :], zsem)

            piece = min(slot_rows, MOE_BLOCK * ROW_TILES)

            def zero_part(c_):
                return pltpu.make_async_copy(
                    h2buf.at[0].at[pl.ds(0, piece), :],
                    xs_ref.at[pl.ds(pl.multiple_of(c_ * piece, piece), piece), :], zsem)

            def per_expert(e, carry):
                lo_, hi_ = fin_s[e, 0], fin_s[e, 1]
                n_big = lax.div(hi_ - lo_, DEST_GROUP)
                mid = lo_ + n_big * DEST_GROUP
                for act in ("start", "wait"):
                    lax.fori_loop(0, n_big, lambda j, c, act=act: (
                        getattr(zero_rows(lo_ + j * DEST_GROUP, DEST_GROUP), act)(), c)[1], 0)
                    lax.fori_loop(mid, hi_, lambda r_, c, act=act: (getattr(zero_rows(r_, 1), act)(), c)[1], 0)
                return carry

            lax.fori_loop(0, N_EXPERTS, per_expert, 0)
            parts = MOE_BLOCK * ROW_TILES // piece
            lo, hi = fin_s[0, 2] * parts, n_blocks * parts
            lax.fori_loop(lo, hi, lambda c_, c: (zero_part(c_).start(), c)[1], 0)
            lax.fori_loop(lo, hi, lambda c_, c: (zero_part(c_).wait(), c)[1], 0)


def _mix(x, mod, ln0_g, ln0_b, w_in, b_in, bias, sink, conv_w, w_oa, w_ob, w_o, ln1_g, ln1_b,
         w_rt, b_r, k0, v0, u0, state, table, xs_in, *, tile, apply_ln0, mask_first, alpha, n_blocks, finalize):
    b, s, d = x.shape
    n_t = s // tile
    assert b * n_t >= RING
    const = lambda shape: pl.BlockSpec(shape, lambda bb, ii: (0,) * len(shape), pipeline_mode=pl.Buffered(1))
    per_b = lambda shape: pl.BlockSpec((1,) + shape, lambda bb, ii: (bb,) + (0,) * len(shape))
    whole = lambda shape: pl.BlockSpec(shape, lambda bb, ii: (0,) * len(shape))
    kern = functools.partial(_mix_kernel, tile=tile, apply_ln0=apply_ln0, mask_first=mask_first, alpha=alpha,
                             n_steps=b * n_t, n_blocks=n_blocks, has_xs_in=xs_in is not None, finalize=finalize)
    upto = (jnp.arange(tile)[:, None] <= jnp.arange(tile)[None, :]).astype(BF16)
    xs_rows = (n_blocks + SPARE_BLOCKS) * MOE_BLOCK * ROW_TILES
    groups = tile // DEST_GROUP
    operands = [x, mod, ln0_g, ln0_b, w_in, b_in, bias, sink, conv_w, w_oa, w_ob, w_o, ln1_g, ln1_b, w_rt, b_r,
                k0, v0, u0, upto, state, table]
    in_specs = [
        pl.BlockSpec((1, tile, d), lambda bb, ii: (bb, ii, 0)),
        per_b((6, d)),
        const((1, d)), const((1, d)),
        const((d, IN_WIDTH)), const((1, IN_WIDTH)),
        const((N_KV_HEADS, PAIR_ROWS, PAIR_COLS)),
        pl.BlockSpec(memory_space=pltpu.SMEM),
        const((CONV_K, CONV_WIDTH)),
        const((ATT_WIDTH, d)), const((CONV_WIDTH, d)), const((d, d)),
        const((1, d)), const((1, d)),
        const((N_EXPERTS, d)), const((N_EXPERTS, 1)),
        per_b((WINDOW, KV_WIDTH)), per_b((WINDOW, KV_WIDTH)), per_b((SUBLANES, CONV_WIDTH)),
        const((tile, tile)), const((N_EXPERTS, LANES)), const((N_EXPERTS, TABLE_LANES)),
    ]
    aliases = {}
    if xs_in is not None:
        operands.append(xs_in)
        in_specs.append(pl.BlockSpec(memory_space=pl.ANY))
        aliases = {len(operands) - 1: 8}
    return pl.pallas_call(
        kern,
        out_shape=(
            jax.ShapeDtypeStruct((b, s, d), F32),
            jax.ShapeDtypeStruct((b * n_t, TOP_K, tile), F32),
            jax.ShapeDtypeStruct((b * s // DEST_GROUP, TOP_K, DEST_GROUP), jnp.int32),
            jax.ShapeDtypeStruct((b, WINDOW, KV_WIDTH), F32),
            jax.ShapeDtypeStruct((b, WINDOW, KV_WIDTH), F32),
            jax.ShapeDtypeStruct((b, SUBLANES, CONV_WIDTH), F32),
            jax.ShapeDtypeStruct((N_EXPERTS, LANES), F32),
            jax.ShapeDtypeStruct((N_EXPERTS, TABLE_LANES), F32),
            jax.ShapeDtypeStruct((xs_rows, LANES), F32),
        ),
        grid=(b, n_t),
        in_specs=in_specs,
        out_specs=(
            pl.BlockSpec((1, tile, d), lambda bb, ii: (bb, ii, 0)),
            pl.BlockSpec((1, TOP_K, tile), lambda bb, ii: (bb * n_t + ii, 0, 0)),
            pl.BlockSpec((groups, TOP_K, DEST_GROUP), lambda bb, ii: (bb * n_t + ii, 0, 0)),
            per_b((WINDOW, KV_WIDTH)), per_b((WINDOW, KV_WIDTH)), per_b((SUBLANES, CONV_WIDTH)),
            whole((N_EXPERTS, LANES)), whole((N_EXPERTS, TABLE_LANES)),
            pl.BlockSpec(memory_space=pl.ANY),
        ),
        scratch_shapes=[
            pltpu.VMEM((WINDOW, KV_WIDTH), F32),
            pltpu.VMEM((WINDOW, KV_WIDTH), F32),
            pltpu.VMEM((tile + SUBLANES, CONV_WIDTH), F32),
            pltpu.VMEM((N_EXPERTS, 1), F32),
            pltpu.VMEM((N_EXPERTS, 1), F32),
            pltpu.VMEM((1, 1), F32),
            pltpu.VMEM((RING, tile * ROW_TILES, LANES), F32),
            pltpu.VMEM((RING, TOP_K, tile), jnp.int32),
            pltpu.SMEM((RING, TOP_K, tile), jnp.int32),
            pltpu.VMEM((N_EXPERTS, LANES), jnp.int32),
            pltpu.SMEM((N_EXPERTS, LANES), jnp.int32),
            pltpu.SemaphoreType.DMA((RING,)),
            pltpu.SemaphoreType.DMA((RING,)),
            pltpu.SemaphoreType.DMA,
        ],
        input_output_aliases=aliases,
        compiler_params=pltpu.CompilerParams(dimension_semantics=("arbitrary", "arbitrary"),
                                             vmem_limit_bytes=_vmem_limit(56)),
        name="mix",
    )(*operands)


def _ffn_kernel(first_ref, nblk_ref, nused_ref, order_ref, short_ref, xs_ref, wgu_ref, bgu_ref, wd_ref, bd_ref,
                ys_ref, wgu_bf, wd_bf, xbuf, ybuf, in_sem, out_sem, *, n_blocks):
    e = pl.program_id(0)
    first = first_ref[e]
    n_used = nused_ref[0]
    block_rows = MOE_BLOCK * ROW_TILES

    def blk(ref, b):
        return ref.at[pl.ds(pl.multiple_of(b * block_rows, block_rows), block_rows), :]

    def in_copy(b, slot):
        return pltpu.make_async_copy(blk(xs_ref, order_ref[b]), xbuf.at[slot], in_sem.at[slot])

    def out_copy(b, slot):
        return pltpu.make_async_copy(ybuf.at[slot], blk(ys_ref, order_ref[b]), out_sem.at[slot])

    @pl.when(e == 0)
    def _():
        in_copy(0, 0).start()
        ybuf[...] = jnp.zeros(ybuf.shape, F32)

    @pl.when(nblk_ref[e] > 0)
    def _():
        wgu_bf[...] = wgu_ref[0, 0].astype(BF16)
        wd_bf[...] = wd_ref[0, 0].astype(BF16)

    def compute(slot, rows):
        x = _from_row_tiles(xbuf.at[slot], 0, rows).astype(BF16)
        gu = jnp.dot(x, wgu_bf[...], preferred_element_type=F32) + bgu_ref[0, 0]
        g = jnp.minimum(gu[:, :D_FF], SWIGLU_LIMIT)
        lin = jnp.clip(gu[:, D_FF:], -SWIGLU_LIMIT, SWIGLU_LIMIT)
        a = g * jax.nn.sigmoid(SWIGLU_ALPHA * g) * (lin + 1.0)
        y = jnp.dot(a.astype(BF16), wd_bf[...], preferred_element_type=F32) + bd_ref[0, 0]
        _to_row_tiles(ybuf.at[slot], y, rows)

    def body(j, carry):
        b = first + j
        slot = lax.rem(b, 2)
        in_copy(b, slot).wait()

        @pl.when(b + 1 < n_used)
        def _():
            in_copy(b + 1, 1 - slot).start()

        @pl.when(b >= 2)
        def _():
            out_copy(b - 2, slot).wait()

        short = jnp.logical_and(j == nblk_ref[e] - 1, short_ref[e] == 1)

        @pl.when(short)
        def _():
            compute(slot, MOE_BLOCK // 2)

        @pl.when(jnp.logical_not(short))
        def _():
            compute(slot, MOE_BLOCK)

        out_copy(b, slot).start()
        return carry

    lax.fori_loop(0, nblk_ref[e], body, 0)

    @pl.when(e == N_EXPERTS - 1)
    def _():
        @pl.when(n_used >= 2)
        def _():
            out_copy(n_used - 2, lax.rem(n_used, 2)).wait()

        out_copy(n_used - 1, lax.rem(n_used - 1, 2)).wait()
        ybuf[0] = jnp.zeros(ybuf.shape[1:], F32)
        lax.fori_loop(n_used, n_blocks, lambda b, c: (out_copy(b, 0).start(), c)[1], 0)
        lax.fori_loop(n_used, n_blocks, lambda b, c: (out_copy(b, 0).wait(), c)[1], 0)


def _ffn(layer, first_blk, n_blk, n_used, order, short, xs, w_gu, b_gu, w_down, b_down, n_blocks):
    block_rows = MOE_BLOCK * ROW_TILES
    depth = w_gu.shape[0]

    def expert(e, *_):
        return (layer, e, 0, 0)

    return pl.pallas_call(
        functools.partial(_ffn_kernel, n_blocks=n_blocks),
        out_shape=jax.ShapeDtypeStruct((n_blocks * block_rows, LANES), F32),
        grid_spec=pltpu.PrefetchScalarGridSpec(
            num_scalar_prefetch=5,
            grid=(N_EXPERTS,),
            in_specs=[
                pl.BlockSpec(memory_space=pl.ANY),
                pl.BlockSpec((1, 1, D_MODEL, 2 * D_FF), expert),
                pl.BlockSpec((1, 1, 1, 2 * D_FF), expert),
                pl.BlockSpec((1, 1, D_FF, D_MODEL), expert),
                pl.BlockSpec((1, 1, 1, D_MODEL), expert),
            ],
            out_specs=pl.BlockSpec(memory_space=pl.ANY),
            scratch_shapes=[
                pltpu.VMEM((D_MODEL, 2 * D_FF), BF16),
                pltpu.VMEM((D_FF, D_MODEL), BF16),
                pltpu.VMEM((2, block_rows, LANES), F32),
                pltpu.VMEM((2, block_rows, LANES), F32),
                pltpu.SemaphoreType.DMA((2,)),
                pltpu.SemaphoreType.DMA((2,)),
            ],
        ),
        compiler_params=pltpu.CompilerParams(dimension_semantics=("arbitrary",),
                                             vmem_limit_bytes=_vmem_limit(56)),
        name="ffn",
    )(first_blk, n_blk, n_used, order, short, xs, w_gu, b_gu.reshape(depth, N_EXPERTS, 1, 2 * D_FF), w_down,
      b_down.reshape(depth, N_EXPERTS, 1, D_MODEL))


def _combine_kernel(dest_ref, next_ref, x1_ref, mod_ref, gate_ref, ln2g_ref, ln2b_ref, ys_ref, o_ref,
                    buf0, buf1, sem0, sem1, *, tile, alpha, n_steps):
    t = tile
    step = pl.program_id(0) * pl.num_programs(1) + pl.program_id(1)

    def fetch(d_ref, g, tt, buf, sem):
        tok = g * DEST_GROUP + tt
        for k in range(TOP_K):
            pltpu.make_async_copy(_row(ys_ref, d_ref[g, k, tt]), _row(buf, k * t + tok),
                                  sem).start(priority=k % 2)

    def wait(buf, sem):
        pltpu.make_async_copy(ys_ref.at[pl.ds(0, TOP_K * t * ROW_TILES), :], buf, sem).wait()

    @pl.when(step == 0)
    def _():
        for g in range(t // DEST_GROUP):
            lax.fori_loop(0, DEST_GROUP, lambda tt, c, g=g: (fetch(dest_ref, g, tt, buf0, sem0), c)[1], 0)

    def phase(buf, sem, nxt_buf, nxt_sem):
        wait(buf, sem)
        for g in range(t // DEST_GROUP):
            for tt in range(DEST_GROUP):
                fetch(next_ref, g, tt, nxt_buf, nxt_sem)
        gates = gate_ref[0]
        gates_t = jnp.transpose(jnp.concatenate([gates, jnp.zeros_like(gates)], axis=0))
        ff = jnp.zeros((t, D_MODEL), F32)
        for k in range(TOP_K):
            ff = ff + gates_t[:, k:k + 1] * _from_row_tiles(buf, k * t, t)
        g2 = mod_ref[0][5:6, :]
        o_ref[0] = _layernorm(alpha * x1_ref[0] + (1.0 + g2) * ff, ln2g_ref[...], ln2b_ref[...])

    @pl.when(step % 2 == 0)
    def _():
        phase(buf0, sem0, buf1, sem1)

    @pl.when(step % 2 == 1)
    def _():
        phase(buf1, sem1, buf0, sem0)

    @pl.when(step == n_steps - 1)
    def _():
        if (n_steps - 1) % 2 == 0:
            wait(buf1, sem1)
        else:
            wait(buf0, sem0)


def _combine(dest, group_offset, x1, mod, gates, ln2_g, ln2_b, ys, *, tile, alpha):
    b, s, d = x1.shape
    n_t = s // tile
    n_steps = b * n_t
    groups = tile // DEST_GROUP
    goff = group_offset // groups
    per_gate_row = gates.shape[2] // tile
    kern = functools.partial(_combine_kernel, tile=tile, alpha=alpha, n_steps=n_steps)
    buf = pltpu.VMEM((TOP_K * tile * ROW_TILES, LANES), F32)
    return pl.pallas_call(
        kern,
        out_shape=jax.ShapeDtypeStruct((b, s, d), F32),
        grid=(b, n_t),
        in_specs=[
            pl.BlockSpec((groups, TOP_K, DEST_GROUP), lambda bb, ii: (goff + bb * n_t + ii, 0, 0),
                         memory_space=pltpu.SMEM),
            pl.BlockSpec((groups, TOP_K, DEST_GROUP),
                         lambda bb, ii: (goff + jnp.minimum(bb * n_t + ii + 1, n_steps - 1), 0, 0),
                         memory_space=pltpu.SMEM),
            pl.BlockSpec((1, tile, d), lambda bb, ii: (bb, ii, 0)),
            pl.BlockSpec((1, 6, d), lambda bb, ii: (bb, 0, 0)),
            pl.BlockSpec((1, TOP_K, tile), lambda bb, ii: ((bb * n_t + ii) // per_gate_row, 0,
                                                           (bb * n_t + ii) % per_gate_row)),
            pl.BlockSpec((1, d), lambda bb, ii: (0, 0)),
            pl.BlockSpec((1, d), lambda bb, ii: (0, 0)),
            pl.BlockSpec(memory_space=pl.ANY),
        ],
        out_specs=pl.BlockSpec((1, tile, d), lambda bb, ii: (bb, ii, 0)),
        scratch_shapes=[buf, buf, pltpu.SemaphoreType.DMA, pltpu.SemaphoreType.DMA],
        compiler_params=pltpu.CompilerParams(dimension_semantics=("arbitrary", "arbitrary"),
                                             vmem_limit_bytes=_vmem_limit(48)),
        name="combine",
    )(dest, dest, x1, mod, gates, ln2_g, ln2_b, ys)


def kernel(x_prompt, x_sample, c_prompt, c_sample, cache_k, cache_v, state_conv, rel_table, ln0_g, ln0_b, w_ada, b_ada, w_in, b_in, sinks, conv_w, w_oa, w_ob, w_o, ln1_g, ln1_b, w_router, b_router, w_gu, b_gu, w_down, b_down, ln2_g, ln2_b):
    depth = w_ada.shape[0]
    bp, sp, d = x_prompt.shape
    bs, ss, _ = x_sample.shape
    alpha = (2 * depth) ** 0.25
    n_tok = bp * sp + bs * ss
    assert sp % PROMPT_TILE == 0 and ss % DEST_GROUP == 0 and ss <= WINDOW
    assert -(-n_tok // MOE_BLOCK) <= TABLE_LANES
    n_blocks = -(-(n_tok * TOP_K) // MOE_BLOCK) + N_EXPERTS

    mod_all = _ada(jnp.concatenate([c_prompt, c_sample], axis=0), w_ada, b_ada)
    mod_all = mod_all.reshape(depth, bp + bs, 6, d)
    bias = _bias_table(rel_table)
    row = lambda a: a.reshape(1, -1)
    zeros_kv = jnp.zeros((bp, WINDOW, KV_WIDTH), F32)
    zeros_u = jnp.zeros((bp, SUBLANES, CONV_WIDTH), F32)
    state0 = jnp.zeros((N_EXPERTS, LANES), F32)
    table0 = jnp.zeros((N_EXPERTS, TABLE_LANES), F32)
    slot = jnp.arange(n_blocks, dtype=jnp.int32)

    y_p, y_s = x_prompt, x_sample
    outs = {name: [] for name in ("kp", "vp", "up", "ks", "vs", "us")}
    for l in range(depth):
        shared = (w_in[l].astype(BF16), row(b_in[l]))
        tail = (sinks[l], conv_w[l], w_oa[l].astype(BF16), w_ob[l].astype(BF16), w_o[l].astype(BF16),
                row(ln1_g[l]), row(ln1_b[l]), w_router[l].T.astype(BF16), b_router[l].reshape(N_EXPERTS, 1))
        mod_p, mod_s = mod_all[l, :bp], mod_all[l, bp:]
        x1_p, gate_p, dest_p, k_p, v_p, u_p, state, table, xs = _mix(
            y_p, mod_p, row(ln0_g), row(ln0_b), *shared, bias, *tail, zeros_kv, zeros_kv, zeros_u,
            state0, table0, None, tile=PROMPT_TILE, apply_ln0=(l == 0), mask_first=True, alpha=alpha,
            n_blocks=n_blocks, finalize=True)
        u0 = jnp.pad(state_conv[l], ((0, 0), (SUBLANES - (CONV_K - 1), 0), (0, 0)))
        x1_s, gate_s, dest_s, k_s, v_s, u_s, state, table, xs = _mix(
            y_s, mod_s, row(ln0_g), row(ln0_b), *shared, bias, *tail,
            cache_k[l].reshape(bs, WINDOW, KV_WIDTH), cache_v[l].reshape(bs, WINDOW, KV_WIDTH), u0,
            state, table, xs, tile=ss, apply_ln0=(l == 0), mask_first=False, alpha=alpha,
            n_blocks=n_blocks, finalize=False)

        counts = state[:, 0].astype(jnp.int32)
        n_used = state[0:1, 2].astype(jnp.int32)
        n_blk = (counts + MOE_BLOCK - 1) // MOE_BLOCK
        blk_end = jnp.cumsum(n_blk)
        first_blk = blk_end - n_blk
        owner = jnp.minimum(jnp.sum((blk_end[None, :] <= slot[:, None]).astype(jnp.int32), axis=1), N_EXPERTS - 1)
        nth = jnp.clip(slot - first_blk[owner], 0, TABLE_LANES - 1)
        order = jnp.where(slot < n_used, table[owner, nth].astype(jnp.int32), slot)

        last_rows = counts - (n_blk - 1) * MOE_BLOCK
        short = jnp.logical_and(n_blk > 0, last_rows <= MOE_BLOCK // 2).astype(jnp.int32)

        ys = _ffn(l, first_blk, n_blk, n_used, order, short, xs, w_gu, b_gu, w_down, b_down, n_blocks)
        y_p = _combine(dest_p, 0, x1_p, mod_p, gate_p, row(ln2_g[l]), row(ln2_b[l]), ys,
                       tile=COMBINE_TILE, alpha=alpha)
        y_s = _combine(dest_s, 0, x1_s, mod_s, gate_s, row(ln2_g[l]), row(ln2_b[l]), ys,
                       tile=ss, alpha=alpha)

        outs["kp"].append(k_p.reshape(bp, WINDOW, N_KV_HEADS, HEAD_DIM))
        outs["vp"].append(v_p.reshape(bp, WINDOW, N_KV_HEADS, HEAD_DIM))
        outs["up"].append(u_p[:, SUBLANES - (CONV_K - 1):, :])
        outs["ks"].append(k_s.reshape(bs, WINDOW, N_KV_HEADS, HEAD_DIM))
        outs["vs"].append(v_s.reshape(bs, WINDOW, N_KV_HEADS, HEAD_DIM))
        outs["us"].append(u_s[:, SUBLANES - (CONV_K - 1):, :])
    return (y_p, y_s, jnp.stack(outs["kp"]), jnp.stack(outs["vp"]), jnp.stack(outs["up"]),
            jnp.stack(outs["ks"]), jnp.stack(outs["vs"]), jnp.stack(outs["us"]))
```
